```python
import jax
import jax.numpy as jnp
from jax import lax
import numpy as np

D_MODEL = 1024
BATCH = 16
SEQ = 2048
DEPTH = 4

GRID_W = 64
CTX_LEN = 256
NORM_EPS = 1e-6

ATT_HEADS = 8
ATT_KV_HEADS = 2
ATT_HEAD_DIM = 64
ATT_GROUP = ATT_HEADS // ATT_KV_HEADS
ATT_WIDTH = ATT_HEADS * ATT_HEAD_DIM
ATT_KV_WIDTH = ATT_KV_HEADS * ATT_HEAD_DIM
ROPE_THETA = 10000.0
Q_BLOCK = 128

RWKV_HEADS = 8
RWKV_HEAD_DIM = 64
RWKV_WIDTH = RWKV_HEADS * RWKV_HEAD_DIM
DECAY_RANK = 64
ICLR_RANK = 64
GATE_RANK = 128
RWKV_GN_EPS = 64e-5

CONV_WIDTH = 512
CONV_K = 3

N_BRANCHES = 3

N_EXPERTS = 16
EXPERT_FF = 1024
CAPACITY_FACTOR = 2

ATT_SPLITS = [ATT_WIDTH, ATT_KV_WIDTH, ATT_KV_WIDTH]
RWKV_SPLITS = [RWKV_WIDTH, RWKV_WIDTH, RWKV_WIDTH, 2 * DECAY_RANK, 2 * ICLR_RANK, GATE_RANK]
CONV_SPLITS = [CONV_WIDTH, CONV_WIDTH, CONV_WIDTH]
GROUP_SPLITS = [sum(ATT_SPLITS), sum(RWKV_SPLITS), sum(CONV_SPLITS), N_BRANCHES * D_MODEL]
RWKV_SEG = sum(RWKV_SPLITS)
IN_WIDTH = sum(GROUP_SPLITS)

kernel_name = 'hybrid_flow_rwkv_gqa_conv_ecmoe'


def _split(x, sizes):
    return jnp.split(x, np.cumsum(sizes)[:-1].tolist(), axis=-1)


def _rms_norm(x, gain):
    xf = x.astype(jnp.float32)
    y = xf * lax.rsqrt(jnp.mean(xf * xf, axis=-1, keepdims=True) + NORM_EPS)
    return (y * gain.astype(jnp.float32)).astype(x.dtype)


def _modulate(x, gain, shift, scale):
    return _rms_norm(x, gain) * (1 + scale) + shift


def _neighbours(u):
    up = jnp.pad(u, ((0, 0), (1, 1), (0, 0)))
    return up[:, :-2], up[:, 2:]


def _axial_rope_tables(n_tokens, dtype):
    rows = n_tokens // GRID_W
    row = jnp.repeat(jnp.arange(rows, dtype=jnp.float32), GRID_W)
    col = jnp.tile(jnp.arange(GRID_W, dtype=jnp.float32), rows)
    axis_dim = ATT_HEAD_DIM // 2
    inv_freq = ROPE_THETA ** (-jnp.arange(0, axis_dim, 2, dtype=jnp.float32) / axis_dim)
    ang_r = row[:, None] * inv_freq[None, :]
    ang_c = col[:, None] * inv_freq[None, :]
    return (jnp.cos(ang_r).astype(dtype), jnp.sin(ang_r).astype(dtype),
            jnp.cos(ang_c).astype(dtype), jnp.sin(ang_c).astype(dtype))


def _rotate_half(x, cos, sin):
    x1, x2 = jnp.split(x, 2, axis=-1)
    cos = cos[:, None, :]
    sin = sin[:, None, :]
    return jnp.concatenate([x1 * cos - x2 * sin, x1 * sin + x2 * cos], axis=-1)


def _axial_rotary(x, tables):
    cos_r, sin_r, cos_c, sin_c = tables
    x_row, x_col = jnp.split(x, 2, axis=-1)
    return jnp.concatenate([_rotate_half(x_row, cos_r, sin_r), _rotate_half(x_col, cos_c, sin_c)], axis=-1)


def _attend(q, k, v):
    s = jnp.einsum('bqkgd,bskd->bkgqs', q, k).astype(jnp.float32) * (ATT_HEAD_DIM ** -0.5)
    p = jax.nn.softmax(s, axis=-1).astype(v.dtype)
    return jnp.einsum('bkgqs,bskd->bqkgd', p, v)


def _latent_attention(q, k_all, v_all):
    b, t = q.shape[:2]
    n_blocks = t // Q_BLOCK
    q_blocks = jnp.moveaxis(q.reshape(b, n_blocks, Q_BLOCK, ATT_KV_HEADS, ATT_GROUP, ATT_HEAD_DIM), 1, 0)
    o = lax.map(lambda q_blk: _attend(q_blk, k_all, v_all), q_blocks)
    return jnp.moveaxis(o, 0, 1).reshape(b, t, ATT_WIDTH)


def _attention_branch(att_lat, att_ctx, q_gain, k_gain, need_ctx):
    def heads(seg):
        b, t = seg.shape[:2]
        q, k, v = _split(seg, ATT_SPLITS)
        q = _rms_norm(q.reshape(b, t, ATT_HEADS, ATT_HEAD_DIM), q_gain)
        k = _rms_norm(k.reshape(b, t, ATT_KV_HEADS, ATT_HEAD_DIM), k_gain)
        return q, k, v.reshape(b, t, ATT_KV_HEADS, ATT_HEAD_DIM)

    b, t = att_lat.shape[:2]
    q_l, k_l, v_l = heads(att_lat)
    q_c, k_c, v_c = heads(att_ctx)
    tables = _axial_rope_tables(t, q_l.dtype)
    q_l = _axial_rotary(q_l, tables).reshape(b, t, ATT_KV_HEADS, ATT_GROUP, ATT_HEAD_DIM)
    k_l = _axial_rotary(k_l, tables)
    k_all = jnp.concatenate([k_c, k_l], axis=1)
    v_all = jnp.concatenate([v_c, v_l], axis=1)
    o_lat = _latent_attention(q_l, k_all, v_all)
    if not need_ctx:
        return o_lat, None
    n_ctx = att_ctx.shape[1]
    q_c = q_c.reshape(b, n_ctx, ATT_KV_HEADS, ATT_GROUP, ATT_HEAD_DIM)
    o_ctx = _attend(q_c, k_c, v_c).reshape(b, n_ctx, ATT_WIDTH)
    return o_lat, o_ctx


def _to_heads(u):
    return u.reshape(u.shape[:-1] + (RWKV_HEADS, RWKV_HEAD_DIM))


def _l2_heads(u):
    uf = u.astype(jnp.float32)
    return (uf * lax.rsqrt(jnp.sum(uf * uf, axis=-1, keepdims=True) + 1e-12)).astype(u.dtype)


def _rwkv_prepare(seg, mu, decay_w0, decay_w2, iclr_a0, iclr_a2, gate_g2, k_k, k_a):
    b, t, _ = seg.shape
    prev, nxt = _neighbours(seg)
    seg = seg + mu * (0.5 * (prev + nxt) - seg)
    r, k, v, w_lo, a_lo, g_lo = _split(seg, RWKV_SPLITS)
    w_pre = decay_w0 + jnp.einsum('btdr,drc->btdc', jnp.tanh(w_lo.reshape(b, t, 2, DECAY_RANK)), decay_w2)
    decay = jnp.exp(-jnp.exp(-jax.nn.softplus(-w_pre) - 0.5))
    a = jax.nn.sigmoid(iclr_a0 + jnp.einsum('btdr,drc->btdc', a_lo.reshape(b, t, 2, ICLR_RANK), iclr_a2))
    g = jax.nn.sigmoid(g_lo) @ gate_g2
    kk = _l2_heads(_to_heads(k * k_k))
    k_dir = k[:, :, None, :] * (1 + (a - 1) * k_a)
    return (_to_heads(r), _to_heads(k), _to_heads(v), g, _to_heads(decay), _to_heads(k_dir), kk, _to_heads(a))


def _time_major(x_fwd, x_bwd):
    return jnp.moveaxis(jnp.stack([x_fwd, jnp.flip(x_bwd, axis=1)], axis=0), 2, 0)


def _rwkv_bidir_scan(r, v, decay, k_dir, kk, a, s0):
    b_vec = kk[:, :, None] * a
    xs = (_time_major(r, r), _time_major(decay[:, :, 0], decay[:, :, 1]),
          _time_major(k_dir[:, :, 0], k_dir[:, :, 1]), _time_major(v, v),
          _time_major(-kk, -kk), _time_major(b_vec[:, :, 0], b_vec[:, :, 1]))

    def step(state, inp):
        r_t, w_t, k_t, v_t, a_t, b_t = inp
        sa = jnp.einsum('dbhvk,dbhk->dbhv', state, a_t)
        state = state * w_t[..., None, :] + sa[..., :, None] * b_t[..., None, :] + v_t[..., :, None] * k_t[..., None, :]
        return state, jnp.einsum('dbhvk,dbhk->dbhv', state, r_t)

    s_final, ys = lax.scan(step, s0, xs)
    y = ys[:, 0] + jnp.flip(ys[:, 1], axis=0)
    return jnp.moveaxis(y, 0, 1), s_final


def _rwkv_readout(y, r, k, v, g, r_k, gn_w, gn_b):
    b, t = y.shape[:2]
    yf = y.astype(jnp.float32)
    mean = jnp.mean(yf, axis=-1, keepdims=True)
    var = jnp.mean(jnp.square(yf - mean), axis=-1, keepdims=True)
    yn = ((yf - mean) * lax.rsqrt(var + RWKV_GN_EPS)).astype(y.dtype).reshape(b, t, RWKV_WIDTH)
    bonus = jnp.sum(r * k * _to_heads(r_k), axis=-1, keepdims=True) * v
    return (yn * gn_w + gn_b + bonus.reshape(b, t, RWKV_WIDTH)) * g


def _rwkv_branch(seg_lat, seg_ctx, need_ctx, mu, decay_w0, decay_w2, iclr_a0, iclr_a2, gate_g2, k_k, k_a, r_k, gn_w, gn_b):
    r_c, k_c, v_c, g_c, w_c, kd_c, kk_c, a_c = _rwkv_prepare(seg_ctx, mu, decay_w0, decay_w2, iclr_a0, iclr_a2, gate_g2, k_k, k_a)
    s0 = jnp.zeros((2, seg_ctx.shape[0], RWKV_HEADS, RWKV_HEAD_DIM, RWKV_HEAD_DIM), r_c.dtype)
    y_c, s_ctx = _rwkv_bidir_scan(r_c, v_c, w_c, kd_c, kk_c, a_c, s0)
    r_l, k_l, v_l, g_l, w_l, kd_l, kk_l, a_l = _rwkv_prepare(seg_lat, mu, decay_w0, decay_w2, iclr_a0, iclr_a2, gate_g2, k_k, k_a)
    y_l, _ = _rwkv_bidir_scan(r_l, v_l, w_l, kd_l, kk_l, a_l, s_ctx)
    o_lat = _rwkv_readout(y_l, r_l, k_l, v_l, g_l, r_k, gn_w, gn_b)
    if not need_ctx:
        return o_lat, None
    return o_lat, _rwkv_readout(y_c, r_c, k_c, v_c, g_c, r_k, gn_w, gn_b)


def _short_conv(seg, conv_w):
    b_gate, c_gate, u = _split(seg, CONV_SPLITS)
    z = c_gate * u
    z_prev, z_next = _neighbours(z)
    return b_gate * (conv_w[0] * z_prev + conv_w[1] * z + conv_w[2] * z_next)


def _mixer(h_lat, h_ctx, need_ctx, w_in, q_gain, k_gain, shift_mu, decay_w0, decay_w2, iclr_a0, iclr_a2,
           gate_g2, rwkv_kk, rwkv_ka, rwkv_rk, rwkv_gn_w, rwkv_gn_b, conv_w, w_br_att, w_br_rwkv, w_br_conv, w_out):
    att_l, rw_l, cv_l, gt_l = _split(h_lat @ w_in, GROUP_SPLITS)
    att_c, rw_c, cv_c, gt_c = _split(h_ctx @ w_in, GROUP_SPLITS)
    o_att_l, o_att_c = _attention_branch(att_l, att_c, q_gain, k_gain, need_ctx)
    o_rw_l, o_rw_c = _rwkv_branch(rw_l, rw_c, need_ctx, shift_mu, decay_w0, decay_w2, iclr_a0, iclr_a2,
                                  gate_g2, rwkv_kk, rwkv_ka, rwkv_rk, rwkv_gn_w, rwkv_gn_b)

    def merge(gates, o_att, o_rwkv, o_conv):
        g_att, g_rwkv, g_conv = jnp.split(jax.nn.sigmoid(gates), N_BRANCHES, axis=-1)
        m = g_att * (o_att @ w_br_att) + g_rwkv * (o_rwkv @ w_br_rwkv) + g_conv * (o_conv @ w_br_conv)
        return m @ w_out

    y_lat = merge(gt_l, o_att_l, o_rw_l, _short_conv(cv_l, conv_w))
    if not need_ctx:
        return y_lat, None
    return y_lat, merge(gt_c, o_att_c, o_rw_c, _short_conv(cv_c, conv_w))


def _expert_choice_ffn(h, w_router, w_gate, w_up, w_down):
    b, n, d = h.shape
    cap = CAPACITY_FACTOR * n // N_EXPERTS
    aff = jax.nn.softmax(jnp.einsum('bnd,de->bne', h, w_router).astype(jnp.float32), axis=-1)
    gate, idx = lax.top_k(jnp.swapaxes(aff, 1, 2), cap)
    batch_idx = jnp.arange(b)[:, None, None]
    xe = h[batch_idx, idx]
    hid = jax.nn.silu(jnp.einsum('becd,edf->becf', xe, w_gate)) * jnp.einsum('becd,edf->becf', xe, w_up)
    ye = jnp.einsum('becf,efd->becd', hid, w_down) * gate[..., None].astype(h.dtype)
    flat = (batch_idx * n + idx).reshape(-1)
    out = jnp.zeros((b * n, d), h.dtype).at[flat].add(ye.reshape(-1, d))
    return out.reshape(b, n, d)


def setup_inputs(seed: int = 0) -> dict:
    key = jax.random.key(seed)
    ks = jax.random.split(key, 40)
    f32 = jnp.float32
    L = DEPTH
    D = D_MODEL

    def nrm(k, shape, scale):
        return jax.random.normal(k, shape, f32) * scale

    def gain(k, shape, base=1.0):
        return base + 0.05 * jax.random.normal(k, shape, f32)

    return {
        'x': nrm(ks[0], (BATCH, SEQ, D), 1.0),
        'c': nrm(ks[1], (BATCH, D), 1.0),
        'ctx': nrm(ks[2], (BATCH, CTX_LEN, D), 1.0),
        'c_ctx': nrm(ks[3], (D,), 1.0),
        'ada_w': nrm(ks[4], (L, D, 6 * D), 0.5 * D ** -0.5),
        'ada_b': nrm(ks[5], (L, 6 * D), 0.02),
        'norm1': gain(ks[6], (L, D)),
        'w_in': nrm(ks[7], (L, D, IN_WIDTH), D ** -0.5),
        'q_gain': gain(ks[8], (L, ATT_HEAD_DIM)),
        'k_gain': gain(ks[9], (L, ATT_HEAD_DIM)),
        'shift_mu': jax.random.uniform(ks[10], (L, RWKV_SEG), f32),
        'decay_w0': jax.random.uniform(ks[11], (L, 2, RWKV_WIDTH), f32, -4.0, 1.0),
        'decay_w2': nrm(ks[12], (L, 2, DECAY_RANK, RWKV_WIDTH), 0.1 * DECAY_RANK ** -0.5),
        'iclr_a0': nrm(ks[13], (L, 2, RWKV_WIDTH), 0.5),
        'iclr_a2': nrm(ks[14], (L, 2, ICLR_RANK, RWKV_WIDTH), 0.5 * ICLR_RANK ** -0.5),
        'gate_g2': nrm(ks[15], (L, GATE_RANK, RWKV_WIDTH), GATE_RANK ** -0.5),
        'rwkv_kk': gain(ks[16], (L, RWKV_WIDTH), 0.85),
        'rwkv_ka': gain(ks[17], (L, RWKV_WIDTH)),
        'rwkv_rk': nrm(ks[18], (L, RWKV_WIDTH), 0.1),
        'rwkv_gn_w': gain(ks[19], (L, RWKV_WIDTH)),
        'rwkv_gn_b': nrm(ks[20], (L, RWKV_WIDTH), 0.02),
        'conv_w': nrm(ks[21], (L, CONV_K, CONV_WIDTH), CONV_K ** -0.5),
        'w_br_att': nrm(ks[22], (L, ATT_WIDTH, D), ATT_WIDTH ** -0.5),
        'w_br_rwkv': nrm(ks[23], (L, RWKV_WIDTH, D), RWKV_WIDTH ** -0.5),
        'w_br_conv': nrm(ks[24], (L, CONV_WIDTH, D), CONV_WIDTH ** -0.5),
        'w_out': nrm(ks[25], (L, D, D), D ** -0.5),
        'norm2': gain(ks[26], (L, D)),
        'w_router': nrm(ks[27], (L, D, N_EXPERTS), D ** -0.5),
        'exp_gate': nrm(ks[28], (L, N_EXPERTS, D, EXPERT_FF), D ** -0.5),
        'exp_up': nrm(ks[29], (L, N_EXPERTS, D, EXPERT_FF), D ** -0.5),
        'exp_down': nrm(ks[30], (L, N_EXPERTS, EXPERT_FF, D), EXPERT_FF ** -0.5),
        'final_norm': gain(ks[31], (D,)),
    }


def reference(x, c, ctx, c_ctx, ada_w, ada_b, norm1, w_in, q_gain, k_gain, shift_mu, decay_w0, decay_w2,
              iclr_a0, iclr_a2, gate_g2, rwkv_kk, rwkv_ka, rwkv_rk, rwkv_gn_w, rwkv_gn_b, conv_w,
              w_br_att, w_br_rwkv, w_br_conv, w_out, norm2, w_router, exp_gate, exp_up, exp_down, final_norm):
    silu_c = jax.nn.silu(c)
    silu_cc = jax.nn.silu(c_ctx)
    for layer in range(DEPTH):
        need_ctx = layer < DEPTH - 1
        mod_l = (silu_c @ ada_w[layer] + ada_b[layer])[:, None, :]
        mod_c = silu_cc @ ada_w[layer] + ada_b[layer]
        sh1, sc1, g1, sh2, sc2, g2 = jnp.split(mod_l, 6, axis=-1)
        sh1c, sc1c, g1c, sh2c, sc2c, g2c = jnp.split(mod_c, 6, axis=-1)

        h_l = _modulate(x, norm1[layer], sh1, sc1)
        h_c = _modulate(ctx, norm1[layer], sh1c, sc1c)
        y_l, y_c = _mixer(h_l, h_c, need_ctx, w_in[layer], q_gain[layer], k_gain[layer], shift_mu[layer],
                          decay_w0[layer], decay_w2[layer], iclr_a0[layer], iclr_a2[layer], gate_g2[layer],
                          rwkv_kk[layer], rwkv_ka[layer], rwkv_rk[layer], rwkv_gn_w[layer], rwkv_gn_b[layer],
                          conv_w[layer], w_br_att[layer], w_br_rwkv[layer], w_br_conv[layer], w_out[layer])
        x = x + g1 * y_l
        h_l = _modulate(x, norm2[layer], sh2, sc2)
        x = x + g2 * _expert_choice_ffn(h_l, w_router[layer], exp_gate[layer], exp_up[layer], exp_down[layer])

        if need_ctx:
            ctx = ctx + g1c * y_c
            h_c = _modulate(ctx, norm2[layer], sh2c, sc2c)
            ctx = ctx + g2c * _expert_choice_ffn(h_c, w_router[layer], exp_gate[layer], exp_up[layer], exp_down[layer])
    return _rms_norm(x, final_norm)
```

```python
import functools

import numpy as np
import jax
import jax.numpy as jnp
from jax import lax
from jax.experimental import pallas as pl
from jax.experimental.pallas import tpu as pltpu

F32 = jnp.float32
BF16 = jnp.bfloat16

GRID_W = 64
NORM_EPS = 1e-6
ATT_HEADS = 8
ATT_KV_HEADS = 2
HEAD_DIM = 64
ATT_GROUP = ATT_HEADS // ATT_KV_HEADS
ATT_WIDTH = ATT_HEADS * HEAD_DIM
ATT_KV_WIDTH = ATT_KV_HEADS * HEAD_DIM
ROPE_THETA = 10000.0
RWKV_HEADS = 8
RWKV_WIDTH = RWKV_HEADS * HEAD_DIM
DECAY_RANK = 64
ICLR_RANK = 64
GATE_RANK = 128
RWKV_GN_EPS = 64e-5
RWKV_SEG = 3 * RWKV_WIDTH + 2 * DECAY_RANK + 2 * ICLR_RANK + GATE_RANK
CONV_WIDTH = 512
N_EXPERTS = 16
CAPACITY_FACTOR = 2

ROW_TILE = 256
CHUNK = 64
QUAD = 4 * HEAD_DIM
HALO = 8
SELECT_BLOCK = 256
VMEM_LIMIT = 56 * 1024 * 1024

NT = (((1,), (1,)), ((), ()))
TN = (((0,), (0,)), ((), ()))
NN = (((1,), (0,)), ((), ()))


def _params(*sem):
    return pltpu.CompilerParams(dimension_semantics=sem, vmem_limit_bytes=VMEM_LIMIT)


def _dg(a, b, dn=NN):
    return lax.dot_general(a, b, dn, preferred_element_type=F32)


def _split2(x):
    hi = x.astype(BF16)
    lo = (x - hi.astype(F32)).astype(BF16)
    return hi, lo


def _split3(x):
    hi = x.astype(BF16)
    r = x - hi.astype(F32)
    mid = r.astype(BF16)
    lo = (r - mid.astype(F32)).astype(BF16)
    return hi, mid, lo


def _mm1(a, b, dn=NN):
    return _dg(a.astype(BF16), b.astype(BF16), dn)


def _mm3(a, b, dn=NN):
    ah, al = _split2(a)
    bh, bl = _split2(b)
    return _dg(ah, bh, dn) + (_dg(ah, bl, dn) + _dg(al, bh, dn))


def _mm_exact_lhs(a_bf16, b, dn=NN):
    h, m, l = _split3(b)
    return _dg(a_bf16, h, dn) + (_dg(a_bf16, m, dn) + _dg(a_bf16, l, dn))


def _mm_exact_rhs(a, b_bf16, dn=NN):
    h, m, l = _split3(a)
    return _dg(h, b_bf16, dn) + (_dg(m, b_bf16, dn) + _dg(l, b_bf16, dn))


def _sigmoid(x):
    return 1.0 / (1.0 + jnp.exp(-x))


def _rms(x):
    return x * lax.rsqrt(jnp.mean(x * x, axis=-1, keepdims=True) + NORM_EPS)


def _ada_kernel(c_ref, w_ref, b_ref, o_ref):
    c = c_ref[...]
    s = c * _sigmoid(c)
    o_ref[0] = _mm3(s, w_ref[0]) + b_ref[0]


def _ada_table(cc, ada_w, ada_b):
    n_layers, d, six_d = ada_w.shape
    rows = cc.shape[0]
    tn = six_d // 4
    return pl.pallas_call(
        _ada_kernel,
        grid=(n_layers, six_d // tn),
        in_specs=[pl.BlockSpec((rows, d), lambda l, j: (0, 0)),
                  pl.BlockSpec((1, d, tn), lambda l, j: (l, 0, j)),
                  pl.BlockSpec((1, 1, tn), lambda l, j: (l, 0, j))],
        out_specs=pl.BlockSpec((1, rows, tn), lambda l, j: (l, 0, j)),
        out_shape=jax.ShapeDtypeStruct((n_layers, rows, six_d), F32),
        compiler_params=_params("parallel", "parallel"),
        name="ada_table",
    )(cc, ada_w, ada_b.reshape(n_layers, 1, six_d))


def _normmod_mm_kernel(x_ref, gain_ref, mt_ref, w_ref, o_ref):
    h = _rms(x_ref[0]) * gain_ref[...]
    h = h * (1.0 + mt_ref[0, 0, 1:2]) + mt_ref[0, 0, 0:1]
    o_ref[0] = jnp.dot(h.astype(BF16), w_ref[...], preferred_element_type=F32)


def _normmod_mm(xc, gain, mt, w, n_lat_tiles):
    b, s, d = xc.shape
    n = w.shape[1]
    return pl.pallas_call(
        _normmod_mm_kernel,
        grid=(b, s // ROW_TILE),
        in_specs=[pl.BlockSpec((1, ROW_TILE, d), lambda i, j: (i, j, 0)),
                  pl.BlockSpec((1, d), lambda i, j: (0, 0)),
                  pl.BlockSpec((1, 1, 6, d), lambda i, j: (i, j // n_lat_tiles, 0, 0)),
                  pl.BlockSpec((d, n), lambda i, j: (0, 0))],
        out_specs=pl.BlockSpec((1, ROW_TILE, n), lambda i, j: (i, j, 0)),
        out_shape=jax.ShapeDtypeStruct((b, s, n), F32),
        compiler_params=_params("parallel", "parallel"),
        name="normmod_mm",
    )(xc, gain, mt, w)


def _swap16(x):
    w = x.shape[-1]
    lane = lax.broadcasted_iota(jnp.int32, x.shape, x.ndim - 1)
    return jnp.where((lane & 16) == 0, pltpu.roll(x, w - 16, x.ndim - 1), pltpu.roll(x, 16, x.ndim - 1))


def _att_prep_kernel(a_ref, cos_ref, sin_ref, qg_ref, kg_ref, seg_ref, q_ref, k_ref, v_ref):
    a = a_ref[0]
    seg = seg_ref[...]

    def norm_rope(u, gain, cos, sin, segm):
        hi, lo = _split2(u * u)
        ms = (_dg(hi, segm) + _dg(lo, segm)) * (1.0 / HEAD_DIM)
        un = u * lax.rsqrt(ms + NORM_EPS) * gain
        return un * cos + _swap16(un) * sin

    q = norm_rope(a[:, :ATT_WIDTH], qg_ref[...], cos_ref[...], sin_ref[...], seg)
    kw = ATT_KV_WIDTH
    k = norm_rope(a[:, ATT_WIDTH:ATT_WIDTH + kw], kg_ref[...], cos_ref[:, :kw], sin_ref[:, :kw], seg[:kw, :kw])
    q_ref[0] = (q * (HEAD_DIM ** -0.5)).astype(BF16)
    k_ref[0] = k.astype(BF16)
    v_ref[0] = a[:, ATT_WIDTH + kw:].astype(BF16)


def _att_prep(att, cos_t, sin_t, q_gain_t, k_gain_t, seg):
    b, s, w = att.shape
    row = lambda n: pl.BlockSpec((1, ROW_TILE, n), lambda i, j: (i, j, 0))
    return pl.pallas_call(
        _att_prep_kernel,
        grid=(b, s // ROW_TILE),
        in_specs=[row(w),
                  pl.BlockSpec((ROW_TILE, ATT_WIDTH), lambda i, j: (j, 0)),
                  pl.BlockSpec((ROW_TILE, ATT_WIDTH), lambda i, j: (j, 0)),
                  pl.BlockSpec((1, ATT_WIDTH), lambda i, j: (0, 0)),
                  pl.BlockSpec((1, ATT_KV_WIDTH), lambda i, j: (0, 0)),
                  pl.BlockSpec((ATT_WIDTH, ATT_WIDTH), lambda i, j: (0, 0))],
        out_specs=[row(ATT_WIDTH), row(ATT_KV_WIDTH), row(ATT_KV_WIDTH)],
        out_shape=[jax.ShapeDtypeStruct((b, s, ATT_WIDTH), BF16),
                   jax.ShapeDtypeStruct((b, s, ATT_KV_WIDTH), BF16),
                   jax.ShapeDtypeStruct((b, s, ATT_KV_WIDTH), BF16)],
        compiler_params=_params("parallel", "parallel"),
        name="att_prep",
    )(att, cos_t, sin_t, q_gain_t, k_gain_t, seg)


def _attn_kernel(q_ref, k_ref, v_ref, o_ref, *, n_lat, n_lat_tiles):
    j = pl.program_id(1)
    s_all = k_ref.shape[1]

    def run(k0, k1):
        for kvh in range(ATT_KV_HEADS):
            k = k_ref[0, k0:k1, kvh * HEAD_DIM:(kvh + 1) * HEAD_DIM]
            v = v_ref[0, k0:k1, kvh * HEAD_DIM:(kvh + 1) * HEAD_DIM]
            for g in range(ATT_GROUP):
                h = kvh * ATT_GROUP + g
                q = q_ref[0, :, h * HEAD_DIM:(h + 1) * HEAD_DIM]
                s = _dg(q, k, NT)
                p = jnp.exp(s - jnp.max(s, axis=-1, keepdims=True))
                l = jnp.sum(p, axis=-1, keepdims=True)
                o = _dg(p.astype(BF16), v) / l
                o_ref[0, :, h * HEAD_DIM:(h + 1) * HEAD_DIM] = o.astype(BF16)

    @pl.when(j < n_lat_tiles)
    def _():
        run(0, s_all)

    @pl.when(j >= n_lat_tiles)
    def _():
        run(n_lat, s_all)


def _attention(q, k, v, n_lat):
    b, s, _ = q.shape
    return pl.pallas_call(
        functools.partial(_attn_kernel, n_lat=n_lat, n_lat_tiles=n_lat // ROW_TILE),
        grid=(b, s // ROW_TILE),
        in_specs=[pl.BlockSpec((1, ROW_TILE, ATT_WIDTH), lambda i, j: (i, j, 0)),
                  pl.BlockSpec((1, s, ATT_KV_WIDTH), lambda i, j: (i, 0, 0)),
                  pl.BlockSpec((1, s, ATT_KV_WIDTH), lambda i, j: (i, 0, 0))],
        out_specs=pl.BlockSpec((1, ROW_TILE, ATT_WIDTH), lambda i, j: (i, j, 0)),
        out_shape=jax.ShapeDtypeStruct((b, s, ATT_WIDTH), BF16),
        compiler_params=_params("parallel", "parallel"),
        name="attention",
    )(q, k, v)


def _neighbours(x, halo_prev, halo_next, first, last):
    rows = x.shape[0]
    ridx = lax.broadcasted_iota(jnp.int32, x.shape, 0)
    row_p = jnp.where(first, 0.0, halo_prev[HALO - 1:HALO])
    row_n = jnp.where(last, 0.0, halo_next[0:1])
    prev = jnp.where(ridx == 0, row_p, pltpu.roll(x, 1, 0))
    nxt = jnp.where(ridx == rows - 1, row_n, pltpu.roll(x, rows - 1, 0))
    return prev, nxt


def _tile_ends(j, n_lat_tiles, n_tiles):
    first = jnp.logical_or(j == 0, j == n_lat_tiles)
    last = jnp.logical_or(j == n_lat_tiles - 1, j == n_tiles - 1)
    return first, last


def _halo_specs(width, n_tiles):
    per = ROW_TILE // HALO
    prev = pl.BlockSpec((1, HALO, width), lambda i, j: (i, jnp.maximum(j * per - 1, 0), 0))
    nxt = pl.BlockSpec((1, HALO, width), lambda i, j: (i, jnp.minimum((j + 1) * per, n_tiles * per - 1), 0))
    return prev, nxt


def _rwkv_prep_kernel(x_ref, hp_ref, hn_ref, mu_ref, w2_ref, w0_ref, a2_ref, a0_ref, g2_ref, kk_ref, ka_ref,
                      seg_ref, r_ref, k_ref, v_ref, g_ref, nkk_ref, lw_ref, kd_ref, bb_ref, *, n_lat_tiles, n_tiles):
    j = pl.program_id(1)
    first, last = _tile_ends(j, n_lat_tiles, n_tiles)
    x = x_ref[0]
    prev, nxt = _neighbours(x, hp_ref[0], hn_ref[0], first, last)
    xs = x + mu_ref[...] * (0.5 * (prev + nxt) - x)
    w = RWKV_WIDTH
    r = xs[:, 0:w]
    k = xs[:, w:2 * w]
    v = xs[:, 2 * w:3 * w]
    o = 3 * w
    w_lo = xs[:, o:o + 2 * DECAY_RANK]
    a_lo = xs[:, o + 2 * DECAY_RANK:o + 2 * DECAY_RANK + 2 * ICLR_RANK]
    g_lo = xs[:, o + 2 * DECAY_RANK + 2 * ICLR_RANK:]
    w_pre = w0_ref[...] + _mm3(jnp.tanh(w_lo), w2_ref[...])
    logw = -_sigmoid(w_pre) * float(np.exp(-0.5))
    a = _sigmoid(a0_ref[...] + _mm3(a_lo, a2_ref[...]))
    g = _mm3(_sigmoid(g_lo), g2_ref[...])
    kk = k * kk_ref[...]
    kk = kk * lax.rsqrt(_mm_exact_rhs(kk * kk, seg_ref[...]) + 1e-12)
    r_ref[0] = r
    k_ref[0] = k
    v_ref[0] = v
    g_ref[0] = g
    nkk_ref[0] = -kk
    for d in range(2):
        a_d = a[:, d * w:(d + 1) * w]
        lw_ref[0, d] = logw[:, d * w:(d + 1) * w]
        kd_ref[0, d] = k * (1.0 + (a_d - 1.0) * ka_ref[...])
        bb_ref[0, d] = kk * a_d


def _rwkv_prep(rw, mu, w2cat, w0, a2cat, a0, g2, k_k, k_a, seg, n_lat_tiles):
    b, s, wseg = rw.shape
    n_tiles = s // ROW_TILE
    w = RWKV_WIDTH
    hp, hn = _halo_specs(wseg, n_tiles)
    full = lambda shape: pl.BlockSpec(shape, lambda i, j: (0,) * len(shape))
    tok = pl.BlockSpec((1, ROW_TILE, w), lambda i, j: (i, j, 0))
    tok2 = pl.BlockSpec((1, 2, ROW_TILE, w), lambda i, j: (i, 0, j, 0))
    one = jax.ShapeDtypeStruct((b, s, w), F32)
    two = jax.ShapeDtypeStruct((b, 2, s, w), F32)
    return pl.pallas_call(
        functools.partial(_rwkv_prep_kernel, n_lat_tiles=n_lat_tiles, n_tiles=n_tiles),
        grid=(b, n_tiles),
        in_specs=[pl.BlockSpec((1, ROW_TILE, wseg), lambda i, j: (i, j, 0)), hp, hn,
                  full((1, wseg)), full(w2cat.shape), full((1, 2 * w)), full(a2cat.shape), full((1, 2 * w)),
                  full(g2.shape), full((1, w)), full((1, w)), full(seg.shape)],
        out_specs=[tok, tok, tok, tok, tok, tok2, tok2, tok2],
        out_shape=[one, one, one, one, one, two, two, two],
        compiler_params=_params("parallel", "parallel"),
        name="rwkv_prep",
    )(rw, rw, rw, mu, w2cat, w0, a2cat, a0, g2, k_k, k_a, seg)


def _block_diag(y, head_masks):
    return jnp.concatenate([jnp.where(m, y, 0.0) for m in head_masks], axis=0)


def _rwkv_scan_kernel(r_ref, v_ref, nkk_ref, lw_ref, kd_ref, bb_ref, y_ref, s_ref):
    d = pl.program_id(1)
    c = pl.program_id(2)

    @pl.when(c == 0)
    def _():
        s_ref[...] = jnp.zeros_like(s_ref)

    sgn = 1 - 2 * d
    row = lax.broadcasted_iota(jnp.int32, (CHUNK, QUAD), 0)
    lane = lax.broadcasted_iota(jnp.int32, (CHUNK, QUAD), 1)
    rel = ((lane % CHUNK) - row) * sgn
    strict = rel < 0
    incl = rel <= 0
    eye = jnp.where(rel == 0, 1.0, 0.0)
    head_masks = [(lane // HEAD_DIM) == h for h in range(4)]
    bd_mask = (lax.broadcasted_iota(jnp.int32, (QUAD, QUAD), 0) // HEAD_DIM
               == lax.broadcasted_iota(jnp.int32, (QUAD, QUAD), 1) // HEAD_DIM)
    t_row = lax.broadcasted_iota(jnp.int32, (CHUNK, CHUNK), 0)
    t_col = lax.broadcasted_iota(jnp.int32, (CHUNK, CHUNK), 1)
    tri = jnp.where((t_col - t_row) * sgn <= 0, 1.0, 0.0).astype(BF16)
    last_row = jnp.where(d == 0, CHUNK - 1, 0)

    for q in range(RWKV_WIDTH // QUAD):
        ql = slice(q * QUAD, (q + 1) * QUAD)
        lw = lw_ref[0, 0, :, ql]
        cum = _mm_exact_lhs(tri, lw)
        cum_end = jnp.sum(jnp.where(row == last_row, cum, 0.0), axis=0, keepdims=True)
        w_in = jnp.exp(cum)
        w_inv = jnp.exp(-cum)
        a_t = nkk_ref[0, :, ql] * jnp.exp(cum - lw)
        r_t = r_ref[0, :, ql] * w_in
        b_t = bb_ref[0, 0, :, ql] * w_inv
        k_t = kd_ref[0, 0, :, ql] * w_inv
        w_rem = jnp.exp(cum_end - cum)
        b_end = bb_ref[0, 0, :, ql] * w_rem
        k_end = kd_ref[0, 0, :, ql] * w_rem
        vm = v_ref[0, :, ql]
        s0 = s_ref[q]

        ar = jnp.concatenate([a_t, r_t], axis=0)
        xb = _mm3(ar, _block_diag(b_t, head_masks), NT)
        xk = _mm3(ar, _block_diag(k_t, head_masks), NT)
        n_ab = jnp.where(strict, xb[:CHUNK], 0.0)
        l_ak = jnp.where(strict, xk[:CHUNK], 0.0)
        g_rb = jnp.where(incl, xb[CHUNK:], 0.0)
        g_rk = jnp.where(incl, xk[CHUNK:], 0.0)

        m = eye + n_ab
        n_pow = n_ab
        for _ in range(int(np.log2(CHUNK)) - 1):
            n_pow = _mm3(n_pow, _block_diag(n_pow, head_masks))
            m = m + _mm3(m, _block_diag(n_pow, head_masks))

        ars = _mm3(ar, s0, NT)
        vm_bd = _block_diag(vm, head_masks)
        x0 = ars[:CHUNK] + _mm3(l_ak, vm_bd)
        u = _mm3(m, _block_diag(x0, head_masks))
        y = ars[CHUNK:] + _mm3(g_rb, _block_diag(u, head_masks)) + _mm3(g_rk, vm_bd)
        y_ref[0, 0, :, ql] = y

        z = _mm3(jnp.concatenate([u, vm], axis=0), jnp.concatenate([b_end, k_end], axis=0), TN)
        s_ref[q] = s0 * jnp.exp(cum_end) + jnp.where(bd_mask, z, 0.0)


def _rwkv_scan(r, v, nkk, lw, kd, bb, n_lat):
    b, s, w = r.shape
    nc = s // CHUNK
    nc_lat = n_lat // CHUNK
    nc_ctx = nc - nc_lat

    def chunk(d, c):
        fwd = jnp.where(c < nc_ctx, nc_lat + c, c - nc_ctx)
        bwd = jnp.where(c < nc_ctx, nc - 1 - c, nc_lat - 1 - (c - nc_ctx))
        return jnp.where(d == 0, fwd, bwd)

    tok = pl.BlockSpec((1, CHUNK, w), lambda i, d, c: (i, chunk(d, c), 0))
    tok2 = pl.BlockSpec((1, 1, CHUNK, w), lambda i, d, c: (i, d, chunk(d, c), 0))
    return pl.pallas_call(
        _rwkv_scan_kernel,
        grid=(b, 2, nc),
        in_specs=[tok, tok, tok, tok2, tok2, tok2],
        out_specs=tok2,
        out_shape=jax.ShapeDtypeStruct((b, 2, s, w), F32),
        scratch_shapes=[pltpu.VMEM((w // QUAD, QUAD, QUAD), F32)],
        compiler_params=_params("parallel", "arbitrary", "arbitrary"),
        name="rwkv_scan",
    )(r, v, nkk, lw, kd, bb)


def _rwkv_readout_kernel(y_ref, r_ref, k_ref, v_ref, g_ref, rk_ref, gw_ref, gb_ref, seg_ref, o_ref):
    seg = seg_ref[...]
    y = y_ref[0, 0] + y_ref[0, 1]
    mean = _mm_exact_rhs(y, seg) * (1.0 / HEAD_DIM)
    yc = y - mean
    var = _mm_exact_rhs(yc * yc, seg) * (1.0 / HEAD_DIM)
    yn = yc * lax.rsqrt(var + RWKV_GN_EPS)
    bonus = _mm_exact_rhs(r_ref[0] * k_ref[0] * rk_ref[...], seg) * v_ref[0]
    o_ref[0] = ((yn * gw_ref[...] + gb_ref[...] + bonus) * g_ref[0]).astype(BF16)


def _rwkv_readout(y2, r, k, v, g, r_k, gn_w, gn_b, seg):
    b, s, w = r.shape
    tok = pl.BlockSpec((1, ROW_TILE, w), lambda i, j: (i, j, 0))
    vec = pl.BlockSpec((1, w), lambda i, j: (0, 0))
    return pl.pallas_call(
        _rwkv_readout_kernel,
        grid=(b, s // ROW_TILE),
        in_specs=[pl.BlockSpec((1, 2, ROW_TILE, w), lambda i, j: (i, 0, j, 0)), tok, tok, tok, tok,
                  vec, vec, vec, pl.BlockSpec((w, w), lambda i, j: (0, 0))],
        out_specs=tok,
        out_shape=jax.ShapeDtypeStruct((b, s, w), BF16),
        compiler_params=_params("parallel", "parallel"),
        name="rwkv_readout",
    )(y2, r, k, v, g, r_k, gn_w, gn_b, seg)


def _merge_kernel(x_ref, oa_ref, or_ref, cv_ref, hp_ref, hn_ref, gt_ref, mt_ref, cw_ref, wa_ref, wr_ref, wc_ref,
                  wo_ref, o_ref, *, n_lat_tiles, n_tiles):
    j = pl.program_id(1)
    first, last = _tile_ends(j, n_lat_tiles, n_tiles)
    cw = CONV_WIDTH
    d = x_ref.shape[2]
    cv = cv_ref[0]
    z = cv[:, cw:2 * cw] * cv[:, 2 * cw:]
    zp = hp_ref[0, :, cw:2 * cw] * hp_ref[0, :, 2 * cw:]
    zn = hn_ref[0, :, cw:2 * cw] * hn_ref[0, :, 2 * cw:]
    z_prev, z_next = _neighbours(z, zp, zn, first, last)
    o_cv = cv[:, :cw] * (cw_ref[0:1] * z_prev + cw_ref[1:2] * z + cw_ref[2:3] * z_next)
    gt = gt_ref[0]
    m = (_sigmoid(gt[:, :d]) * _dg(oa_ref[0], wa_ref[...])
         + _sigmoid(gt[:, d:2 * d]) * _dg(or_ref[0], wr_ref[...])
         + _sigmoid(gt[:, 2 * d:]) * _dg(o_cv.astype(BF16), wc_ref[...]))
    y = _dg(m.astype(BF16), wo_ref[...])
    o_ref[0] = x_ref[0] + mt_ref[0, 0, 2:3] * y


def _merge(xc, o_att, o_rw, cv, gt, mt, conv_w, wa, wr, wc, wo, n_lat_tiles):
    b, s, d = xc.shape
    n_tiles = s // ROW_TILE
    hp, hn = _halo_specs(cv.shape[2], n_tiles)
    tok = lambda n: pl.BlockSpec((1, ROW_TILE, n), lambda i, j: (i, j, 0))
    full = lambda a: pl.BlockSpec(a.shape, lambda i, j: (0,) * a.ndim)
    return pl.pallas_call(
        functools.partial(_merge_kernel, n_lat_tiles=n_lat_tiles, n_tiles=n_tiles),
        grid=(b, n_tiles),
        in_specs=[tok(d), tok(o_att.shape[2]), tok(o_rw.shape[2]), tok(cv.shape[2]), hp, hn, tok(gt.shape[2]),
                  pl.BlockSpec((1, 1, 6, d), lambda i, j: (i, j // n_lat_tiles, 0, 0)),
                  full(conv_w), full(wa), full(wr), full(wc), full(wo)],
        out_specs=tok(d),
        out_shape=jax.ShapeDtypeStruct((b, s, d), F32),
        compiler_params=_params("parallel", "parallel"),
        name="merge",
    )(xc, o_att, o_rw, cv, cv, cv, gt, mt, conv_w, wa, wr, wc, wo)


def _router_kernel(x_ref, gain_ref, mt_ref, wr_ref, h_ref, aff_ref):
    h = _rms(x_ref[0]) * gain_ref[...]
    h = h * (1.0 + mt_ref[0, 0, 4:5]) + mt_ref[0, 0, 3:4]
    h_ref[0] = h.astype(BF16)
    logits = _mm3(wr_ref[...], h, NT)
    e = jnp.exp(logits - jnp.max(logits, axis=0, keepdims=True))
    aff_ref[0] = e / jnp.sum(e, axis=0, keepdims=True)


def _router(xc, gain, mt, w_router_t, n_lat_tiles):
    b, s, d = xc.shape
    ne = w_router_t.shape[0]
    return pl.pallas_call(
        _router_kernel,
        grid=(b, s // ROW_TILE),
        in_specs=[pl.BlockSpec((1, ROW_TILE, d), lambda i, j: (i, j, 0)),
                  pl.BlockSpec((1, d), lambda i, j: (0, 0)),
                  pl.BlockSpec((1, 1, 6, d), lambda i, j: (i, j // n_lat_tiles, 0, 0)),
                  pl.BlockSpec((ne, d), lambda i, j: (0, 0))],
        out_specs=[pl.BlockSpec((1, ROW_TILE, d), lambda i, j: (i, j, 0)),
                   pl.BlockSpec((1, ne, ROW_TILE), lambda i, j: (i, 0, j))],
        out_shape=[jax.ShapeDtypeStruct((b, s, d), BF16), jax.ShapeDtypeStruct((b, ne, s), F32)],
        compiler_params=_params("parallel", "parallel"),
        name="router",
    )(xc, gain, mt, w_router_t)


def _select_kernel(aff_ref, pos_ref, gate_ref, *, cap):
    a = aff_ref[0]
    ne, n = a.shape
    bits = pltpu.bitcast(a, jnp.int32)

    def count(mask):
        return jnp.sum(jnp.where(mask, 1.0, 0.0), axis=1, keepdims=True)

    def body(_, carry):
        lo, hi = carry
        mid = lo + ((hi - lo + 1) >> 1)
        ok = count(bits >= mid) >= cap
        return jnp.where(ok, mid, lo), jnp.where(ok, hi, mid - 1)

    lo0 = jnp.zeros((ne, 1), jnp.int32)
    hi0 = jnp.full((ne, 1), 0x7F800000, jnp.int32)
    thr, _ = lax.fori_loop(0, 32, body, (lo0, hi0))
    gt = bits > thr
    eq = bits == thr
    need = cap - count(gt)

    def tokens_before(mask):
        m = jnp.where(mask, 1.0, 0.0).astype(BF16)
        blk = min(n, SELECT_BLOCK)
        cols = []
        for j in range(n // blk):
            s_idx = lax.broadcasted_iota(jnp.int32, (n, blk), 0)
            t_idx = lax.broadcasted_iota(jnp.int32, (n, blk), 1) + j * blk
            cols.append(_dg(m, jnp.where(s_idx < t_idx, 1.0, 0.0).astype(BF16)))
        return jnp.concatenate(cols, axis=1)

    sel = jnp.logical_or(gt, jnp.logical_and(eq, tokens_before(eq) < need))
    pos_ref[0] = jnp.where(sel, tokens_before(sel).astype(jnp.int32), -1)
    gate_ref[0] = jnp.where(sel, a, 0.0)


def _select(aff_t, tok0, n, cap):
    b, ne, _ = aff_t.shape
    blk = tok0 // n
    return pl.pallas_call(
        functools.partial(_select_kernel, cap=cap),
        grid=(b,),
        in_specs=[pl.BlockSpec((1, ne, n), lambda i: (i, 0, blk))],
        out_specs=[pl.BlockSpec((1, ne, n), lambda i: (i, 0, 0)), pl.BlockSpec((1, ne, n), lambda i: (i, 0, 0))],
        out_shape=[jax.ShapeDtypeStruct((b, ne, n), jnp.int32), jax.ShapeDtypeStruct((b, ne, n), F32)],
        compiler_params=_params("parallel"),
        name="moe_select",
    )(aff_t)


def _expert_kernel(h_ref, pos_ref, gate_ref, wg_ref, wu_ref, wd_ref, o_ref, *, cap):
    e = pl.program_id(1)

    @pl.when(e == 0)
    def _():
        o_ref[...] = jnp.zeros_like(o_ref)

    n = h_ref.shape[1]
    pos = pos_ref[0, 0]
    slot = lax.broadcasted_iota(jnp.int32, (cap, n), 0)
    onehot = jnp.where(pos == slot, 1.0, 0.0).astype(BF16)
    xe = _dg(onehot, h_ref[0]).astype(BF16)
    g8 = jnp.broadcast_to(gate_ref[0, 0], (8, n))
    gate_slot = _mm_exact_lhs(onehot, g8, NT)[:, 0:1]
    hg = _dg(xe, wg_ref[0])
    hu = _dg(xe, wu_ref[0])
    hid = (hg * _sigmoid(hg) * hu).astype(BF16)
    ye = _dg(hid, wd_ref[0]) * gate_slot
    hi, lo = _split2(ye)
    o_ref[0] += _dg(onehot, hi, TN) + _dg(onehot, lo, TN)


def _experts(h, pos, gate, wg, wu, wd, tok0, n, cap):
    b, s, d = h.shape
    ne, _, f = wg.shape
    blk = tok0 // n
    sel = pl.BlockSpec((1, 1, 1, n), lambda i, e: (i, e, 0, 0))
    return pl.pallas_call(
        functools.partial(_expert_kernel, cap=cap),
        grid=(b, ne),
        in_specs=[pl.BlockSpec((1, n, d), lambda i, e: (i, blk, 0)), sel, sel,
                  pl.BlockSpec((1, d, f), lambda i, e: (e, 0, 0)),
                  pl.BlockSpec((1, d, f), lambda i, e: (e, 0, 0)),
                  pl.BlockSpec((1, f, d), lambda i, e: (e, 0, 0))],
        out_specs=pl.BlockSpec((1, n, d), lambda i, e: (i, 0, 0)),
        out_shape=jax.ShapeDtypeStruct((b, n, d), F32),
        compiler_params=_params("parallel", "arbitrary"),
        name="moe_experts",
    )(h, pos.reshape(b, ne, 1, n), gate.reshape(b, ne, 1, n), wg, wu, wd)


def _residual_kernel(x_ref, m_ref, mt_ref, o_ref):
    o_ref[0] = x_ref[0] + mt_ref[0, 0, 5:6] * m_ref[0]


def _moe_residual(xc, moe, mt, tok0, stream):
    b, s, d = xc.shape
    n = moe.shape[1]
    t0 = tok0 // ROW_TILE
    return pl.pallas_call(
        _residual_kernel,
        grid=(b, n // ROW_TILE),
        in_specs=[pl.BlockSpec((1, ROW_TILE, d), lambda i, j: (i, t0 + j, 0)),
                  pl.BlockSpec((1, ROW_TILE, d), lambda i, j: (i, j, 0)),
                  pl.BlockSpec((1, 1, 6, d), lambda i, j: (i, stream, 0, 0))],
        out_specs=pl.BlockSpec((1, ROW_TILE, d), lambda i, j: (i, t0 + j, 0)),
        out_shape=jax.ShapeDtypeStruct((b, s, d), F32),
        input_output_aliases={0: 0},
        compiler_params=_params("parallel", "parallel"),
        name="moe_residual",
    )(xc, moe, mt)


def _final_norm_kernel(x_ref, g_ref, o_ref):
    o_ref[0] = _rms(x_ref[0]) * g_ref[...]


def _final_norm(xc, gain, n_lat):
    b, s, d = xc.shape
    return pl.pallas_call(
        _final_norm_kernel,
        grid=(b, n_lat // ROW_TILE),
        in_specs=[pl.BlockSpec((1, ROW_TILE, d), lambda i, j: (i, j, 0)),
                  pl.BlockSpec((1, d), lambda i, j: (0, 0))],
        out_specs=pl.BlockSpec((1, ROW_TILE, d), lambda i, j: (i, j, 0)),
        out_shape=jax.ShapeDtypeStruct((b, n_lat, d), F32),
        compiler_params=_params("parallel", "parallel"),
        name="final_norm",
    )(xc, gain)


def _rope_tables(n_lat, n_ctx):
    rows = n_lat // GRID_W
    row = jnp.repeat(jnp.arange(rows, dtype=F32), GRID_W)
    col = jnp.tile(jnp.arange(GRID_W, dtype=F32), rows)
    axis_dim = HEAD_DIM // 2
    inv_freq = ROPE_THETA ** (-jnp.arange(0, axis_dim, 2, dtype=F32) / axis_dim)
    ang_r = row[:, None] * inv_freq[None, :]
    ang_c = col[:, None] * inv_freq[None, :]
    cos_h = jnp.concatenate([jnp.cos(ang_r), jnp.cos(ang_r), jnp.cos(ang_c), jnp.cos(ang_c)], axis=1)
    sin_h = jnp.concatenate([-jnp.sin(ang_r), jnp.sin(ang_r), -jnp.sin(ang_c), jnp.sin(ang_c)], axis=1)
    cos_h = jnp.concatenate([cos_h, jnp.ones((n_ctx, HEAD_DIM), F32)], axis=0)
    sin_h = jnp.concatenate([sin_h, jnp.zeros((n_ctx, HEAD_DIM), F32)], axis=0)
    return jnp.tile(cos_h, (1, ATT_HEADS)), jnp.tile(sin_h, (1, ATT_HEADS))


def _head_sum_matrix(width):
    idx = np.arange(width) // HEAD_DIM
    return jnp.asarray(idx[:, None] == idx[None, :], dtype=BF16)


def _two_dir_lowrank(w2):
    _, rank, w = w2.shape
    z = jnp.zeros((rank, w), w2.dtype)
    return jnp.concatenate([jnp.concatenate([w2[0], z], axis=1), jnp.concatenate([z, w2[1]], axis=1)], axis=0)


def kernel(x, c, ctx, c_ctx, ada_w, ada_b, norm1, w_in, q_gain, k_gain, shift_mu, decay_w0, decay_w2, iclr_a0, iclr_a2, gate_g2, rwkv_kk, rwkv_ka, rwkv_rk, rwkv_gn_w, rwkv_gn_b, conv_w, w_br_att, w_br_rwkv, w_br_conv, w_out, norm2, w_router, exp_gate, exp_up, exp_down, final_norm):
    b, n_lat, d = x.shape
    n_ctx = ctx.shape[1]
    depth = ada_w.shape[0]
    assert n_lat % ROW_TILE == 0 and n_ctx % ROW_TILE == 0 and n_lat % n_ctx == 0
    n_lat_tiles = n_lat // ROW_TILE
    w = RWKV_WIDTH

    pad = (-(b + 1)) % 8
    cc = jnp.concatenate([c, c_ctx[None, :], jnp.zeros((pad, d), F32)], axis=0)
    mods = _ada_table(cc, ada_w, ada_b)

    cos_t, sin_t = _rope_tables(n_lat, n_ctx)
    seg = _head_sum_matrix(ATT_WIDTH)
    att_w = ATT_WIDTH + 2 * ATT_KV_WIDTH
    cv_w = 3 * CONV_WIDTH
    offs = np.cumsum([0, att_w, RWKV_SEG, cv_w, 3 * d])

    xc = jnp.concatenate([x, ctx], axis=1)
    for l in range(depth):
        mod_lat = mods[l, :b].reshape(b, 1, 6, d)
        mod_ctx = jnp.broadcast_to(mods[l, b].reshape(1, 1, 6, d), (b, 1, 6, d))
        mt = jnp.concatenate([mod_lat, mod_ctx], axis=1)

        w_l = w_in[l].astype(BF16)
        gain1 = norm1[l].reshape(1, d)
        att, rw, cv, gt = [_normmod_mm(xc, gain1, mt, w_l[:, offs[i]:offs[i + 1]], n_lat_tiles) for i in range(4)]

        q, k, v = _att_prep(att, cos_t, sin_t, jnp.tile(q_gain[l], ATT_HEADS).reshape(1, -1),
                            jnp.tile(k_gain[l], ATT_KV_HEADS).reshape(1, -1), seg)
        o_att = _attention(q, k, v, n_lat)

        r, kx, vx, g, nkk, lw, kd, bb = _rwkv_prep(
            rw, shift_mu[l].reshape(1, -1), _two_dir_lowrank(decay_w2[l]), decay_w0[l].reshape(1, -1),
            _two_dir_lowrank(iclr_a2[l]), iclr_a0[l].reshape(1, -1), gate_g2[l],
            rwkv_kk[l].reshape(1, -1), rwkv_ka[l].reshape(1, -1), seg, n_lat_tiles)
        y2 = _rwkv_scan(r, vx, nkk, lw, kd, bb, n_lat)
        o_rw = _rwkv_readout(y2, r, kx, vx, g, rwkv_rk[l].reshape(1, -1), rwkv_gn_w[l].reshape(1, -1),
                             rwkv_gn_b[l].reshape(1, -1), seg)

        xc = _merge(xc, o_att, o_rw, cv, gt, mt, conv_w[l], w_br_att[l].astype(BF16), w_br_rwkv[l].astype(BF16),
                    w_br_conv[l].astype(BF16), w_out[l].astype(BF16), n_lat_tiles)

        h2, aff_t = _router(xc, norm2[l].reshape(1, d), mt, w_router[l].T, n_lat_tiles)
        wg, wu, wd = exp_gate[l].astype(BF16), exp_up[l].astype(BF16), exp_down[l].astype(BF16)
        streams = [(0, n_lat, 0)] + ([(n_lat, n_ctx, 1)] if l < depth - 1 else [])
        for tok0, n, stream in streams:
            cap = CAPACITY_FACTOR * n // N_EXPERTS
            pos, gate = _select(aff_t, tok0, n, cap)
            moe = _experts(h2, pos, gate, wg, wu, wd, tok0, n, cap)
            xc = _moe_residual(xc, moe, mt, tok0, stream)
    return _final_norm(xc, final_norm.reshape(1, d), n_lat)
```

```python
import functools

import numpy as np
import jax
import jax.numpy as jnp
from jax import lax
from jax.experimental import pallas as pl
from jax.experimental.pallas import tpu as pltpu

F32 = jnp.float32
BF16 = jnp.bfloat16

GRID_W = 64
NORM_EPS = 1e-6
ATT_HEADS = 8
ATT_KV_HEADS = 2
HEAD_DIM = 64
ATT_GROUP = ATT_HEADS // ATT_KV_HEADS
ATT_WIDTH = ATT_HEADS * HEAD_DIM
ATT_KV_WIDTH = ATT_KV_HEADS * HEAD_DIM
ROPE_THETA = 10000.0
RWKV_HEADS = 8
RWKV_WIDTH = RWKV_HEADS * HEAD_DIM
DECAY_RANK = 64
ICLR_RANK = 64
GATE_RANK = 128
RWKV_GN_EPS = 64e-5
RWKV_SEG = 3 * RWKV_WIDTH + 2 * DECAY_RANK + 2 * ICLR_RANK + GATE_RANK
CONV_WIDTH = 512
N_EXPERTS = 16
CAPACITY_FACTOR = 2

ROW_TILE = 256
CHUNK = 64
QUAD = 4 * HEAD_DIM
SCAN_BATCH = 2
HALO = 8
SELECT_BLOCK = 256
VMEM_LIMIT = 56 * 1024 * 1024

NT = (((1,), (1,)), ((), ()))
TN = (((0,), (0,)), ((), ()))
NN = (((1,), (0,)), ((), ()))


def _params(*sem):
    return pltpu.CompilerParams(dimension_semantics=sem, vmem_limit_bytes=VMEM_LIMIT)


def _dg(a, b, dn=NN):
    return lax.dot_general(a, b, dn, preferred_element_type=F32)


def _split2(x):
    hi = x.astype(BF16)
    lo = (x - hi.astype(F32)).astype(BF16)
    return hi, lo


def _split3(x):
    hi = x.astype(BF16)
    r = x - hi.astype(F32)
    mid = r.astype(BF16)
    lo = (r - mid.astype(F32)).astype(BF16)
    return hi, mid, lo


def _mm1(a, b, dn=NN):
    return _dg(a.astype(BF16), b.astype(BF16), dn)


def _mm3(a, b, dn=NN):
    ah, al = _split2(a)
    bh, bl = _split2(b)
    return _dg(ah, bh, dn) + (_dg(ah, bl, dn) + _dg(al, bh, dn))


def _mm_exact_lhs(a_bf16, b, dn=NN):
    h, m, l = _split3(b)
    return _dg(a_bf16, h, dn) + (_dg(a_bf16, m, dn) + _dg(a_bf16, l, dn))


def _mm_exact_rhs(a, b_bf16, dn=NN):
    h, m, l = _split3(a)
    return _dg(h, b_bf16, dn) + (_dg(m, b_bf16, dn) + _dg(l, b_bf16, dn))


def _sigmoid(x):
    return 1.0 / (1.0 + jnp.exp(-x))


def _rms(x):
    return x * lax.rsqrt(jnp.mean(x * x, axis=-1, keepdims=True) + NORM_EPS)


def _ada_kernel(c_ref, w_ref, b_ref, o_ref):
    c = c_ref[...]
    s = c * _sigmoid(c)
    o_ref[0] = _mm3(s, w_ref[0]) + b_ref[0]


def _ada_table(cc, ada_w, ada_b):
    n_layers, d, six_d = ada_w.shape
    rows = cc.shape[0]
    tn = six_d // 4
    return pl.pallas_call(
        _ada_kernel,
        grid=(n_layers, six_d // tn),
        in_specs=[pl.BlockSpec((rows, d), lambda l, j: (0, 0)),
                  pl.BlockSpec((1, d, tn), lambda l, j: (l, 0, j)),
                  pl.BlockSpec((1, 1, tn), lambda l, j: (l, 0, j))],
        out_specs=pl.BlockSpec((1, rows, tn), lambda l, j: (l, 0, j)),
        out_shape=jax.ShapeDtypeStruct((n_layers, rows, six_d), F32),
        compiler_params=_params("parallel", "parallel"),
        name="ada_table",
    )(cc, ada_w, ada_b.reshape(n_layers, 1, six_d))


def _normmod_mm_kernel(x_ref, gain_ref, mt_ref, w_ref, o_ref):
    h = _rms(x_ref[0]) * gain_ref[...]
    h = h * (1.0 + mt_ref[0, 0, 1:2]) + mt_ref[0, 0, 0:1]
    o_ref[0] = jnp.dot(h.astype(BF16), w_ref[...], preferred_element_type=F32)


def _normmod_mm(xc, gain, mt, w, n_lat_tiles):
    b, s, d = xc.shape
    n = w.shape[1]
    return pl.pallas_call(
        _normmod_mm_kernel,
        grid=(b, s // ROW_TILE),
        in_specs=[pl.BlockSpec((1, ROW_TILE, d), lambda i, j: (i, j, 0)),
                  pl.BlockSpec((1, d), lambda i, j: (0, 0)),
                  pl.BlockSpec((1, 1, 6, d), lambda i, j: (i, j // n_lat_tiles, 0, 0)),
                  pl.BlockSpec((d, n), lambda i, j: (0, 0))],
        out_specs=pl.BlockSpec((1, ROW_TILE, n), lambda i, j: (i, j, 0)),
        out_shape=jax.ShapeDtypeStruct((b, s, n), F32),
        compiler_params=_params("parallel", "parallel"),
        name="normmod_mm",
    )(xc, gain, mt, w)


def _swap16(x):
    w = x.shape[-1]
    lane = lax.broadcasted_iota(jnp.int32, x.shape, x.ndim - 1)
    return jnp.where((lane & 16) == 0, pltpu.roll(x, w - 16, x.ndim - 1), pltpu.roll(x, 16, x.ndim - 1))


def _att_prep_kernel(a_ref, cos_ref, sin_ref, qg_ref, kg_ref, seg_ref, q_ref, k_ref, v_ref):
    a = a_ref[0]
    seg = seg_ref[...]

    def norm_rope(u, gain, cos, sin, segm):
        hi, lo = _split2(u * u)
        ms = (_dg(hi, segm) + _dg(lo, segm)) * (1.0 / HEAD_DIM)
        un = u * lax.rsqrt(ms + NORM_EPS) * gain
        return un * cos + _swap16(un) * sin

    q = norm_rope(a[:, :ATT_WIDTH], qg_ref[...], cos_ref[...], sin_ref[...], seg)
    kw = ATT_KV_WIDTH
    k = norm_rope(a[:, ATT_WIDTH:ATT_WIDTH + kw], kg_ref[...], cos_ref[:, :kw], sin_ref[:, :kw], seg[:kw, :kw])
    q_ref[0] = (q * (HEAD_DIM ** -0.5)).astype(BF16)
    k_ref[0] = k.astype(BF16)
    v_ref[0] = a[:, ATT_WIDTH + kw:].astype(BF16)


def _att_prep(att, cos_t, sin_t, q_gain_t, k_gain_t, seg):
    b, s, w = att.shape
    row = lambda n: pl.BlockSpec((1, ROW_TILE, n), lambda i, j: (i, j, 0))
    return pl.pallas_call(
        _att_prep_kernel,
        grid=(b, s // ROW_TILE),
        in_specs=[row(w),
                  pl.BlockSpec((ROW_TILE, ATT_WIDTH), lambda i, j: (j, 0)),
                  pl.BlockSpec((ROW_TILE, ATT_WIDTH), lambda i, j: (j, 0)),
                  pl.BlockSpec((1, ATT_WIDTH), lambda i, j: (0, 0)),
                  pl.BlockSpec((1, ATT_KV_WIDTH), lambda i, j: (0, 0)),
                  pl.BlockSpec((ATT_WIDTH, ATT_WIDTH), lambda i, j: (0, 0))],
        out_specs=[row(ATT_WIDTH), row(ATT_KV_WIDTH), row(ATT_KV_WIDTH)],
        out_shape=[jax.ShapeDtypeStruct((b, s, ATT_WIDTH), BF16),
                   jax.ShapeDtypeStruct((b, s, ATT_KV_WIDTH), BF16),
                   jax.ShapeDtypeStruct((b, s, ATT_KV_WIDTH), BF16)],
        compiler_params=_params("parallel", "parallel"),
        name="att_prep",
    )(att, cos_t, sin_t, q_gain_t, k_gain_t, seg)


def _attn_kernel(q_ref, k_ref, v_ref, o_ref, *, n_lat, n_lat_tiles):
    j = pl.program_id(1)
    s_all = k_ref.shape[1]

    def run(k0, k1):
        for kvh in range(ATT_KV_HEADS):
            k = k_ref[0, k0:k1, kvh * HEAD_DIM:(kvh + 1) * HEAD_DIM]
            v = v_ref[0, k0:k1, kvh * HEAD_DIM:(kvh + 1) * HEAD_DIM]
            for g in range(ATT_GROUP):
                h = kvh * ATT_GROUP + g
                q = q_ref[0, :, h * HEAD_DIM:(h + 1) * HEAD_DIM]
                s = _dg(q, k, NT)
                p = jnp.exp(s - jnp.max(s, axis=-1, keepdims=True))
                l = jnp.sum(p, axis=-1, keepdims=True)
                o = _dg(p.astype(BF16), v) / l
                o_ref[0, :, h * HEAD_DIM:(h + 1) * HEAD_DIM] = o.astype(BF16)

    @pl.when(j < n_lat_tiles)
    def _():
        run(0, s_all)

    @pl.when(j >= n_lat_tiles)
    def _():
        run(n_lat, s_all)


def _attention(q, k, v, n_lat):
    b, s, _ = q.shape
    return pl.pallas_call(
        functools.partial(_attn_kernel, n_lat=n_lat, n_lat_tiles=n_lat // ROW_TILE),
        grid=(b, s // ROW_TILE),
        in_specs=[pl.BlockSpec((1, ROW_TILE, ATT_WIDTH), lambda i, j: (i, j, 0)),
                  pl.BlockSpec((1, s, ATT_KV_WIDTH), lambda i, j: (i, 0, 0)),
                  pl.BlockSpec((1, s, ATT_KV_WIDTH), lambda i, j: (i, 0, 0))],
        out_specs=pl.BlockSpec((1, ROW_TILE, ATT_WIDTH), lambda i, j: (i, j, 0)),
        out_shape=jax.ShapeDtypeStruct((b, s, ATT_WIDTH), BF16),
        compiler_params=_params("parallel", "parallel"),
        name="attention",
    )(q, k, v)


def _neighbours(x, halo_prev, halo_next, first, last):
    rows = x.shape[0]
    ridx = lax.broadcasted_iota(jnp.int32, x.shape, 0)
    row_p = jnp.where(first, 0.0, halo_prev[HALO - 1:HALO])
    row_n = jnp.where(last, 0.0, halo_next[0:1])
    prev = jnp.where(ridx == 0, row_p, pltpu.roll(x, 1, 0))
    nxt = jnp.where(ridx == rows - 1, row_n, pltpu.roll(x, rows - 1, 0))
    return prev, nxt


def _tile_ends(j, n_lat_tiles, n_tiles):
    first = jnp.logical_or(j == 0, j == n_lat_tiles)
    last = jnp.logical_or(j == n_lat_tiles - 1, j == n_tiles - 1)
    return first, last


def _halo_specs(width, n_tiles):
    per = ROW_TILE // HALO
    prev = pl.BlockSpec((1, HALO, width), lambda i, j: (i, jnp.maximum(j * per - 1, 0), 0))
    nxt = pl.BlockSpec((1, HALO, width), lambda i, j: (i, jnp.minimum((j + 1) * per, n_tiles * per - 1), 0))
    return prev, nxt


def _rwkv_prep_kernel(x_ref, hp_ref, hn_ref, mu_ref, w2_ref, w0_ref, a2_ref, a0_ref, g2_ref, kk_ref, ka_ref,
                      seg_ref, r_ref, k_ref, v_ref, g_ref, nkk_ref, lw_ref, kd_ref, bb_ref, *, n_lat_tiles, n_tiles):
    j = pl.program_id(1)
    first, last = _tile_ends(j, n_lat_tiles, n_tiles)
    x = x_ref[0]
    prev, nxt = _neighbours(x, hp_ref[0], hn_ref[0], first, last)
    xs = x + mu_ref[...] * (0.5 * (prev + nxt) - x)
    w = RWKV_WIDTH
    r = xs[:, 0:w]
    k = xs[:, w:2 * w]
    v = xs[:, 2 * w:3 * w]
    o = 3 * w
    w_lo = xs[:, o:o + 2 * DECAY_RANK]
    a_lo = xs[:, o + 2 * DECAY_RANK:o + 2 * DECAY_RANK + 2 * ICLR_RANK]
    g_lo = xs[:, o + 2 * DECAY_RANK + 2 * ICLR_RANK:]
    w_pre = w0_ref[...] + _mm3(jnp.tanh(w_lo), w2_ref[...])
    logw = -_sigmoid(w_pre) * float(np.exp(-0.5))
    a = _sigmoid(a0_ref[...] + _mm3(a_lo, a2_ref[...]))
    g = _mm3(_sigmoid(g_lo), g2_ref[...])
    kk = k * kk_ref[...]
    kk = kk * lax.rsqrt(_mm_exact_rhs(kk * kk, seg_ref[...]) + 1e-12)
    r_ref[0] = r
    k_ref[0] = k
    v_ref[0] = v
    g_ref[0] = g
    nkk_ref[0] = -kk
    for d in range(2):
        a_d = a[:, d * w:(d + 1) * w]
        lw_ref[0, d] = logw[:, d * w:(d + 1) * w]
        kd_ref[0, d] = k * (1.0 + (a_d - 1.0) * ka_ref[...])
        bb_ref[0, d] = kk * a_d


def _rwkv_prep(rw, mu, w2cat, w0, a2cat, a0, g2, k_k, k_a, seg, n_lat_tiles):
    b, s, wseg = rw.shape
    n_tiles = s // ROW_TILE
    w = RWKV_WIDTH
    hp, hn = _halo_specs(wseg, n_tiles)
    full = lambda shape: pl.BlockSpec(shape, lambda i, j: (0,) * len(shape))
    tok = pl.BlockSpec((1, ROW_TILE, w), lambda i, j: (i, j, 0))
    tok2 = pl.BlockSpec((1, 2, ROW_TILE, w), lambda i, j: (i, 0, j, 0))
    one = jax.ShapeDtypeStruct((b, s, w), F32)
    two = jax.ShapeDtypeStruct((b, 2, s, w), F32)
    return pl.pallas_call(
        functools.partial(_rwkv_prep_kernel, n_lat_tiles=n_lat_tiles, n_tiles=n_tiles),
        grid=(b, n_tiles),
        in_specs=[pl.BlockSpec((1, ROW_TILE, wseg), lambda i, j: (i, j, 0)), hp, hn,
                  full((1, wseg)), full(w2cat.shape), full((1, 2 * w)), full(a2cat.shape), full((1, 2 * w)),
                  full(g2.shape), full((1, w)), full((1, w)), full(seg.shape)],
        out_specs=[tok, tok, tok, tok, tok, tok2, tok2, tok2],
        out_shape=[one, one, one, one, one, two, two, two],
        compiler_params=_params("parallel", "parallel"),
        name="rwkv_prep",
    )(rw, rw, rw, mu, w2cat, w0, a2cat, a0, g2, k_k, k_a, seg)


def _block_diag(y, head_masks):
    return jnp.concatenate([jnp.where(m, y, 0.0) for m in head_masks], axis=0)


def _scan_chain(sgn, r, v, nkk, lw, kd, bb, s0):
    row = lax.broadcasted_iota(jnp.int32, (CHUNK, QUAD), 0)
    lane = lax.broadcasted_iota(jnp.int32, (CHUNK, QUAD), 1)
    rel = ((lane % CHUNK) - row) * sgn
    strict = rel < 0
    incl = rel <= 0
    eye = jnp.where(rel == 0, 1.0, 0.0)
    head_masks = [(lane // HEAD_DIM) == h for h in range(4)]
    bd_mask = (lax.broadcasted_iota(jnp.int32, (QUAD, QUAD), 0) // HEAD_DIM
               == lax.broadcasted_iota(jnp.int32, (QUAD, QUAD), 1) // HEAD_DIM)
    t_row = lax.broadcasted_iota(jnp.int32, (CHUNK, CHUNK), 0)
    t_col = lax.broadcasted_iota(jnp.int32, (CHUNK, CHUNK), 1)
    tri = jnp.where((t_col - t_row) * sgn <= 0, 1.0, 0.0).astype(BF16)
    last_row = CHUNK - 1 if sgn > 0 else 0
    bd = lambda t: _block_diag(t, head_masks).astype(BF16)

    cum = _mm_exact_lhs(tri, lw)
    yield
    cum_end = cum[last_row:last_row + 1]
    w_inv = jnp.exp(-cum)
    w_rem = jnp.exp(cum_end - cum)
    ar = jnp.concatenate([nkk * jnp.exp(cum - lw), r * jnp.exp(cum)], axis=0).astype(BF16)
    xb = _dg(ar, bd(bb * w_inv), NT)
    xk = _dg(ar, bd(kd * w_inv), NT)
    ars = _dg(ar, s0.astype(BF16), NT)
    yield
    n_ab = jnp.where(strict, xb[:CHUNK], 0.0)
    l_ak = jnp.where(strict, xk[:CHUNK], 0.0)
    g_rb = jnp.where(incl, xb[CHUNK:], 0.0)
    g_rk = jnp.where(incl, xk[CHUNK:], 0.0)

    m = eye + n_ab
    p = _dg(n_ab.astype(BF16), bd(n_ab))
    lg = _dg(jnp.concatenate([l_ak, g_rk], axis=0).astype(BF16), bd(v))
    yield
    for _ in range(int(np.log2(CHUNK)) - 2):
        both = _dg(jnp.concatenate([p, m], axis=0).astype(BF16), bd(p))
        yield
        p = both[:CHUNK]
        m = m + both[CHUNK:]
    m = m + _dg(m.astype(BF16), bd(p))
    yield
    x0 = ars[:CHUNK] + lg[:CHUNK]
    u = _dg(m.astype(BF16), bd(x0))
    yield
    y = ars[CHUNK:] + _dg(g_rb.astype(BF16), bd(u)) + lg[CHUNK:]
    z = _dg(jnp.concatenate([u, v], axis=0).astype(BF16),
            jnp.concatenate([bb * w_rem, kd * w_rem], axis=0).astype(BF16), TN)
    yield
    return y, s0 * jnp.exp(cum_end) + jnp.where(bd_mask, z, 0.0)


def _run_interleaved(chains):
    results = [None] * len(chains)
    live = list(enumerate(chains))
    while live:
        still = []
        for idx, g in live:
            try:
                next(g)
                still.append((idx, g))
            except StopIteration as stop:
                results[idx] = stop.value
        live = still
    return results


def _rwkv_scan_kernel(rf_ref, vf_ref, nf_ref, lwf_ref, kdf_ref, bbf_ref, rb_ref, vb_ref, nb_ref, lwb_ref, kdb_ref,
                      bbb_ref, yf_ref, yb_ref, s_ref):
    @pl.when(pl.program_id(1) == 0)
    def _():
        s_ref[...] = jnp.zeros_like(s_ref)

    dirs = ((1, rf_ref, vf_ref, nf_ref, lwf_ref, kdf_ref, bbf_ref, yf_ref),
            (-1, rb_ref, vb_ref, nb_ref, lwb_ref, kdb_ref, bbb_ref, yb_ref))
    work = []
    for i in range(SCAN_BATCH):
        for d, (sgn, r_ref, v_ref, n_ref, lw_ref, kd_ref, bb_ref, y_ref) in enumerate(dirs):
            for q in range(RWKV_WIDTH // QUAD):
                ql = slice(q * QUAD, (q + 1) * QUAD)
                args = (r_ref[i, :, ql], v_ref[i, :, ql], n_ref[i, :, ql], lw_ref[i, 0, :, ql],
                        kd_ref[i, 0, :, ql], bb_ref[i, 0, :, ql], s_ref[i, d, q])
                work.append((sgn, args, y_ref, (i, slice(None), ql), (i, d, q)))
    results = _run_interleaved([_scan_chain(sgn, *args) for sgn, args, _, _, _ in work])
    for (_, _, y_ref, y_idx, s_idx), (y, s_new) in zip(work, results):
        y_ref[y_idx] = y
        s_ref[s_idx] = s_new


def _rwkv_scan(r, v, nkk, lw, kd, bb, n_lat):
    b, s, w = r.shape
    nc = s // CHUNK
    nc_lat = n_lat // CHUNK
    nc_ctx = nc - nc_lat
    nb = SCAN_BATCH
    assert b % nb == 0

    fwd = lambda c: jnp.where(c < nc_ctx, nc_lat + c, c - nc_ctx)
    bwd = lambda c: jnp.where(c < nc_ctx, nc - 1 - c, nc_lat - 1 - (c - nc_ctx))
    tok = lambda order: pl.BlockSpec((nb, CHUNK, w), lambda i, c: (i, order(c), 0))
    tok2 = lambda order, d: pl.BlockSpec((nb, 1, CHUNK, w), lambda i, c: (i, d, order(c), 0))
    out = jax.ShapeDtypeStruct((b, s, w), F32)
    return pl.pallas_call(
        _rwkv_scan_kernel,
        grid=(b // nb, nc),
        in_specs=[tok(fwd), tok(fwd), tok(fwd), tok2(fwd, 0), tok2(fwd, 0), tok2(fwd, 0),
                  tok(bwd), tok(bwd), tok(bwd), tok2(bwd, 1), tok2(bwd, 1), tok2(bwd, 1)],
        out_specs=[tok(fwd), tok(bwd)],
        out_shape=[out, out],
        scratch_shapes=[pltpu.VMEM((nb, 2, w // QUAD, QUAD, QUAD), F32)],
        compiler_params=_params("parallel", "arbitrary"),
        name="rwkv_scan",
    )(r, v, nkk, lw, kd, bb, r, v, nkk, lw, kd, bb)


def _rwkv_readout_kernel(yf_ref, yb_ref, r_ref, k_ref, v_ref, g_ref, rk_ref, gw_ref, gb_ref, seg_ref, o_ref):
    seg = seg_ref[...]
    y = yf_ref[0] + yb_ref[0]
    mean = _mm_exact_rhs(y, seg) * (1.0 / HEAD_DIM)
    yc = y - mean
    var = _mm_exact_rhs(yc * yc, seg) * (1.0 / HEAD_DIM)
    yn = yc * lax.rsqrt(var + RWKV_GN_EPS)
    bonus = _mm_exact_rhs(r_ref[0] * k_ref[0] * rk_ref[...], seg) * v_ref[0]
    o_ref[0] = ((yn * gw_ref[...] + gb_ref[...] + bonus) * g_ref[0]).astype(BF16)


def _rwkv_readout(y_f, y_b, r, k, v, g, r_k, gn_w, gn_b, seg):
    b, s, w = r.shape
    tok = pl.BlockSpec((1, ROW_TILE, w), lambda i, j: (i, j, 0))
    vec = pl.BlockSpec((1, w), lambda i, j: (0, 0))
    return pl.pallas_call(
        _rwkv_readout_kernel,
        grid=(b, s // ROW_TILE),
        in_specs=[tok, tok, tok, tok, tok, tok, vec, vec, vec, pl.BlockSpec((w, w), lambda i, j: (0, 0))],
        out_specs=tok,
        out_shape=jax.ShapeDtypeStruct((b, s, w), BF16),
        compiler_params=_params("parallel", "parallel"),
        name="rwkv_readout",
    )(y_f, y_b, r, k, v, g, r_k, gn_w, gn_b, seg)


def _merge_kernel(x_ref, oa_ref, or_ref, cv_ref, hp_ref, hn_ref, gt_ref, mt_ref, cw_ref, wa_ref, wr_ref, wc_ref,
                  wo_ref, o_ref, *, n_lat_tiles, n_tiles):
    j = pl.program_id(1)
    first, last = _tile_ends(j, n_lat_tiles, n_tiles)
    cw = CONV_WIDTH
    d = x_ref.shape[2]
    cv = cv_ref[0]
    z = cv[:, cw:2 * cw] * cv[:, 2 * cw:]
    zp = hp_ref[0, :, cw:2 * cw] * hp_ref[0, :, 2 * cw:]
    zn = hn_ref[0, :, cw:2 * cw] * hn_ref[0, :, 2 * cw:]
    z_prev, z_next = _neighbours(z, zp, zn, first, last)
    o_cv = cv[:, :cw] * (cw_ref[0:1] * z_prev + cw_ref[1:2] * z + cw_ref[2:3] * z_next)
    gt = gt_ref[0]
    m = (_sigmoid(gt[:, :d]) * _dg(oa_ref[0], wa_ref[...])
         + _sigmoid(gt[:, d:2 * d]) * _dg(or_ref[0], wr_ref[...])
         + _sigmoid(gt[:, 2 * d:]) * _dg(o_cv.astype(BF16), wc_ref[...]))
    y = _dg(m.astype(BF16), wo_ref[...])
    o_ref[0] = x_ref[0] + mt_ref[0, 0, 2:3] * y


def _merge(xc, o_att, o_rw, cv, gt, mt, conv_w, wa, wr, wc, wo, n_lat_tiles):
    b, s, d = xc.shape
    n_tiles = s // ROW_TILE
    hp, hn = _halo_specs(cv.shape[2], n_tiles)
    tok = lambda n: pl.BlockSpec((1, ROW_TILE, n), lambda i, j: (i, j, 0))
    full = lambda a: pl.BlockSpec(a.shape, lambda i, j: (0,) * a.ndim)
    return pl.pallas_call(
        functools.partial(_merge_kernel, n_lat_tiles=n_lat_tiles, n_tiles=n_tiles),
        grid=(b, n_tiles),
        in_specs=[tok(d), tok(o_att.shape[2]), tok(o_rw.shape[2]), tok(cv.shape[2]), hp, hn, tok(gt.shape[2]),
                  pl.BlockSpec((1, 1, 6, d), lambda i, j: (i, j // n_lat_tiles, 0, 0)),
                  full(conv_w), full(wa), full(wr), full(wc), full(wo)],
        out_specs=tok(d),
        out_shape=jax.ShapeDtypeStruct((b, s, d), F32),
        compiler_params=_params("parallel", "parallel"),
        name="merge",
    )(xc, o_att, o_rw, cv, cv, cv, gt, mt, conv_w, wa, wr, wc, wo)


def _router_kernel(x_ref, gain_ref, mt_ref, wr_ref, h_ref, aff_ref):
    h = _rms(x_ref[0]) * gain_ref[...]
    h = h * (1.0 + mt_ref[0, 0, 4:5]) + mt_ref[0, 0, 3:4]
    h_ref[0] = h.astype(BF16)
    logits = _mm3(wr_ref[...], h, NT)
    e = jnp.exp(logits - jnp.max(logits, axis=0, keepdims=True))
    aff_ref[0] = e / jnp.sum(e, axis=0, keepdims=True)


def _router(xc, gain, mt, w_router_t, n_lat_tiles):
    b, s, d = xc.shape
    ne = w_router_t.shape[0]
    return pl.pallas_call(
        _router_kernel,
        grid=(b, s // ROW_TILE),
        in_specs=[pl.BlockSpec((1, ROW_TILE, d), lambda i, j: (i, j, 0)),
                  pl.BlockSpec((1, d), lambda i, j: (0, 0)),
                  pl.BlockSpec((1, 1, 6, d), lambda i, j: (i, j // n_lat_tiles, 0, 0)),
                  pl.BlockSpec((ne, d), lambda i, j: (0, 0))],
        out_specs=[pl.BlockSpec((1, ROW_TILE, d), lambda i, j: (i, j, 0)),
                   pl.BlockSpec((1, ne, ROW_TILE), lambda i, j: (i, 0, j))],
        out_shape=[jax.ShapeDtypeStruct((b, s, d), BF16), jax.ShapeDtypeStruct((b, ne, s), F32)],
        compiler_params=_params("parallel", "parallel"),
        name="router",
    )(xc, gain, mt, w_router_t)


def _select_kernel(aff_ref, pos_ref, gate_ref, *, cap):
    a = aff_ref[0]
    ne, n = a.shape
    bits = pltpu.bitcast(a, jnp.int32)

    def count(mask):
        return jnp.sum(jnp.where(mask, 1.0, 0.0), axis=1, keepdims=True)

    def body(_, carry):
        lo, hi = carry
        mid = lo + ((hi - lo + 1) >> 1)
        ok = count(bits >= mid) >= cap
        return jnp.where(ok, mid, lo), jnp.where(ok, hi, mid - 1)

    lo0 = jnp.zeros((ne, 1), jnp.int32)
    hi0 = jnp.full((ne, 1), 0x7F800000, jnp.int32)
    thr, _ = lax.fori_loop(0, 32, body, (lo0, hi0))
    gt = bits > thr
    eq = bits == thr
    need = cap - count(gt)

    def tokens_before(mask):
        m = jnp.where(mask, 1.0, 0.0).astype(BF16)
        blk = min(n, SELECT_BLOCK)
        cols = []
        for j in range(n // blk):
            s_idx = lax.broadcasted_iota(jnp.int32, (n, blk), 0)
            t_idx = lax.broadcasted_iota(jnp.int32, (n, blk), 1) + j * blk
            cols.append(_dg(m, jnp.where(s_idx < t_idx, 1.0, 0.0).astype(BF16)))
        return jnp.concatenate(cols, axis=1)

    sel = jnp.logical_or(gt, jnp.logical_and(eq, tokens_before(eq) < need))
    pos_ref[0] = jnp.where(sel, tokens_before(sel).astype(jnp.int32), -1)
    gate_ref[0] = jnp.where(sel, a, 0.0)


def _select(aff_t, tok0, n, cap):
    b, ne, _ = aff_t.shape
    blk = tok0 // n
    return pl.pallas_call(
        functools.partial(_select_kernel, cap=cap),
        grid=(b,),
        in_specs=[pl.BlockSpec((1, ne, n), lambda i: (i, 0, blk))],
        out_specs=[pl.BlockSpec((1, ne, n), lambda i: (i, 0, 0)), pl.BlockSpec((1, ne, n), lambda i: (i, 0, 0))],
        out_shape=[jax.ShapeDtypeStruct((b, ne, n), jnp.int32), jax.ShapeDtypeStruct((b, ne, n), F32)],
        compiler_params=_params("parallel"),
        name="moe_select",
    )(aff_t)


def _expert_kernel(h_ref, pos_ref, gate_ref, wg_ref, wu_ref, wd_ref, o_ref, *, cap):
    e = pl.program_id(1)

    @pl.when(e == 0)
    def _():
        o_ref[...] = jnp.zeros_like(o_ref)

    n = h_ref.shape[1]
    pos = pos_ref[0, 0]
    slot = lax.broadcasted_iota(jnp.int32, (cap, n), 0)
    onehot = jnp.where(pos == slot, 1.0, 0.0).astype(BF16)
    xe = _dg(onehot, h_ref[0]).astype(BF16)
    g8 = jnp.broadcast_to(gate_ref[0, 0], (8, n))
    gate_slot = _mm_exact_lhs(onehot, g8, NT)[:, 0:1]
    hg = _dg(xe, wg_ref[0])
    hu = _dg(xe, wu_ref[0])
    hid = (hg * _sigmoid(hg) * hu).astype(BF16)
    ye = _dg(hid, wd_ref[0]) * gate_slot
    hi, lo = _split2(ye)
    o_ref[0] += _dg(onehot, hi, TN) + _dg(onehot, lo, TN)


def _experts(h, pos, gate, wg, wu, wd, tok0, n, cap):
    b, s, d = h.shape
    ne, _, f = wg.shape
    blk = tok0 // n
    sel = pl.BlockSpec((1, 1, 1, n), lambda i, e: (i, e, 0, 0))
    return pl.pallas_call(
        functools.partial(_expert_kernel, cap=cap),
        grid=(b, ne),
        in_specs=[pl.BlockSpec((1, n, d), lambda i, e: (i, blk, 0)), sel, sel,
                  pl.BlockSpec((1, d, f), lambda i, e: (e, 0, 0)),
                  pl.BlockSpec((1, d, f), lambda i, e: (e, 0, 0)),
                  pl.BlockSpec((1, f, d), lambda i, e: (e, 0, 0))],
        out_specs=pl.BlockSpec((1, n, d), lambda i, e: (i, 0, 0)),
        out_shape=jax.ShapeDtypeStruct((b, n, d), F32),
        compiler_params=_params("parallel", "arbitrary"),
        name="moe_experts",
    )(h, pos.reshape(b, ne, 1, n), gate.reshape(b, ne, 1, n), wg, wu, wd)


def _residual_kernel(x_ref, m_ref, mt_ref, o_ref):
    o_ref[0] = x_ref[0] + mt_ref[0, 0, 5:6] * m_ref[0]


def _moe_residual(xc, moe, mt, tok0, stream):
    b, s, d = xc.shape
    n = moe.shape[1]
    t0 = tok0 // ROW_TILE
    return pl.pallas_call(
        _residual_kernel,
        grid=(b, n // ROW_TILE),
        in_specs=[pl.BlockSpec((1, ROW_TILE, d), lambda i, j: (i, t0 + j, 0)),
                  pl.BlockSpec((1, ROW_TILE, d), lambda i, j: (i, j, 0)),
                  pl.BlockSpec((1, 1, 6, d), lambda i, j: (i, stream, 0, 0))],
        out_specs=pl.BlockSpec((1, ROW_TILE, d), lambda i, j: (i, t0 + j, 0)),
        out_shape=jax.ShapeDtypeStruct((b, s, d), F32),
        input_output_aliases={0: 0},
        compiler_params=_params("parallel", "parallel"),
        name="moe_residual",
    )(xc, moe, mt)


def _final_norm_kernel(x_ref, g_ref, o_ref):
    o_ref[0] = _rms(x_ref[0]) * g_ref[...]


def _final_norm(xc, gain, n_lat):
    b, s, d = xc.shape
    return pl.pallas_call(
        _final_norm_kernel,
        grid=(b, n_lat // ROW_TILE),
        in_specs=[pl.BlockSpec((1, ROW_TILE, d), lambda i, j: (i, j, 0)),
                  pl.BlockSpec((1, d), lambda i, j: (0, 0))],
        out_specs=pl.BlockSpec((1, ROW_TILE, d), lambda i, j: (i, j, 0)),
        out_shape=jax.ShapeDtypeStruct((b, n_lat, d), F32),
        compiler_params=_params("parallel", "parallel"),
        name="final_norm",
    )(xc, gain)


def _rope_tables(n_lat, n_ctx):
    rows = n_lat // GRID_W
    row = jnp.repeat(jnp.arange(rows, dtype=F32), GRID_W)
    col = jnp.tile(jnp.arange(GRID_W, dtype=F32), rows)
    axis_dim = HEAD_DIM // 2
    inv_freq = ROPE_THETA ** (-jnp.arange(0, axis_dim, 2, dtype=F32) / axis_dim)
    ang_r = row[:, None] * inv_freq[None, :]
    ang_c = col[:, None] * inv_freq[None, :]
    cos_h = jnp.concatenate([jnp.cos(ang_r), jnp.cos(ang_r), jnp.cos(ang_c), jnp.cos(ang_c)], axis=1)
    sin_h = jnp.concatenate([-jnp.sin(ang_r), jnp.sin(ang_r), -jnp.sin(ang_c), jnp.sin(ang_c)], axis=1)
    cos_h = jnp.concatenate([cos_h, jnp.ones((n_ctx, HEAD_DIM), F32)], axis=0)
    sin_h = jnp.concatenate([sin_h, jnp.zeros((n_ctx, HEAD_DIM), F32)], axis=0)
    return jnp.tile(cos_h, (1, ATT_HEADS)), jnp.tile(sin_h, (1, ATT_HEADS))


def _head_sum_matrix(width):
    idx = np.arange(width) // HEAD_DIM
    return jnp.asarray(idx[:, None] == idx[None, :], dtype=BF16)


def _two_dir_lowrank(w2):
    _, rank, w = w2.shape
    z = jnp.zeros((rank, w), w2.dtype)
    return jnp.concatenate([jnp.concatenate([w2[0], z], axis=1), jnp.concatenate([z, w2[1]], axis=1)], axis=0)


def kernel(x, c, ctx, c_ctx, ada_w, ada_b, norm1, w_in, q_gain, k_gain, shift_mu, decay_w0, decay_w2, iclr_a0, iclr_a2, gate_g2, rwkv_kk, rwkv_ka, rwkv_rk, rwkv_gn_w, rwkv_gn_b, conv_w, w_br_att, w_br_rwkv, w_br_conv, w_out, norm2, w_router, exp_gate, exp_up, exp_down, final_norm):
    b, n_lat, d = x.shape
    n_ctx = ctx.shape[1]
    depth = ada_w.shape[0]
    assert n_lat % ROW_TILE == 0 and n_ctx % ROW_TILE == 0 and n_lat % n_ctx == 0
    n_lat_tiles = n_lat // ROW_TILE
    w = RWKV_WIDTH

    pad = (-(b + 1)) % 8
    cc = jnp.concatenate([c, c_ctx[None, :], jnp.zeros((pad, d), F32)], axis=0)
    mods = _ada_table(cc, ada_w, ada_b)

    cos_t, sin_t = _rope_tables(n_lat, n_ctx)
    seg = _head_sum_matrix(ATT_WIDTH)
    att_w = ATT_WIDTH + 2 * ATT_KV_WIDTH
    cv_w = 3 * CONV_WIDTH
    offs = np.cumsum([0, att_w, RWKV_SEG, cv_w, 3 * d])

    xc = jnp.concatenate([x, ctx], axis=1)
    for l in range(depth):
        mod_lat = mods[l, :b].reshape(b, 1, 6, d)
        mod_ctx = jnp.broadcast_to(mods[l, b].reshape(1, 1, 6, d), (b, 1, 6, d))
        mt = jnp.concatenate([mod_lat, mod_ctx], axis=1)

        w_l = w_in[l].astype(BF16)
        gain1 = norm1[l].reshape(1, d)
        att, rw, cv, gt = [_normmod_mm(xc, gain1, mt, w_l[:, offs[i]:offs[i + 1]], n_lat_tiles) for i in range(4)]

        q, k, v = _att_prep(att, cos_t, sin_t, jnp.tile(q_gain[l], ATT_HEADS).reshape(1, -1),
                            jnp.tile(k_gain[l], ATT_KV_HEADS).reshape(1, -1), seg)
        o_att = _attention(q, k, v, n_lat)

        r, kx, vx, g, nkk, lw, kd, bb = _rwkv_prep(
            rw, shift_mu[l].reshape(1, -1), _two_dir_lowrank(decay_w2[l]), decay_w0[l].reshape(1, -1),
            _two_dir_lowrank(iclr_a2[l]), iclr_a0[l].reshape(1, -1), gate_g2[l],
            rwkv_kk[l].reshape(1, -1), rwkv_ka[l].reshape(1, -1), seg, n_lat_tiles)
        y_f, y_b = _rwkv_scan(r, vx, nkk, lw, kd, bb, n_lat)
        o_rw = _rwkv_readout(y_f, y_b, r, kx, vx, g, rwkv_rk[l].reshape(1, -1), rwkv_gn_w[l].reshape(1, -1),
                             rwkv_gn_b[l].reshape(1, -1), seg)

        xc = _merge(xc, o_att, o_rw, cv, gt, mt, conv_w[l], w_br_att[l].astype(BF16), w_br_rwkv[l].astype(BF16),
                    w_br_conv[l].astype(BF16), w_out[l].astype(BF16), n_lat_tiles)

        h2, aff_t = _router(xc, norm2[l].reshape(1, d), mt, w_router[l].T, n_lat_tiles)
        wg, wu, wd = exp_gate[l].astype(BF16), exp_up[l].astype(BF16), exp_down[l].astype(BF16)
        streams = [(0, n_lat, 0)] + ([(n_lat, n_ctx, 1)] if l < depth - 1 else [])
        for tok0, n, stream in streams:
            cap = CAPACITY_FACTOR * n // N_EXPERTS
            pos, gate = _select(aff_t, tok0, n, cap)
            moe = _experts(h2, pos, gate, wg, wu, wd, tok0, n, cap)
            xc = _moe_residual(xc, moe, mt, tok0, stream)
    return _final_norm(xc, final_norm.reshape(1, d), n_lat)
```

```python
import functools

import numpy as np
import jax
import jax.numpy as jnp
from jax import lax
from jax.experimental import pallas as pl
from jax.experimental.pallas import tpu as pltpu

F32 = jnp.float32
BF16 = jnp.bfloat16

GRID_W = 64
NORM_EPS = 1e-6
LOG2_E = float(np.log2(np.e))
ATT_HEADS = 8
ATT_KV_HEADS = 2
HEAD_DIM = 64
ATT_GROUP = ATT_HEADS // ATT_KV_HEADS
ATT_WIDTH = ATT_HEADS * HEAD_DIM
ATT_KV_WIDTH = ATT_KV_HEADS * HEAD_DIM
ROPE_THETA = 10000.0
RWKV_HEADS = 8
RWKV_WIDTH = RWKV_HEADS * HEAD_DIM
DECAY_RANK = 64
ICLR_RANK = 64
GATE_RANK = 128
RWKV_GN_EPS = 64e-5
RWKV_SEG = 3 * RWKV_WIDTH + 2 * DECAY_RANK + 2 * ICLR_RANK + GATE_RANK
CONV_WIDTH = 512
N_EXPERTS = 16
CAPACITY_FACTOR = 2

ROW_TILE = 256
CHUNK = 64
QUAD = 4 * HEAD_DIM
SCAN_BATCH = 2
HALO = 8
SELECT_BLOCK = 256
EXPERT_ROWS = 256
VMEM_LIMIT = 56 * 1024 * 1024

NT = (((1,), (1,)), ((), ()))
TN = (((0,), (0,)), ((), ()))
NN = (((1,), (0,)), ((), ()))


def _params(*sem):
    return pltpu.CompilerParams(dimension_semantics=sem, vmem_limit_bytes=VMEM_LIMIT)


def _dg(a, b, dn=NN):
    return lax.dot_general(a, b, dn, preferred_element_type=F32)


def _split2(x):
    hi = x.astype(BF16)
    lo = (x - hi.astype(F32)).astype(BF16)
    return hi, lo


def _split3(x):
    hi = x.astype(BF16)
    r = x - hi.astype(F32)
    mid = r.astype(BF16)
    lo = (r - mid.astype(F32)).astype(BF16)
    return hi, mid, lo


def _mm1(a, b, dn=NN):
    return _dg(a.astype(BF16), b.astype(BF16), dn)


def _mm3(a, b, dn=NN):
    ah, al = _split2(a)
    bh, bl = _split2(b)
    return _dg(ah, bh, dn) + (_dg(ah, bl, dn) + _dg(al, bh, dn))


def _mm_exact_lhs(a_bf16, b, dn=NN):
    h, m, l = _split3(b)
    return _dg(a_bf16, h, dn) + (_dg(a_bf16, m, dn) + _dg(a_bf16, l, dn))


def _mm_exact_rhs(a, b_bf16, dn=NN):
    h, m, l = _split3(a)
    return _dg(h, b_bf16, dn) + (_dg(m, b_bf16, dn) + _dg(l, b_bf16, dn))


def _sigmoid(x):
    return 1.0 / (1.0 + jnp.exp(-x))


def _rms(x):
    return x * lax.rsqrt(jnp.mean(x * x, axis=-1, keepdims=True) + NORM_EPS)


def _ada_kernel(c_ref, w_ref, b_ref, o_ref):
    c = c_ref[...]
    s = c * _sigmoid(c)
    o_ref[0] = _mm3(s, w_ref[0]) + b_ref[0]


def _ada_table(cc, ada_w, ada_b):
    n_layers, d, six_d = ada_w.shape
    rows = cc.shape[0]
    tn = six_d // 4
    return pl.pallas_call(
        _ada_kernel,
        grid=(n_layers, six_d // tn),
        in_specs=[pl.BlockSpec((rows, d), lambda l, j: (0, 0)),
                  pl.BlockSpec((1, d, tn), lambda l, j: (l, 0, j)),
                  pl.BlockSpec((1, 1, tn), lambda l, j: (l, 0, j))],
        out_specs=pl.BlockSpec((1, rows, tn), lambda l, j: (l, 0, j)),
        out_shape=jax.ShapeDtypeStruct((n_layers, rows, six_d), F32),
        compiler_params=_params("parallel", "parallel"),
        name="ada_table",
    )(cc, ada_w, ada_b.reshape(n_layers, 1, six_d))


def _in_proj_kernel(x_ref, gain_ref, mt_ref, *refs):
    n = len(refs) // 2
    h = _rms(x_ref[0]) * gain_ref[...]
    h = (h * (1.0 + mt_ref[0, 0, 1:2]) + mt_ref[0, 0, 0:1]).astype(BF16)
    for w_ref, o_ref in zip(refs[:n], refs[n:]):
        o_ref[0] = _dg(h, w_ref[...]).astype(o_ref.dtype)


def _in_proj(xc, gain, mt, weights, out_dtypes, n_lat_tiles):
    b, s, d = xc.shape
    tok = lambda n: pl.BlockSpec((1, ROW_TILE, n), lambda i, j: (i, j, 0))
    return pl.pallas_call(
        _in_proj_kernel,
        grid=(b, s // ROW_TILE),
        in_specs=[tok(d), pl.BlockSpec((1, d), lambda i, j: (0, 0)),
                  pl.BlockSpec((1, 1, 6, d), lambda i, j: (i, j // n_lat_tiles, 0, 0))]
                 + [pl.BlockSpec(w.shape, lambda i, j: (0, 0)) for w in weights],
        out_specs=[tok(w.shape[1]) for w in weights],
        out_shape=[jax.ShapeDtypeStruct((b, s, w.shape[1]), dt) for w, dt in zip(weights, out_dtypes)],
        compiler_params=_params("parallel", "parallel"),
        name="in_proj",
    )(xc, gain, mt, *weights)


def _swap16(x):
    w = x.shape[-1]
    lane = lax.broadcasted_iota(jnp.int32, x.shape, x.ndim - 1)
    return jnp.where((lane & 16) == 0, pltpu.roll(x, w - 16, x.ndim - 1), pltpu.roll(x, 16, x.ndim - 1))


def _att_prep_kernel(a_ref, cos_ref, sin_ref, qg_ref, kg_ref, seg_ref, q_ref, k_ref, v_ref):
    a = a_ref[0].astype(F32)
    seg = seg_ref[...]

    def norm_rope(u, gain, cos, sin, segm):
        hi, lo = _split2(u * u)
        ms = (_dg(hi, segm) + _dg(lo, segm)) * (1.0 / HEAD_DIM)
        un = u * lax.rsqrt(ms + NORM_EPS) * gain
        return un * cos + _swap16(un) * sin

    q = norm_rope(a[:, :ATT_WIDTH], qg_ref[...], cos_ref[...], sin_ref[...], seg)
    kw = ATT_KV_WIDTH
    k = norm_rope(a[:, ATT_WIDTH:ATT_WIDTH + kw], kg_ref[...], cos_ref[:, :kw], sin_ref[:, :kw], seg[:kw, :kw])
    q_ref[0] = (q * (HEAD_DIM ** -0.5 * LOG2_E)).astype(BF16)
    k_ref[0] = k.astype(BF16)
    v_ref[0] = a_ref[0, :, ATT_WIDTH + kw:]


def _att_prep(att, cos_t, sin_t, q_gain_t, k_gain_t, seg):
    b, s, w = att.shape
    row = lambda n: pl.BlockSpec((1, ROW_TILE, n), lambda i, j: (i, j, 0))
    return pl.pallas_call(
        _att_prep_kernel,
        grid=(b, s // ROW_TILE),
        in_specs=[row(w),
                  pl.BlockSpec((ROW_TILE, ATT_WIDTH), lambda i, j: (j, 0)),
                  pl.BlockSpec((ROW_TILE, ATT_WIDTH), lambda i, j: (j, 0)),
                  pl.BlockSpec((1, ATT_WIDTH), lambda i, j: (0, 0)),
                  pl.BlockSpec((1, ATT_KV_WIDTH), lambda i, j: (0, 0)),
                  pl.BlockSpec((ATT_WIDTH, ATT_WIDTH), lambda i, j: (0, 0))],
        out_specs=[row(ATT_WIDTH), row(ATT_KV_WIDTH), row(ATT_KV_WIDTH)],
        out_shape=[jax.ShapeDtypeStruct((b, s, ATT_WIDTH), BF16),
                   jax.ShapeDtypeStruct((b, s, ATT_KV_WIDTH), BF16),
                   jax.ShapeDtypeStruct((b, s, ATT_KV_WIDTH), BF16)],
        compiler_params=_params("parallel", "parallel"),
        name="att_prep",
    )(att, cos_t, sin_t, q_gain_t, k_gain_t, seg)


def _attn_kernel(q_ref, k_ref, v_ref, o_ref, *, n_lat, n_lat_tiles):
    j = pl.program_id(1)
    s_all = k_ref.shape[1]

    def run(k0, k1):
        for kvh in range(ATT_KV_HEADS):
            k = k_ref[0, k0:k1, kvh * HEAD_DIM:(kvh + 1) * HEAD_DIM]
            v = v_ref[0, k0:k1, kvh * HEAD_DIM:(kvh + 1) * HEAD_DIM]
            for g in range(ATT_GROUP):
                h = kvh * ATT_GROUP + g
                q = q_ref[0, :, h * HEAD_DIM:(h + 1) * HEAD_DIM]
                s = _dg(q, k, NT)
                p = jnp.exp2(s - jnp.max(s, axis=-1, keepdims=True))
                l = jnp.sum(p, axis=-1, keepdims=True)
                o = _dg(p.astype(BF16), v) / l
                o_ref[0, :, h * HEAD_DIM:(h + 1) * HEAD_DIM] = o.astype(BF16)

    @pl.when(j < n_lat_tiles)
    def _():
        run(0, s_all)

    @pl.when(j >= n_lat_tiles)
    def _():
        run(n_lat, s_all)


def _attention(q, k, v, n_lat):
    b, s, _ = q.shape
    return pl.pallas_call(
        functools.partial(_attn_kernel, n_lat=n_lat, n_lat_tiles=n_lat // ROW_TILE),
        grid=(b, s // ROW_TILE),
        in_specs=[pl.BlockSpec((1, ROW_TILE, ATT_WIDTH), lambda i, j: (i, j, 0)),
                  pl.BlockSpec((1, s, ATT_KV_WIDTH), lambda i, j: (i, 0, 0)),
                  pl.BlockSpec((1, s, ATT_KV_WIDTH), lambda i, j: (i, 0, 0))],
        out_specs=pl.BlockSpec((1, ROW_TILE, ATT_WIDTH), lambda i, j: (i, j, 0)),
        out_shape=jax.ShapeDtypeStruct((b, s, ATT_WIDTH), BF16),
        compiler_params=_params("parallel", "parallel"),
        name="attention",
    )(q, k, v)


def _neighbours(x, halo_prev, halo_next, first, last):
    rows = x.shape[0]
    ridx = lax.broadcasted_iota(jnp.int32, x.shape, 0)
    row_p = jnp.where(first, 0.0, halo_prev[halo_prev.shape[0] - 1:])
    row_n = jnp.where(last, 0.0, halo_next[0:1])
    prev = jnp.where(ridx == 0, row_p, pltpu.roll(x, 1, 0))
    nxt = jnp.where(ridx == rows - 1, row_n, pltpu.roll(x, rows - 1, 0))
    return prev, nxt


def _tile_ends(j, n_lat_tiles, n_tiles):
    first = jnp.logical_or(j == 0, j == n_lat_tiles)
    last = jnp.logical_or(j == n_lat_tiles - 1, j == n_tiles - 1)
    return first, last


def _halo_specs(width, n_tiles, dtype):
    halo = HALO * (4 // jnp.dtype(dtype).itemsize)
    per = ROW_TILE // halo
    prev = pl.BlockSpec((1, halo, width), lambda i, j: (i, jnp.maximum(j * per - 1, 0), 0))
    nxt = pl.BlockSpec((1, halo, width), lambda i, j: (i, jnp.minimum((j + 1) * per, n_tiles * per - 1), 0))
    return prev, nxt


def _rwkv_prep_kernel(x_ref, hp_ref, hn_ref, mu_ref, w2_ref, w0_ref, a2_ref, a0_ref, g2_ref, kk_ref, ka_ref,
                      seg_ref, r_ref, k_ref, v_ref, g_ref, nkk_ref, lw_ref, kd_ref, bb_ref, *, n_lat_tiles, n_tiles):
    j = pl.program_id(1)
    first, last = _tile_ends(j, n_lat_tiles, n_tiles)
    x = x_ref[0]
    prev, nxt = _neighbours(x, hp_ref[0], hn_ref[0], first, last)
    xs = x + mu_ref[...] * (0.5 * (prev + nxt) - x)
    w = RWKV_WIDTH
    r = xs[:, 0:w]
    k = xs[:, w:2 * w]
    v = xs[:, 2 * w:3 * w]
    o = 3 * w
    w_lo = xs[:, o:o + 2 * DECAY_RANK]
    a_lo = xs[:, o + 2 * DECAY_RANK:o + 2 * DECAY_RANK + 2 * ICLR_RANK]
    g_lo = xs[:, o + 2 * DECAY_RANK + 2 * ICLR_RANK:]
    w_pre = w0_ref[...] + _mm3(jnp.tanh(w_lo), w2_ref[...])
    logw = -_sigmoid(w_pre) * float(np.exp(-0.5))
    a = _sigmoid(a0_ref[...] + _mm3(a_lo, a2_ref[...]))
    g = _mm3(_sigmoid(g_lo), g2_ref[...])
    kk = k * kk_ref[...]
    kk = kk * lax.rsqrt(_mm_exact_rhs(kk * kk, seg_ref[...]) + 1e-12)
    r_ref[0] = r.astype(BF16)
    k_ref[0] = k.astype(BF16)
    v_ref[0] = v.astype(BF16)
    g_ref[0] = g.astype(BF16)
    nkk_ref[0] = (-kk).astype(BF16)
    for d in range(2):
        a_d = a[:, d * w:(d + 1) * w]
        lw_ref[0, d] = logw[:, d * w:(d + 1) * w]
        kd_ref[0, d] = (k * (1.0 + (a_d - 1.0) * ka_ref[...])).astype(BF16)
        bb_ref[0, d] = (kk * a_d).astype(BF16)


def _rwkv_prep(rw, mu, w2cat, w0, a2cat, a0, g2, k_k, k_a, seg, n_lat_tiles):
    b, s, wseg = rw.shape
    n_tiles = s // ROW_TILE
    w = RWKV_WIDTH
    hp, hn = _halo_specs(wseg, n_tiles, rw.dtype)
    full = lambda shape: pl.BlockSpec(shape, lambda i, j: (0,) * len(shape))
    tok = pl.BlockSpec((1, ROW_TILE, w), lambda i, j: (i, j, 0))
    tok2 = pl.BlockSpec((1, 2, ROW_TILE, w), lambda i, j: (i, 0, j, 0))
    one = jax.ShapeDtypeStruct((b, s, w), BF16)
    two = jax.ShapeDtypeStruct((b, 2, s, w), BF16)
    two_f32 = jax.ShapeDtypeStruct((b, 2, s, w), F32)
    return pl.pallas_call(
        functools.partial(_rwkv_prep_kernel, n_lat_tiles=n_lat_tiles, n_tiles=n_tiles),
        grid=(b, n_tiles),
        in_specs=[pl.BlockSpec((1, ROW_TILE, wseg), lambda i, j: (i, j, 0)), hp, hn,
                  full((1, wseg)), full(w2cat.shape), full((1, 2 * w)), full(a2cat.shape), full((1, 2 * w)),
                  full(g2.shape), full((1, w)), full((1, w)), full(seg.shape)],
        out_specs=[tok, tok, tok, tok, tok, tok2, tok2, tok2],
        out_shape=[one, one, one, one, one, two_f32, two, two],
        compiler_params=_params("parallel", "parallel"),
        name="rwkv_prep",
    )(rw, rw, rw, mu, w2cat, w0, a2cat, a0, g2, k_k, k_a, seg)


def _block_diag(y, head_masks):
    return jnp.concatenate([jnp.where(m, y, 0.0) for m in head_masks], axis=0)


def _scan_chain(sgn, r, v, nkk, lw, kd, bb, s0):
    row = lax.broadcasted_iota(jnp.int32, (CHUNK, QUAD), 0)
    lane = lax.broadcasted_iota(jnp.int32, (CHUNK, QUAD), 1)
    rel = ((lane % CHUNK) - row) * sgn
    strict = rel < 0
    incl = rel <= 0
    eye = jnp.where(rel == 0, 1.0, 0.0)
    head_masks = [(lane // HEAD_DIM) == h for h in range(4)]
    bd_mask = (lax.broadcasted_iota(jnp.int32, (QUAD, QUAD), 0) // HEAD_DIM
               == lax.broadcasted_iota(jnp.int32, (QUAD, QUAD), 1) // HEAD_DIM)
    t_row = lax.broadcasted_iota(jnp.int32, (CHUNK, CHUNK), 0)
    t_col = lax.broadcasted_iota(jnp.int32, (CHUNK, CHUNK), 1)
    tri = jnp.where((t_col - t_row) * sgn <= 0, 1.0, 0.0).astype(BF16)
    last_row = CHUNK - 1 if sgn > 0 else 0
    bd = lambda t: _block_diag(t, head_masks).astype(BF16)

    cum = _mm_exact_lhs(tri, lw)
    yield
    cum_end = cum[last_row:last_row + 1]
    w_inv = jnp.exp(-cum)
    w_rem = jnp.exp(cum_end - cum)
    ar = jnp.concatenate([nkk * jnp.exp(cum - lw), r * jnp.exp(cum)], axis=0).astype(BF16)
    xb = _dg(ar, bd(bb * w_inv), NT)
    xk = _dg(ar, bd(kd * w_inv), NT)
    ars = _dg(ar, s0.astype(BF16), NT)
    yield
    n_ab = jnp.where(strict, xb[:CHUNK], 0.0)
    l_ak = jnp.where(strict, xk[:CHUNK], 0.0)
    g_rb = jnp.where(incl, xb[CHUNK:], 0.0)
    g_rk = jnp.where(incl, xk[CHUNK:], 0.0)

    m = eye + n_ab
    p = _dg(n_ab.astype(BF16), bd(n_ab))
    lg = _dg(jnp.concatenate([l_ak, g_rk], axis=0).astype(BF16), bd(v))
    yield
    for _ in range(int(np.log2(CHUNK)) - 2):
        both = _dg(jnp.concatenate([p, m], axis=0).astype(BF16), bd(p))
        yield
        p = both[:CHUNK]
        m = m + both[CHUNK:]
    m = m + _dg(m.astype(BF16), bd(p))
    yield
    x0 = ars[:CHUNK] + lg[:CHUNK]
    u = _dg(m.astype(BF16), bd(x0))
    yield
    y = ars[CHUNK:] + _dg(g_rb.astype(BF16), bd(u)) + lg[CHUNK:]
    z = _dg(jnp.concatenate([u, v], axis=0).astype(BF16),
            jnp.concatenate([bb * w_rem, kd * w_rem], axis=0).astype(BF16), TN)
    yield
    return y, s0 * jnp.exp(cum_end) + jnp.where(bd_mask, z, 0.0)


def _run_interleaved(chains):
    results = [None] * len(chains)
    live = list(enumerate(chains))
    while live:
        still = []
        for idx, g in live:
            try:
                next(g)
                still.append((idx, g))
            except StopIteration as stop:
                results[idx] = stop.value
        live = still
    return results


def _rwkv_scan_kernel(rf_ref, vf_ref, nf_ref, lwf_ref, kdf_ref, bbf_ref, rb_ref, vb_ref, nb_ref, lwb_ref, kdb_ref,
                      bbb_ref, yf_ref, yb_ref, s_ref):
    @pl.when(pl.program_id(1) == 0)
    def _():
        s_ref[...] = jnp.zeros_like(s_ref)

    dirs = ((1, rf_ref, vf_ref, nf_ref, lwf_ref, kdf_ref, bbf_ref, yf_ref),
            (-1, rb_ref, vb_ref, nb_ref, lwb_ref, kdb_ref, bbb_ref, yb_ref))
    work = []
    for i in range(SCAN_BATCH):
        for d, (sgn, r_ref, v_ref, n_ref, lw_ref, kd_ref, bb_ref, y_ref) in enumerate(dirs):
            for q in range(RWKV_WIDTH // QUAD):
                ql = slice(q * QUAD, (q + 1) * QUAD)
                f32 = lambda t: t.astype(F32)
                args = (f32(r_ref[i, :, ql]), f32(v_ref[i, :, ql]), f32(n_ref[i, :, ql]), lw_ref[i, 0, :, ql],
                        f32(kd_ref[i, 0, :, ql]), f32(bb_ref[i, 0, :, ql]), s_ref[i, d, q])
                work.append((sgn, args, y_ref, (i, slice(None), ql), (i, d, q)))
    results = _run_interleaved([_scan_chain(sgn, *args) for sgn, args, _, _, _ in work])
    for (_, _, y_ref, y_idx, s_idx), (y, s_new) in zip(work, results):
        y_ref[y_idx] = y
        s_ref[s_idx] = s_new


def _rwkv_scan(r, v, nkk, lw, kd, bb, n_lat):
    b, s, w = r.shape
    nc = s // CHUNK
    nc_lat = n_lat // CHUNK
    nc_ctx = nc - nc_lat
    nb = SCAN_BATCH
    assert b % nb == 0

    fwd = lambda c: jnp.where(c < nc_ctx, nc_lat + c, c - nc_ctx)
    bwd = lambda c: jnp.where(c < nc_ctx, nc - 1 - c, nc_lat - 1 - (c - nc_ctx))
    tok = lambda order: pl.BlockSpec((nb, CHUNK, w), lambda i, c: (i, order(c), 0))
    tok2 = lambda order, d: pl.BlockSpec((nb, 1, CHUNK, w), lambda i, c: (i, d, order(c), 0))
    out = jax.ShapeDtypeStruct((b, s, w), F32)
    return pl.pallas_call(
        _rwkv_scan_kernel,
        grid=(b // nb, nc),
        in_specs=[tok(fwd), tok(fwd), tok(fwd), tok2(fwd, 0), tok2(fwd, 0), tok2(fwd, 0),
                  tok(bwd), tok(bwd), tok(bwd), tok2(bwd, 1), tok2(bwd, 1), tok2(bwd, 1)],
        out_specs=[tok(fwd), tok(bwd)],
        out_shape=[out, out],
        scratch_shapes=[pltpu.VMEM((nb, 2, w // QUAD, QUAD, QUAD), F32)],
        compiler_params=_params("parallel", "arbitrary"),
        name="rwkv_scan",
    )(r, v, nkk, lw, kd, bb, r, v, nkk, lw, kd, bb)


def _rwkv_readout_kernel(yf_ref, yb_ref, r_ref, k_ref, v_ref, g_ref, rk_ref, gw_ref, gb_ref, seg_ref, o_ref):
    seg = seg_ref[...]
    y = yf_ref[0] + yb_ref[0]
    mean = _mm_exact_rhs(y, seg) * (1.0 / HEAD_DIM)
    yc = y - mean
    var = _mm_exact_rhs(yc * yc, seg) * (1.0 / HEAD_DIM)
    yn = yc * lax.rsqrt(var + RWKV_GN_EPS)
    f32 = lambda ref: ref[0].astype(F32)
    bonus = _mm_exact_rhs(f32(r_ref) * f32(k_ref) * rk_ref[...], seg) * f32(v_ref)
    o_ref[0] = ((yn * gw_ref[...] + gb_ref[...] + bonus) * f32(g_ref)).astype(BF16)


def _rwkv_readout(y_f, y_b, r, k, v, g, r_k, gn_w, gn_b, seg):
    b, s, w = r.shape
    tok = pl.BlockSpec((1, ROW_TILE, w), lambda i, j: (i, j, 0))
    vec = pl.BlockSpec((1, w), lambda i, j: (0, 0))
    return pl.pallas_call(
        _rwkv_readout_kernel,
        grid=(b, s // ROW_TILE),
        in_specs=[tok, tok, tok, tok, tok, tok, vec, vec, vec, pl.BlockSpec((w, w), lambda i, j: (0, 0))],
        out_specs=tok,
        out_shape=jax.ShapeDtypeStruct((b, s, w), BF16),
        compiler_params=_params("parallel", "parallel"),
        name="rwkv_readout",
    )(y_f, y_b, r, k, v, g, r_k, gn_w, gn_b, seg)


def _merge_kernel(x_ref, oa_ref, or_ref, cv_ref, hp_ref, hn_ref, gt_ref, mt_ref, cw_ref, wa_ref, wr_ref, wc_ref,
                  wo_ref, o_ref, *, n_lat_tiles, n_tiles):
    j = pl.program_id(1)
    first, last = _tile_ends(j, n_lat_tiles, n_tiles)
    cw = CONV_WIDTH
    d = x_ref.shape[2]
    cv = cv_ref[0].astype(F32)
    hp = hp_ref[0].astype(F32)
    hn = hn_ref[0].astype(F32)
    z = cv[:, cw:2 * cw] * cv[:, 2 * cw:]
    zp = hp[:, cw:2 * cw] * hp[:, 2 * cw:]
    zn = hn[:, cw:2 * cw] * hn[:, 2 * cw:]
    z_prev, z_next = _neighbours(z, zp, zn, first, last)
    o_cv = cv[:, :cw] * (cw_ref[0:1] * z_prev + cw_ref[1:2] * z + cw_ref[2:3] * z_next)
    gt = gt_ref[0].astype(F32)
    m = (_sigmoid(gt[:, :d]) * _dg(oa_ref[0], wa_ref[...])
         + _sigmoid(gt[:, d:2 * d]) * _dg(or_ref[0], wr_ref[...])
         + _sigmoid(gt[:, 2 * d:]) * _dg(o_cv.astype(BF16), wc_ref[...]))
    y = _dg(m.astype(BF16), wo_ref[...])
    o_ref[0] = x_ref[0] + mt_ref[0, 0, 2:3] * y


def _merge(xc, o_att, o_rw, cv, gt, mt, conv_w, wa, wr, wc, wo, n_lat_tiles):
    b, s, d = xc.shape
    n_tiles = s // ROW_TILE
    hp, hn = _halo_specs(cv.shape[2], n_tiles, cv.dtype)
    tok = lambda n: pl.BlockSpec((1, ROW_TILE, n), lambda i, j: (i, j, 0))
    full = lambda a: pl.BlockSpec(a.shape, lambda i, j: (0,) * a.ndim)
    return pl.pallas_call(
        functools.partial(_merge_kernel, n_lat_tiles=n_lat_tiles, n_tiles=n_tiles),
        grid=(b, n_tiles),
        in_specs=[tok(d), tok(o_att.shape[2]), tok(o_rw.shape[2]), tok(cv.shape[2]), hp, hn, tok(gt.shape[2]),
                  pl.BlockSpec((1, 1, 6, d), lambda i, j: (i, j // n_lat_tiles, 0, 0)),
                  full(conv_w), full(wa), full(wr), full(wc), full(wo)],
        out_specs=tok(d),
        out_shape=jax.ShapeDtypeStruct((b, s, d), F32),
        compiler_params=_params("parallel", "parallel"),
        name="merge",
    )(xc, o_att, o_rw, cv, cv, cv, gt, mt, conv_w, wa, wr, wc, wo)


def _router_kernel(x_ref, gain_ref, mt_ref, wr_ref, h_ref, aff_ref):
    h = _rms(x_ref[0]) * gain_ref[...]
    h = h * (1.0 + mt_ref[0, 0, 4:5]) + mt_ref[0, 0, 3:4]
    h_ref[0] = h.astype(BF16)
    logits = _mm3(wr_ref[...], h, NT)
    e = jnp.exp(logits - jnp.max(logits, axis=0, keepdims=True))
    aff_ref[0] = e / jnp.sum(e, axis=0, keepdims=True)


def _router(xc, gain, mt, w_router_t, n_lat_tiles):
    b, s, d = xc.shape
    ne = w_router_t.shape[0]
    return pl.pallas_call(
        _router_kernel,
        grid=(b, s // ROW_TILE),
        in_specs=[pl.BlockSpec((1, ROW_TILE, d), lambda i, j: (i, j, 0)),
                  pl.BlockSpec((1, d), lambda i, j: (0, 0)),
                  pl.BlockSpec((1, 1, 6, d), lambda i, j: (i, j // n_lat_tiles, 0, 0)),
                  pl.BlockSpec((ne, d), lambda i, j: (0, 0))],
        out_specs=[pl.BlockSpec((1, ROW_TILE, d), lambda i, j: (i, j, 0)),
                   pl.BlockSpec((1, ne, ROW_TILE), lambda i, j: (i, 0, j))],
        out_shape=[jax.ShapeDtypeStruct((b, s, d), BF16), jax.ShapeDtypeStruct((b, ne, s), F32)],
        compiler_params=_params("parallel", "parallel"),
        name="router",
    )(xc, gain, mt, w_router_t)


def _select_kernel(aff_ref, pos_ref, gate_ref, *, cap):
    a = aff_ref[0]
    ne, n = a.shape
    bits = pltpu.bitcast(a, jnp.int32)

    def count(mask):
        return jnp.sum(jnp.where(mask, 1.0, 0.0), axis=1, keepdims=True)

    def body(_, carry):
        lo, hi = carry
        mid = lo + ((hi - lo + 1) >> 1)
        ok = count(bits >= mid) >= cap
        return jnp.where(ok, mid, lo), jnp.where(ok, hi, mid - 1)

    lo0 = jnp.zeros((ne, 1), jnp.int32)
    hi0 = jnp.full((ne, 1), 0x7F800000, jnp.int32)
    thr, _ = lax.fori_loop(0, 32, body, (lo0, hi0))
    gt = bits > thr
    eq = bits == thr
    need = cap - count(gt)

    def tokens_before(mask):
        m = jnp.where(mask, 1.0, 0.0).astype(BF16)
        blk = min(n, SELECT_BLOCK)
        cols = []
        for j in range(n // blk):
            s_idx = lax.broadcasted_iota(jnp.int32, (n, blk), 0)
            t_idx = lax.broadcasted_iota(jnp.int32, (n, blk), 1) + j * blk
            cols.append(_dg(m, jnp.where(s_idx < t_idx, 1.0, 0.0).astype(BF16)))
        return jnp.concatenate(cols, axis=1)

    sel = jnp.logical_or(gt, jnp.logical_and(eq, tokens_before(eq) < need))
    pos_ref[0] = jnp.where(sel, tokens_before(sel).astype(jnp.int32), -1)
    gate_ref[0] = jnp.where(sel, a, 0.0)


def _select(aff_t, tok0, n, cap):
    b, ne, _ = aff_t.shape
    blk = tok0 // n
    return pl.pallas_call(
        functools.partial(_select_kernel, cap=cap),
        grid=(b,),
        in_specs=[pl.BlockSpec((1, ne, n), lambda i: (i, 0, blk))],
        out_specs=[pl.BlockSpec((1, ne, n), lambda i: (i, 0, 0)), pl.BlockSpec((1, ne, n), lambda i: (i, 0, 0))],
        out_shape=[jax.ShapeDtypeStruct((b, ne, n), jnp.int32), jax.ShapeDtypeStruct((b, ne, n), F32)],
        compiler_params=_params("parallel"),
        name="moe_select",
    )(aff_t)


def _expert_kernel(h_ref, pos_ref, gate_ref, wg_ref, wu_ref, wd_ref, o_ref, *, cap):
    e = pl.program_id(1)

    @pl.when(e == 0)
    def _():
        o_ref[...] = jnp.zeros_like(o_ref)

    nb, n, _ = h_ref.shape
    slot = lax.broadcasted_iota(jnp.int32, (cap, n), 0)
    onehots, xes, gates = [], [], []
    for i in range(nb):
        onehot = jnp.where(pos_ref[i, 0] == slot, 1.0, 0.0).astype(BF16)
        onehots.append(onehot)
        xes.append(_dg(onehot, h_ref[i]).astype(BF16))
        g8 = jnp.broadcast_to(gate_ref[i, 0], (8, n))
        gates.append(_mm_exact_lhs(onehot, g8, NT)[:, 0:1])
    xe = jnp.concatenate(xes, axis=0)
    hg = _dg(xe, wg_ref[0])
    hu = _dg(xe, wu_ref[0])
    hid = (hg * _sigmoid(hg) * hu).astype(BF16)
    ye = (_dg(hid, wd_ref[0]) * jnp.concatenate(gates, axis=0)).astype(BF16)
    for i in range(nb):
        o_ref[i] += _dg(onehots[i], ye[i * cap:(i + 1) * cap], TN)


def _experts(h, pos, gate, wg, wu, wd, tok0, n, cap):
    b, s, d = h.shape
    ne, _, f = wg.shape
    blk = tok0 // n
    nb = max(1, min(b // 2, EXPERT_ROWS // cap))
    assert b % nb == 0
    sel = pl.BlockSpec((nb, 1, 1, n), lambda i, e: (i, e, 0, 0))
    return pl.pallas_call(
        functools.partial(_expert_kernel, cap=cap),
        grid=(b // nb, ne),
        in_specs=[pl.BlockSpec((nb, n, d), lambda i, e: (i, blk, 0)), sel, sel,
                  pl.BlockSpec((1, d, f), lambda i, e: (e, 0, 0)),
                  pl.BlockSpec((1, d, f), lambda i, e: (e, 0, 0)),
                  pl.BlockSpec((1, f, d), lambda i, e: (e, 0, 0))],
        out_specs=pl.BlockSpec((nb, n, d), lambda i, e: (i, 0, 0)),
        out_shape=jax.ShapeDtypeStruct((b, n, d), F32),
        compiler_params=_params("parallel", "arbitrary"),
        name="moe_experts",
    )(h, pos.reshape(b, ne, 1, n), gate.reshape(b, ne, 1, n), wg, wu, wd)


def _residual_kernel(x_ref, m_ref, mt_ref, o_ref):
    o_ref[0] = x_ref[0] + mt_ref[0, 0, 5:6] * m_ref[0]


def _moe_residual(xc, moe, mt, tok0, stream):
    b, s, d = xc.shape
    n = moe.shape[1]
    t0 = tok0 // ROW_TILE
    return pl.pallas_call(
        _residual_kernel,
        grid=(b, n // ROW_TILE),
        in_specs=[pl.BlockSpec((1, ROW_TILE, d), lambda i, j: (i, t0 + j, 0)),
                  pl.BlockSpec((1, ROW_TILE, d), lambda i, j: (i, j, 0)),
                  pl.BlockSpec((1, 1, 6, d), lambda i, j: (i, stream, 0, 0))],
        out_specs=pl.BlockSpec((1, ROW_TILE, d), lambda i, j: (i, t0 + j, 0)),
        out_shape=jax.ShapeDtypeStruct((b, s, d), F32),
        input_output_aliases={0: 0},
        compiler_params=_params("parallel", "parallel"),
        name="moe_residual",
    )(xc, moe, mt)


def _final_norm_kernel(x_ref, g_ref, o_ref):
    o_ref[0] = _rms(x_ref[0]) * g_ref[...]


def _final_norm(xc, gain, n_lat):
    b, s, d = xc.shape
    return pl.pallas_call(
        _final_norm_kernel,
        grid=(b, n_lat // ROW_TILE),
        in_specs=[pl.BlockSpec((1, ROW_TILE, d), lambda i, j: (i, j, 0)),
                  pl.BlockSpec((1, d), lambda i, j: (0, 0))],
        out_specs=pl.BlockSpec((1, ROW_TILE, d), lambda i, j: (i, j, 0)),
        out_shape=jax.ShapeDtypeStruct((b, n_lat, d), F32),
        compiler_params=_params("parallel", "parallel"),
        name="final_norm",
    )(xc, gain)


def _rope_tables(n_lat, n_ctx):
    rows = n_lat // GRID_W
    row = jnp.repeat(jnp.arange(rows, dtype=F32), GRID_W)
    col = jnp.tile(jnp.arange(GRID_W, dtype=F32), rows)
    axis_dim = HEAD_DIM // 2
    inv_freq = ROPE_THETA ** (-jnp.arange(0, axis_dim, 2, dtype=F32) / axis_dim)
    ang_r = row[:, None] * inv_freq[None, :]
    ang_c = col[:, None] * inv_freq[None, :]
    cos_h = jnp.concatenate([jnp.cos(ang_r), jnp.cos(ang_r), jnp.cos(ang_c), jnp.cos(ang_c)], axis=1)
    sin_h = jnp.concatenate([-jnp.sin(ang_r), jnp.sin(ang_r), -jnp.sin(ang_c), jnp.sin(ang_c)], axis=1)
    cos_h = jnp.concatenate([cos_h, jnp.ones((n_ctx, HEAD_DIM), F32)], axis=0)
    sin_h = jnp.concatenate([sin_h, jnp.zeros((n_ctx, HEAD_DIM), F32)], axis=0)
    return jnp.tile(cos_h, (1, ATT_HEADS)), jnp.tile(sin_h, (1, ATT_HEADS))


def _head_sum_matrix(width):
    idx = np.arange(width) // HEAD_DIM
    return jnp.asarray(idx[:, None] == idx[None, :], dtype=BF16)


def _two_dir_lowrank(w2):
    _, rank, w = w2.shape
    z = jnp.zeros((rank, w), w2.dtype)
    return jnp.concatenate([jnp.concatenate([w2[0], z], axis=1), jnp.concatenate([z, w2[1]], axis=1)], axis=0)


def kernel(x, c, ctx, c_ctx, ada_w, ada_b, norm1, w_in, q_gain, k_gain, shift_mu, decay_w0, decay_w2, iclr_a0, iclr_a2, gate_g2, rwkv_kk, rwkv_ka, rwkv_rk, rwkv_gn_w, rwkv_gn_b, conv_w, w_br_att, w_br_rwkv, w_br_conv, w_out, norm2, w_router, exp_gate, exp_up, exp_down, final_norm):
    b, n_lat, d = x.shape
    n_ctx = ctx.shape[1]
    depth = ada_w.shape[0]
    assert n_lat % ROW_TILE == 0 and n_ctx % ROW_TILE == 0 and n_lat % n_ctx == 0
    n_lat_tiles = n_lat // ROW_TILE
    w = RWKV_WIDTH

    pad = (-(b + 1)) % 8
    cc = jnp.concatenate([c, c_ctx[None, :], jnp.zeros((pad, d), F32)], axis=0)
    mods = _ada_table(cc, ada_w, ada_b)

    cos_t, sin_t = _rope_tables(n_lat, n_ctx)
    seg = _head_sum_matrix(ATT_WIDTH)
    att_w = ATT_WIDTH + 2 * ATT_KV_WIDTH
    cv_w = 3 * CONV_WIDTH
    offs = np.cumsum([0, att_w, RWKV_SEG, cv_w, 3 * d])

    xc = jnp.concatenate([x, ctx], axis=1)
    for l in range(depth):
        mod_lat = mods[l, :b].reshape(b, 1, 6, d)
        mod_ctx = jnp.broadcast_to(mods[l, b].reshape(1, 1, 6, d), (b, 1, 6, d))
        mt = jnp.concatenate([mod_lat, mod_ctx], axis=1)

        w_l = w_in[l].astype(BF16)
        gain1 = norm1[l].reshape(1, d)
        att, rw, cv, gt = _in_proj(xc, gain1, mt, [w_l[:, offs[i]:offs[i + 1]] for i in range(4)],
                                   [BF16, F32, BF16, BF16], n_lat_tiles)

        q, k, v = _att_prep(att, cos_t, sin_t, jnp.tile(q_gain[l], ATT_HEADS).reshape(1, -1),
                            jnp.tile(k_gain[l], ATT_KV_HEADS).reshape(1, -1), seg)
        o_att = _attention(q, k, v, n_lat)

        r, kx, vx, g, nkk, lw, kd, bb = _rwkv_prep(
            rw, shift_mu[l].reshape(1, -1), _two_dir_lowrank(decay_w2[l]), decay_w0[l].reshape(1, -1),
            _two_dir_lowrank(iclr_a2[l]), iclr_a0[l].reshape(1, -1), gate_g2[l],
            rwkv_kk[l].reshape(1, -1), rwkv_ka[l].reshape(1, -1), seg, n_lat_tiles)
        y_f, y_b = _rwkv_scan(r, vx, nkk, lw, kd, bb, n_lat)
        o_rw = _rwkv_readout(y_f, y_b, r, kx, vx, g, rwkv_rk[l].reshape(1, -1), rwkv_gn_w[l].reshape(1, -1),
                             rwkv_gn_b[l].reshape(1, -1), seg)

        xc = _merge(xc, o_att, o_rw, cv, gt, mt, conv_w[l], w_br_att[l].astype(BF16), w_br_rwkv[l].astype(BF16),
                    w_br_conv[l].astype(BF16), w_out[l].astype(BF16), n_lat_tiles)

        h2, aff_t = _router(xc, norm2[l].reshape(1, d), mt, w_router[l].T, n_lat_tiles)
        wg, wu, wd = exp_gate[l].astype(BF16), exp_up[l].astype(BF16), exp_down[l].astype(BF16)
        streams = [(0, n_lat, 0)] + ([(n_lat, n_ctx, 1)] if l < depth - 1 else [])
        for tok0, n, stream in streams:
            cap = CAPACITY_FACTOR * n // N_EXPERTS
            pos, gate = _select(aff_t, tok0, n, cap)
            moe = _experts(h2, pos, gate, wg, wu, wd, tok0, n, cap)
            xc = _moe_residual(xc, moe, mt, tok0, stream)
    return _final_norm(xc, final_norm.reshape(1, d), n_lat)
```

```python
import functools

import numpy as np
import jax
import jax.numpy as jnp
from jax import lax
from jax.experimental import pallas as pl
from jax.experimental.pallas import tpu as pltpu

F32 = jnp.float32
BF16 = jnp.bfloat16

GRID_W = 64
NORM_EPS = 1e-6
LOG2_E = float(np.log2(np.e))
ATT_HEADS = 8
ATT_KV_HEADS = 2
HEAD_DIM = 64
ATT_GROUP = ATT_HEADS // ATT_KV_HEADS
ATT_WIDTH = ATT_HEADS * HEAD_DIM
ATT_KV_WIDTH = ATT_KV_HEADS * HEAD_DIM
ROPE_THETA = 10000.0
RWKV_HEADS = 8
RWKV_WIDTH = RWKV_HEADS * HEAD_DIM
DECAY_RANK = 64
ICLR_RANK = 64
GATE_RANK = 128
RWKV_GN_EPS = 64e-5
RWKV_SEG = 3 * RWKV_WIDTH + 2 * DECAY_RANK + 2 * ICLR_RANK + GATE_RANK
CONV_WIDTH = 512
N_EXPERTS = 16
CAPACITY_FACTOR = 2

ROW_TILE = 256
CHUNK = 64
QUAD = 4 * HEAD_DIM
SCAN_BATCH = 4
HALO = 8
SELECT_BLOCK = 256
EXPERT_ROWS = 256
VMEM_LIMIT = 56 * 1024 * 1024

NT = (((1,), (1,)), ((), ()))
TN = (((0,), (0,)), ((), ()))
NN = (((1,), (0,)), ((), ()))


def _params(*sem):
    return pltpu.CompilerParams(dimension_semantics=sem, vmem_limit_bytes=VMEM_LIMIT)


def _dg(a, b, dn=NN):
    return lax.dot_general(a, b, dn, preferred_element_type=F32)


def _split2(x):
    hi = x.astype(BF16)
    lo = (x - hi.astype(F32)).astype(BF16)
    return hi, lo


def _split3(x):
    hi = x.astype(BF16)
    r = x - hi.astype(F32)
    mid = r.astype(BF16)
    lo = (r - mid.astype(F32)).astype(BF16)
    return hi, mid, lo


def _mm1(a, b, dn=NN):
    return _dg(a.astype(BF16), b.astype(BF16), dn)


def _mm3(a, b, dn=NN):
    ah, al = _split2(a)
    bh, bl = _split2(b)
    return _dg(ah, bh, dn) + (_dg(ah, bl, dn) + _dg(al, bh, dn))


def _mm_exact_lhs(a_bf16, b, dn=NN):
    h, m, l = _split3(b)
    return _dg(a_bf16, h, dn) + (_dg(a_bf16, m, dn) + _dg(a_bf16, l, dn))


def _head_sum(x, seg):
    hi, lo = _split2(x)
    return _dg(hi, seg) + _dg(lo, seg)


def _sigmoid(x):
    return 1.0 / (1.0 + jnp.exp(-x))


def _rms(x):
    return x * lax.rsqrt(jnp.mean(x * x, axis=-1, keepdims=True) + NORM_EPS)


def _ada_kernel(c_ref, w_ref, b_ref, o_ref):
    c = c_ref[...]
    s = c * _sigmoid(c)
    o_ref[0] = _mm3(s, w_ref[0]) + b_ref[0]


def _ada_table(cc, ada_w, ada_b):
    n_layers, d, six_d = ada_w.shape
    rows = cc.shape[0]
    tn = six_d // 4
    return pl.pallas_call(
        _ada_kernel,
        grid=(n_layers, six_d // tn),
        in_specs=[pl.BlockSpec((rows, d), lambda l, j: (0, 0)),
                  pl.BlockSpec((1, d, tn), lambda l, j: (l, 0, j)),
                  pl.BlockSpec((1, 1, tn), lambda l, j: (l, 0, j))],
        out_specs=pl.BlockSpec((1, rows, tn), lambda l, j: (l, 0, j)),
        out_shape=jax.ShapeDtypeStruct((n_layers, rows, six_d), F32),
        compiler_params=_params("parallel", "parallel"),
        name="ada_table",
    )(cc, ada_w, ada_b.reshape(n_layers, 1, six_d))


def _swap16(x):
    w = x.shape[-1]
    lane = lax.broadcasted_iota(jnp.int32, x.shape, x.ndim - 1)
    return jnp.where((lane & 16) == 0, pltpu.roll(x, w - 16, x.ndim - 1), pltpu.roll(x, 16, x.ndim - 1))


def _att_heads(a, cos, sin, q_gain, k_gain, seg):
    def norm_rope(u, gain, cos_u, sin_u, seg_u):
        ms = _head_sum(u * u, seg_u) * (1.0 / HEAD_DIM)
        un = u * lax.rsqrt(ms + NORM_EPS) * gain
        return un * cos_u + _swap16(un) * sin_u

    kw = ATT_KV_WIDTH
    q = norm_rope(a[:, :ATT_WIDTH], q_gain, cos, sin, seg)
    k = norm_rope(a[:, ATT_WIDTH:ATT_WIDTH + kw], k_gain, cos[:, :kw], sin[:, :kw], seg[:kw, :kw])
    q = (q * (HEAD_DIM ** -0.5 * LOG2_E)).astype(BF16)
    ones = jnp.ones((a.shape[0], HEAD_DIM), BF16)
    v = a[:, ATT_WIDTH + kw:].astype(BF16)
    v1 = jnp.concatenate([t for h in range(ATT_KV_HEADS) for t in (v[:, h * HEAD_DIM:(h + 1) * HEAD_DIM], ones)],
                         axis=1)
    return q, k.astype(BF16), v1


def _in_proj_kernel(x_ref, gain_ref, mt_ref, cos_ref, sin_ref, qg_ref, kg_ref, seg_ref, wa_ref, wr_ref, wc_ref, wg_ref,
                    q_ref, k_ref, v_ref, rw_ref, cv_ref, gt_ref):
    h = _rms(x_ref[0]) * gain_ref[...]
    h = (h * (1.0 + mt_ref[0, 0, 1:2]) + mt_ref[0, 0, 0:1]).astype(BF16)
    q, k, v1 = _att_heads(_dg(h, wa_ref[...]), cos_ref[...], sin_ref[...], qg_ref[...], kg_ref[...], seg_ref[...])
    q_ref[0] = q
    k_ref[0] = k
    v_ref[0] = v1
    rw_ref[0] = _dg(h, wr_ref[...])
    cv_ref[0] = _dg(h, wc_ref[...]).astype(BF16)
    gt_ref[0] = _dg(h, wg_ref[...]).astype(BF16)


def _in_proj(xc, gain, mt, cos_t, sin_t, q_gain_t, k_gain_t, seg, weights, n_lat_tiles):
    b, s, d = xc.shape
    tok = lambda n: pl.BlockSpec((1, ROW_TILE, n), lambda i, j: (i, j, 0))
    full = lambda a: pl.BlockSpec(a.shape, lambda i, j: (0,) * a.ndim)
    table = pl.BlockSpec((ROW_TILE, ATT_WIDTH), lambda i, j: (j, 0))
    widths = [ATT_WIDTH, ATT_KV_WIDTH, 2 * ATT_KV_WIDTH] + [w.shape[1] for w in weights[1:]]
    dtypes = [BF16, BF16, BF16, F32, BF16, BF16]
    return pl.pallas_call(
        _in_proj_kernel,
        grid=(b, s // ROW_TILE),
        in_specs=[tok(d), full(gain), pl.BlockSpec((1, 1, 6, d), lambda i, j: (i, j // n_lat_tiles, 0, 0)),
                  table, table, full(q_gain_t), full(k_gain_t), full(seg)] + [full(w) for w in weights],
        out_specs=[tok(n) for n in widths],
        out_shape=[jax.ShapeDtypeStruct((b, s, n), dt) for n, dt in zip(widths, dtypes)],
        compiler_params=_params("parallel", "parallel"),
        name="in_proj",
    )(xc, gain, mt, cos_t, sin_t, q_gain_t, k_gain_t, seg, *weights)


def _attn_kernel(q_ref, k_ref, v_ref, o_ref, *, n_lat, n_lat_tiles):
    j = pl.program_id(1)
    s_all = k_ref.shape[1]

    def run(k0, k1):
        def scores(h):
            kvh = h // ATT_GROUP
            k = k_ref[0, k0:k1, kvh * HEAD_DIM:(kvh + 1) * HEAD_DIM]
            return _dg(q_ref[0, :, h * HEAD_DIM:(h + 1) * HEAD_DIM], k, NT)

        s_next = scores(0)
        for h in range(ATT_HEADS):
            s = s_next
            if h + 1 < ATT_HEADS:
                s_next = scores(h + 1)
            kvh = h // ATT_GROUP
            v1 = v_ref[0, k0:k1, 2 * kvh * HEAD_DIM:2 * (kvh + 1) * HEAD_DIM]
            p = jnp.exp2((s - jnp.max(s, axis=-1, keepdims=True)).astype(BF16))
            ov = _dg(p, v1)
            o = ov[:, :HEAD_DIM] / ov[:, HEAD_DIM:]
            o_ref[0, :, h * HEAD_DIM:(h + 1) * HEAD_DIM] = o.astype(BF16)

    @pl.when(j < n_lat_tiles)
    def _():
        run(0, s_all)

    @pl.when(j >= n_lat_tiles)
    def _():
        run(n_lat, s_all)


def _attention(q, k, v, n_lat):
    b, s, _ = q.shape
    return pl.pallas_call(
        functools.partial(_attn_kernel, n_lat=n_lat, n_lat_tiles=n_lat // ROW_TILE),
        grid=(b, s // ROW_TILE),
        in_specs=[pl.BlockSpec((1, ROW_TILE, ATT_WIDTH), lambda i, j: (i, j, 0)),
                  pl.BlockSpec((1, s, k.shape[2]), lambda i, j: (i, 0, 0)),
                  pl.BlockSpec((1, s, v.shape[2]), lambda i, j: (i, 0, 0))],
        out_specs=pl.BlockSpec((1, ROW_TILE, ATT_WIDTH), lambda i, j: (i, j, 0)),
        out_shape=jax.ShapeDtypeStruct((b, s, ATT_WIDTH), BF16),
        compiler_params=_params("parallel", "parallel"),
        name="attention",
    )(q, k, v)


def _neighbours(x, halo_prev, halo_next, first, last):
    rows = x.shape[0]
    ridx = lax.broadcasted_iota(jnp.int32, x.shape, 0)
    row_p = jnp.where(first, 0.0, halo_prev[halo_prev.shape[0] - 1:])
    row_n = jnp.where(last, 0.0, halo_next[0:1])
    prev = jnp.where(ridx == 0, row_p, pltpu.roll(x, 1, 0))
    nxt = jnp.where(ridx == rows - 1, row_n, pltpu.roll(x, rows - 1, 0))
    return prev, nxt


def _tile_ends(j, n_lat_tiles, n_tiles):
    first = jnp.logical_or(j == 0, j == n_lat_tiles)
    last = jnp.logical_or(j == n_lat_tiles - 1, j == n_tiles - 1)
    return first, last


def _halo_specs(width, n_tiles, dtype):
    halo = HALO * (4 // jnp.dtype(dtype).itemsize)
    per = ROW_TILE // halo
    prev = pl.BlockSpec((1, halo, width), lambda i, j: (i, jnp.maximum(j * per - 1, 0), 0))
    nxt = pl.BlockSpec((1, halo, width), lambda i, j: (i, jnp.minimum((j + 1) * per, n_tiles * per - 1), 0))
    return prev, nxt


def _rwkv_prep_kernel(x_ref, hp_ref, hn_ref, mu_ref, w2_ref, w0_ref, a2_ref, a0_ref, g2_ref, kk_ref, ka_ref,
                      seg_ref, r_ref, k_ref, v_ref, g_ref, nkk_ref, lw_ref, kd_ref, bb_ref, *, n_lat_tiles, n_tiles):
    j = pl.program_id(1)
    first, last = _tile_ends(j, n_lat_tiles, n_tiles)
    x = x_ref[0]
    prev, nxt = _neighbours(x, hp_ref[0], hn_ref[0], first, last)
    xs = x + mu_ref[...] * (0.5 * (prev + nxt) - x)
    w = RWKV_WIDTH
    r = xs[:, 0:w]
    k = xs[:, w:2 * w]
    v = xs[:, 2 * w:3 * w]
    o = 3 * w
    w_lo = xs[:, o:o + 2 * DECAY_RANK]
    a_lo = xs[:, o + 2 * DECAY_RANK:o + 2 * DECAY_RANK + 2 * ICLR_RANK]
    g_lo = xs[:, o + 2 * DECAY_RANK + 2 * ICLR_RANK:]
    w_pre = w0_ref[...] + _mm1(jnp.tanh(w_lo), w2_ref[...])
    logw = -_sigmoid(w_pre) * float(np.exp(-0.5))
    a = _sigmoid(a0_ref[...] + _mm1(a_lo, a2_ref[...]))
    g = _mm1(_sigmoid(g_lo), g2_ref[...])
    kk = k * kk_ref[...]
    kk = kk * lax.rsqrt(_head_sum(kk * kk, seg_ref[...]) + 1e-12)
    r_ref[0] = r.astype(BF16)
    k_ref[0] = k.astype(BF16)
    v_ref[0] = v.astype(BF16)
    g_ref[0] = g.astype(BF16)
    nkk_ref[0] = (-kk).astype(BF16)
    for d in range(2):
        a_d = a[:, d * w:(d + 1) * w]
        lw_ref[0, d] = logw[:, d * w:(d + 1) * w]
        kd_ref[0, d] = (k * (1.0 + (a_d - 1.0) * ka_ref[...])).astype(BF16)
        bb_ref[0, d] = (kk * a_d).astype(BF16)


def _rwkv_prep(rw, mu, w2cat, w0, a2cat, a0, g2, k_k, k_a, seg, n_lat_tiles):
    b, s, wseg = rw.shape
    n_tiles = s // ROW_TILE
    w = RWKV_WIDTH
    hp, hn = _halo_specs(wseg, n_tiles, rw.dtype)
    full = lambda shape: pl.BlockSpec(shape, lambda i, j: (0,) * len(shape))
    tok = pl.BlockSpec((1, ROW_TILE, w), lambda i, j: (i, j, 0))
    tok2 = pl.BlockSpec((1, 2, ROW_TILE, w), lambda i, j: (i, 0, j, 0))
    one = jax.ShapeDtypeStruct((b, s, w), BF16)
    two = jax.ShapeDtypeStruct((b, 2, s, w), BF16)
    two_f32 = jax.ShapeDtypeStruct((b, 2, s, w), F32)
    return pl.pallas_call(
        functools.partial(_rwkv_prep_kernel, n_lat_tiles=n_lat_tiles, n_tiles=n_tiles),
        grid=(b, n_tiles),
        in_specs=[pl.BlockSpec((1, ROW_TILE, wseg), lambda i, j: (i, j, 0)), hp, hn,
                  full((1, wseg)), full(w2cat.shape), full((1, 2 * w)), full(a2cat.shape), full((1, 2 * w)),
                  full(g2.shape), full((1, w)), full((1, w)), full(seg.shape)],
        out_specs=[tok, tok, tok, tok, tok, tok2, tok2, tok2],
        out_shape=[one, one, one, one, one, two_f32, two, two],
        compiler_params=_params("parallel", "parallel"),
        name="rwkv_prep",
    )(rw, rw, rw, mu, w2cat, w0, a2cat, a0, g2, k_k, k_a, seg)


def _block_diag(y, head_masks):
    return jnp.concatenate([jnp.where(m, y, 0.0) for m in head_masks], axis=0)


def _scan_chain(sgn, r, v, nkk, lw, kd, bb, s0):
    row = lax.broadcasted_iota(jnp.int32, (CHUNK, QUAD), 0)
    lane = lax.broadcasted_iota(jnp.int32, (CHUNK, QUAD), 1)
    rel = ((lane % CHUNK) - row) * sgn
    strict = rel < 0
    incl = rel <= 0
    eye = jnp.where(rel == 0, 1.0, 0.0)
    head_masks = [(lane // HEAD_DIM) == h for h in range(4)]
    bd_mask = (lax.broadcasted_iota(jnp.int32, (QUAD, QUAD), 0) // HEAD_DIM
               == lax.broadcasted_iota(jnp.int32, (QUAD, QUAD), 1) // HEAD_DIM)
    t_row = lax.broadcasted_iota(jnp.int32, (CHUNK, CHUNK), 0)
    t_col = lax.broadcasted_iota(jnp.int32, (CHUNK, CHUNK), 1)
    tri = jnp.where((t_col - t_row) * sgn <= 0, 1.0, 0.0).astype(BF16)
    last_row = CHUNK - 1 if sgn > 0 else 0
    bd = lambda t: _block_diag(t, head_masks).astype(BF16)

    cum = _mm_exact_lhs(tri, lw)
    yield
    cum_end = cum[last_row:last_row + 1]
    w_inv = jnp.exp(-cum)
    w_rem = jnp.exp(cum_end - cum)
    ar = jnp.concatenate([nkk * jnp.exp(cum - lw), r * jnp.exp(cum)], axis=0).astype(BF16)
    xb = _dg(ar, bd(bb * w_inv), NT)
    xk = _dg(ar, bd(kd * w_inv), NT)
    ars = _dg(ar, s0.astype(BF16), NT)
    yield
    n_ab = jnp.where(strict, xb[:CHUNK], 0.0)
    l_ak = jnp.where(strict, xk[:CHUNK], 0.0)
    g_rb = jnp.where(incl, xb[CHUNK:], 0.0)
    g_rk = jnp.where(incl, xk[CHUNK:], 0.0)

    m = eye + n_ab
    p = _dg(n_ab.astype(BF16), bd(n_ab))
    lg = _dg(jnp.concatenate([l_ak, g_rk], axis=0).astype(BF16), bd(v))
    yield
    for _ in range(int(np.log2(CHUNK)) - 2):
        both = _dg(jnp.concatenate([p, m], axis=0).astype(BF16), bd(p))
        yield
        p = both[:CHUNK]
        m = m + both[CHUNK:]
    m = m + _dg(m.astype(BF16), bd(p))
    yield
    x0 = ars[:CHUNK] + lg[:CHUNK]
    u = _dg(m.astype(BF16), bd(x0))
    yield
    y = ars[CHUNK:] + _dg(g_rb.astype(BF16), bd(u)) + lg[CHUNK:]
    z = _dg(jnp.concatenate([u, v], axis=0).astype(BF16),
            jnp.concatenate([bb * w_rem, kd * w_rem], axis=0).astype(BF16), TN)
    yield
    return y, s0 * jnp.exp(cum_end) + jnp.where(bd_mask, z, 0.0)


def _run_interleaved(chains):
    results = [None] * len(chains)
    live = list(enumerate(chains))
    while live:
        still = []
        for idx, g in live:
            try:
                next(g)
                still.append((idx, g))
            except StopIteration as stop:
                results[idx] = stop.value
        live = still
    return results


def _rwkv_scan_kernel(rf_ref, vf_ref, nf_ref, lwf_ref, kdf_ref, bbf_ref, rb_ref, vb_ref, nb_ref, lwb_ref, kdb_ref,
                      bbb_ref, yf_ref, yb_ref, s_ref):
    @pl.when(pl.program_id(1) == 0)
    def _():
        s_ref[...] = jnp.zeros_like(s_ref)

    dirs = ((1, rf_ref, vf_ref, nf_ref, lwf_ref, kdf_ref, bbf_ref, yf_ref),
            (-1, rb_ref, vb_ref, nb_ref, lwb_ref, kdb_ref, bbb_ref, yb_ref))
    work = []
    for i in range(SCAN_BATCH):
        for d, (sgn, r_ref, v_ref, n_ref, lw_ref, kd_ref, bb_ref, y_ref) in enumerate(dirs):
            for q in range(RWKV_WIDTH // QUAD):
                ql = slice(q * QUAD, (q + 1) * QUAD)
                f32 = lambda t: t.astype(F32)
                args = (f32(r_ref[i, :, ql]), f32(v_ref[i, :, ql]), f32(n_ref[i, :, ql]), lw_ref[i, 0, :, ql],
                        f32(kd_ref[i, 0, :, ql]), f32(bb_ref[i, 0, :, ql]), s_ref[i, d, q])
                work.append((sgn, args, y_ref, (i, slice(None), ql), (i, d, q)))
    results = _run_interleaved([_scan_chain(sgn, *args) for sgn, args, _, _, _ in work])
    for (_, _, y_ref, y_idx, s_idx), (y, s_new) in zip(work, results):
        y_ref[y_idx] = y
        s_ref[s_idx] = s_new


def _rwkv_scan(r, v, nkk, lw, kd, bb, n_lat):
    b, s, w = r.shape
    nc = s // CHUNK
    nc_lat = n_lat // CHUNK
    nc_ctx = nc - nc_lat
    nb = SCAN_BATCH
    assert b % nb == 0

    fwd = lambda c: jnp.where(c < nc_ctx, nc_lat + c, c - nc_ctx)
    bwd = lambda c: jnp.where(c < nc_ctx, nc - 1 - c, nc_lat - 1 - (c - nc_ctx))
    tok = lambda order: pl.BlockSpec((nb, CHUNK, w), lambda i, c: (i, order(c), 0))
    tok2 = lambda order, d: pl.BlockSpec((nb, 1, CHUNK, w), lambda i, c: (i, d, order(c), 0))
    out = jax.ShapeDtypeStruct((b, s, w), F32)
    return pl.pallas_call(
        _rwkv_scan_kernel,
        grid=(b // nb, nc),
        in_specs=[tok(fwd), tok(fwd), tok(fwd), tok2(fwd, 0), tok2(fwd, 0), tok2(fwd, 0),
                  tok(bwd), tok(bwd), tok(bwd), tok2(bwd, 1), tok2(bwd, 1), tok2(bwd, 1)],
        out_specs=[tok(fwd), tok(bwd)],
        out_shape=[out, out],
        scratch_shapes=[pltpu.VMEM((nb, 2, w // QUAD, QUAD, QUAD), F32)],
        compiler_params=_params("parallel", "arbitrary"),
        name="rwkv_scan",
    )(r, v, nkk, lw, kd, bb, r, v, nkk, lw, kd, bb)


def _rwkv_readout(y, r, k, v, g, r_k, gn_w, gn_b, seg):
    mean = _head_sum(y, seg) * (1.0 / HEAD_DIM)
    yc = y - mean
    var = _head_sum(yc * yc, seg) * (1.0 / HEAD_DIM)
    yn = yc * lax.rsqrt(var + RWKV_GN_EPS)
    bonus = _head_sum(r * k * r_k, seg) * v
    return (yn * gn_w + gn_b + bonus) * g


def _route(x, gain, shift, scale, w_router_t):
    h = _rms(x) * gain
    h = h * (1.0 + scale) + shift
    logits = _mm3(w_router_t, h, NT)
    e = jnp.exp(logits - jnp.max(logits, axis=0, keepdims=True))
    return h, e / jnp.sum(e, axis=0, keepdims=True)


def _merge_kernel(x_ref, oa_ref, yf_ref, yb_ref, r_ref, k_ref, v_ref, g_ref, cv_ref, hp_ref, hn_ref, gt_ref, mt_ref,
                  rk_ref, gw_ref, gb_ref, seg_ref, cw_ref, wa_ref, wr_ref, wc_ref, wo_ref, gain2_ref, wrt_ref,
                  o_ref, h_ref, aff_ref, *, n_lat_tiles, n_tiles):
    j = pl.program_id(1)
    first, last = _tile_ends(j, n_lat_tiles, n_tiles)
    cw = CONV_WIDTH
    d = x_ref.shape[2]
    f32 = lambda ref: ref[0].astype(F32)
    o_rw = _rwkv_readout(yf_ref[0] + yb_ref[0], f32(r_ref), f32(k_ref), f32(v_ref), f32(g_ref),
                         rk_ref[...], gw_ref[...], gb_ref[...], seg_ref[...])
    cv = cv_ref[0].astype(F32)
    hp = hp_ref[0].astype(F32)
    hn = hn_ref[0].astype(F32)
    z = cv[:, cw:2 * cw] * cv[:, 2 * cw:]
    zp = hp[:, cw:2 * cw] * hp[:, 2 * cw:]
    zn = hn[:, cw:2 * cw] * hn[:, 2 * cw:]
    z_prev, z_next = _neighbours(z, zp, zn, first, last)
    o_cv = cv[:, :cw] * (cw_ref[0:1] * z_prev + cw_ref[1:2] * z + cw_ref[2:3] * z_next)
    gt = gt_ref[0].astype(F32)
    m = (_sigmoid(gt[:, :d]) * _dg(oa_ref[0], wa_ref[...])
         + _sigmoid(gt[:, d:2 * d]) * _dg(o_rw.astype(BF16), wr_ref[...])
         + _sigmoid(gt[:, 2 * d:]) * _dg(o_cv.astype(BF16), wc_ref[...]))
    y = _dg(m.astype(BF16), wo_ref[...])
    x_new = x_ref[0] + mt_ref[0, 0, 2:3] * y
    o_ref[0] = x_new
    h, aff = _route(x_new, gain2_ref[...], mt_ref[0, 0, 3:4], mt_ref[0, 0, 4:5], wrt_ref[...])
    h_ref[0] = h.astype(BF16)
    aff_ref[0] = aff


def _merge(xc, o_att, y_f, y_b, r, k, v, g, cv, gt, mt, r_k, gn_w, gn_b, seg, conv_w, wa, wr, wc, wo, gain2,
           w_router_t, n_lat_tiles):
    b, s, d = xc.shape
    n_tiles = s // ROW_TILE
    ne = w_router_t.shape[0]
    hp, hn = _halo_specs(cv.shape[2], n_tiles, cv.dtype)
    tok = lambda a: pl.BlockSpec((1, ROW_TILE, a.shape[2]), lambda i, j: (i, j, 0))
    full = lambda a: pl.BlockSpec(a.shape, lambda i, j: (0,) * a.ndim)
    consts = (r_k, gn_w, gn_b, seg, conv_w, wa, wr, wc, wo, gain2, w_router_t)
    return pl.pallas_call(
        functools.partial(_merge_kernel, n_lat_tiles=n_lat_tiles, n_tiles=n_tiles),
        grid=(b, n_tiles),
        in_specs=[tok(a) for a in (xc, o_att, y_f, y_b, r, k, v, g, cv)] + [hp, hn, tok(gt),
                  pl.BlockSpec((1, 1, 6, d), lambda i, j: (i, j // n_lat_tiles, 0, 0))]
                 + [full(a) for a in consts],
        out_specs=[tok(xc), tok(xc), pl.BlockSpec((1, ne, ROW_TILE), lambda i, j: (i, 0, j))],
        out_shape=[jax.ShapeDtypeStruct((b, s, d), F32), jax.ShapeDtypeStruct((b, s, d), BF16),
                   jax.ShapeDtypeStruct((b, ne, s), F32)],
        compiler_params=_params("parallel", "parallel"),
        name="merge",
    )(xc, o_att, y_f, y_b, r, k, v, g, cv, cv, cv, gt, mt, *consts)


def _select_kernel(aff_ref, pos_ref, gate_ref, *, cap):
    a = aff_ref[0]
    ne, n = a.shape
    bits = pltpu.bitcast(a, jnp.int32)

    def count(mask):
        return jnp.sum(jnp.where(mask, 1.0, 0.0), axis=1, keepdims=True)

    def body(_, carry):
        lo, hi = carry
        mid = lo + ((hi - lo + 1) >> 1)
        ok = count(bits >= mid) >= cap
        return jnp.where(ok, mid, lo), jnp.where(ok, hi, mid - 1)

    lo0 = jnp.zeros((ne, 1), jnp.int32)
    hi0 = jnp.full((ne, 1), 0x7F800000, jnp.int32)
    thr, _ = lax.fori_loop(0, 32, body, (lo0, hi0))
    gt = bits > thr
    eq = bits == thr
    need = cap - count(gt)

    def tokens_before(mask):
        m = jnp.where(mask, 1.0, 0.0).astype(BF16)
        blk = min(n, SELECT_BLOCK)
        cols = []
        for j in range(n // blk):
            s_idx = lax.broadcasted_iota(jnp.int32, (n, blk), 0)
            t_idx = lax.broadcasted_iota(jnp.int32, (n, blk), 1) + j * blk
            cols.append(_dg(m, jnp.where(s_idx < t_idx, 1.0, 0.0).astype(BF16)))
        return jnp.concatenate(cols, axis=1)

    sel = jnp.logical_or(gt, jnp.logical_and(eq, tokens_before(eq) < need))
    pos_ref[0] = jnp.where(sel, tokens_before(sel).astype(jnp.int32), -1)
    gate_ref[0] = jnp.where(sel, a, 0.0)


def _select(aff_t, tok0, n, cap):
    b, ne, _ = aff_t.shape
    blk = tok0 // n
    return pl.pallas_call(
        functools.partial(_select_kernel, cap=cap),
        grid=(b,),
        in_specs=[pl.BlockSpec((1, ne, n), lambda i: (i, 0, blk))],
        out_specs=[pl.BlockSpec((1, ne, n), lambda i: (i, 0, 0)), pl.BlockSpec((1, ne, n), lambda i: (i, 0, 0))],
        out_shape=[jax.ShapeDtypeStruct((b, ne, n), jnp.int32), jax.ShapeDtypeStruct((b, ne, n), F32)],
        compiler_params=_params("parallel"),
        name="moe_select",
    )(aff_t)


def _expert_kernel(h_ref, pos_ref, gate_ref, wg_ref, wu_ref, wd_ref, o_ref, *, cap):
    e = pl.program_id(1)

    @pl.when(e == 0)
    def _():
        o_ref[...] = jnp.zeros_like(o_ref)

    nb, n, _ = h_ref.shape
    slot = lax.broadcasted_iota(jnp.int32, (cap, n), 0)
    onehots, xes, gates = [], [], []
    for i in range(nb):
        hit = pos_ref[i, 0] == slot
        onehot = jnp.where(hit, 1.0, 0.0).astype(BF16)
        onehots.append(onehot)
        xes.append(_dg(onehot, h_ref[i]).astype(BF16))
        gates.append(jnp.sum(jnp.where(hit, gate_ref[i, 0], 0.0), axis=1, keepdims=True))
    xe = jnp.concatenate(xes, axis=0)
    hg = _dg(xe, wg_ref[0])
    hu = _dg(xe, wu_ref[0])
    hid = (hg * _sigmoid(hg) * hu).astype(BF16)
    ye = (_dg(hid, wd_ref[0]) * jnp.concatenate(gates, axis=0)).astype(BF16)
    for i in range(nb):
        o_ref[i] += _dg(onehots[i], ye[i * cap:(i + 1) * cap], TN)


def _experts(h, pos, gate, wg, wu, wd, tok0, n, cap):
    b, s, d = h.shape
    ne, _, f = wg.shape
    blk = tok0 // n
    nb = max(1, min(b // 2, EXPERT_ROWS // cap))
    assert b % nb == 0
    sel = pl.BlockSpec((nb, 1, 1, n), lambda i, e: (i, e, 0, 0))
    return pl.pallas_call(
        functools.partial(_expert_kernel, cap=cap),
        grid=(b // nb, ne),
        in_specs=[pl.BlockSpec((nb, n, d), lambda i, e: (i, blk, 0)), sel, sel,
                  pl.BlockSpec((1, d, f), lambda i, e: (e, 0, 0)),
                  pl.BlockSpec((1, d, f), lambda i, e: (e, 0, 0)),
                  pl.BlockSpec((1, f, d), lambda i, e: (e, 0, 0))],
        out_specs=pl.BlockSpec((nb, n, d), lambda i, e: (i, 0, 0)),
        out_shape=jax.ShapeDtypeStruct((b, n, d), F32),
        compiler_params=_params("parallel", "arbitrary"),
        name="moe_experts",
    )(h, pos.reshape(b, ne, 1, n), gate.reshape(b, ne, 1, n), wg, wu, wd)


def _residual_kernel(x_ref, m_ref, mt_ref, o_ref):
    o_ref[0] = x_ref[0] + mt_ref[0, 0, 5:6] * m_ref[0]


def _moe_residual(xc, moe, mt, tok0, stream):
    b, s, d = xc.shape
    n = moe.shape[1]
    t0 = tok0 // ROW_TILE
    return pl.pallas_call(
        _residual_kernel,
        grid=(b, n // ROW_TILE),
        in_specs=[pl.BlockSpec((1, ROW_TILE, d), lambda i, j: (i, t0 + j, 0)),
                  pl.BlockSpec((1, ROW_TILE, d), lambda i, j: (i, j, 0)),
                  pl.BlockSpec((1, 1, 6, d), lambda i, j: (i, stream, 0, 0))],
        out_specs=pl.BlockSpec((1, ROW_TILE, d), lambda i, j: (i, t0 + j, 0)),
        out_shape=jax.ShapeDtypeStruct((b, s, d), F32),
        input_output_aliases={0: 0},
        compiler_params=_params("parallel", "parallel"),
        name="moe_residual",
    )(xc, moe, mt)


def _final_norm_kernel(x_ref, g_ref, o_ref):
    o_ref[0] = _rms(x_ref[0]) * g_ref[...]


def _final_norm(xc, gain, n_lat):
    b, s, d = xc.shape
    return pl.pallas_call(
        _final_norm_kernel,
        grid=(b, n_lat // ROW_TILE),
        in_specs=[pl.BlockSpec((1, ROW_TILE, d), lambda i, j: (i, j, 0)),
                  pl.BlockSpec((1, d), lambda i, j: (0, 0))],
        out_specs=pl.BlockSpec((1, ROW_TILE, d), lambda i, j: (i, j, 0)),
        out_shape=jax.ShapeDtypeStruct((b, n_lat, d), F32),
        compiler_params=_params("parallel", "parallel"),
        name="final_norm",
    )(xc, gain)


def _rope_tables(n_lat, n_ctx):
    rows = n_lat // GRID_W
    row = jnp.repeat(jnp.arange(rows, dtype=F32), GRID_W)
    col = jnp.tile(jnp.arange(GRID_W, dtype=F32), rows)
    axis_dim = HEAD_DIM // 2
    inv_freq = ROPE_THETA ** (-jnp.arange(0, axis_dim, 2, dtype=F32) / axis_dim)
    ang_r = row[:, None] * inv_freq[None, :]
    ang_c = col[:, None] * inv_freq[None, :]
    cos_h = jnp.concatenate([jnp.cos(ang_r), jnp.cos(ang_r), jnp.cos(ang_c), jnp.cos(ang_c)], axis=1)
    sin_h = jnp.concatenate([-jnp.sin(ang_r), jnp.sin(ang_r), -jnp.sin(ang_c), jnp.sin(ang_c)], axis=1)
    cos_h = jnp.concatenate([cos_h, jnp.ones((n_ctx, HEAD_DIM), F32)], axis=0)
    sin_h = jnp.concatenate([sin_h, jnp.zeros((n_ctx, HEAD_DIM), F32)], axis=0)
    return jnp.tile(cos_h, (1, ATT_HEADS)), jnp.tile(sin_h, (1, ATT_HEADS))


def _head_sum_matrix(width):
    idx = np.arange(width) // HEAD_DIM
    return jnp.asarray(idx[:, None] == idx[None, :], dtype=BF16)


def _two_dir_lowrank(w2):
    _, rank, w = w2.shape
    z = jnp.zeros((rank, w), w2.dtype)
    return jnp.concatenate([jnp.concatenate([w2[0], z], axis=1), jnp.concatenate([z, w2[1]], axis=1)], axis=0)


def kernel(x, c, ctx, c_ctx, ada_w, ada_b, norm1, w_in, q_gain, k_gain, shift_mu, decay_w0, decay_w2, iclr_a0, iclr_a2, gate_g2, rwkv_kk, rwkv_ka, rwkv_rk, rwkv_gn_w, rwkv_gn_b, conv_w, w_br_att, w_br_rwkv, w_br_conv, w_out, norm2, w_router, exp_gate, exp_up, exp_down, final_norm):
    b, n_lat, d = x.shape
    n_ctx = ctx.shape[1]
    depth = ada_w.shape[0]
    assert n_lat % ROW_TILE == 0 and n_ctx % ROW_TILE == 0 and n_lat % n_ctx == 0
    n_lat_tiles = n_lat // ROW_TILE
    w = RWKV_WIDTH

    pad = (-(b + 1)) % 8
    cc = jnp.concatenate([c, c_ctx[None, :], jnp.zeros((pad, d), F32)], axis=0)
    mods = _ada_table(cc, ada_w, ada_b)

    cos_t, sin_t = _rope_tables(n_lat, n_ctx)
    seg = _head_sum_matrix(ATT_WIDTH)
    att_w = ATT_WIDTH + 2 * ATT_KV_WIDTH
    cv_w = 3 * CONV_WIDTH
    offs = np.cumsum([0, att_w, RWKV_SEG, cv_w, 3 * d])

    xc = jnp.concatenate([x, ctx], axis=1)
    for l in range(depth):
        mod_lat = mods[l, :b].reshape(b, 1, 6, d)
        mod_ctx = jnp.broadcast_to(mods[l, b].reshape(1, 1, 6, d), (b, 1, 6, d))
        mt = jnp.concatenate([mod_lat, mod_ctx], axis=1)

        w_l = w_in[l].astype(BF16)
        gain1 = norm1[l].reshape(1, d)
        q, k, v, rw, cv, gt = _in_proj(
            xc, gain1, mt, cos_t, sin_t, jnp.tile(q_gain[l], ATT_HEADS).reshape(1, -1),
            jnp.tile(k_gain[l], ATT_KV_HEADS).reshape(1, -1), seg, [w_l[:, offs[i]:offs[i + 1]] for i in range(4)],
            n_lat_tiles)
        o_att = _attention(q, k, v, n_lat)

        r, kx, vx, g, nkk, lw, kd, bb = _rwkv_prep(
            rw, shift_mu[l].reshape(1, -1), _two_dir_lowrank(decay_w2[l]), decay_w0[l].reshape(1, -1),
            _two_dir_lowrank(iclr_a2[l]), iclr_a0[l].reshape(1, -1), gate_g2[l],
            rwkv_kk[l].reshape(1, -1), rwkv_ka[l].reshape(1, -1), seg, n_lat_tiles)
        y_f, y_b = _rwkv_scan(r, vx, nkk, lw, kd, bb, n_lat)

        xc, h2, aff_t = _merge(
            xc, o_att, y_f, y_b, r, kx, vx, g, cv, gt, mt, rwkv_rk[l].reshape(1, -1), rwkv_gn_w[l].reshape(1, -1),
            rwkv_gn_b[l].reshape(1, -1), seg, conv_w[l], w_br_att[l].astype(BF16), w_br_rwkv[l].astype(BF16),
            w_br_conv[l].astype(BF16), w_out[l].astype(BF16), norm2[l].reshape(1, d), w_router[l].T, n_lat_tiles)
        wg, wu, wd = exp_gate[l].astype(BF16), exp_up[l].astype(BF16), exp_down[l].astype(BF16)
        streams = [(0, n_lat, 0)] + ([(n_lat, n_ctx, 1)] if l < depth - 1 else [])
        for tok0, n, stream in streams:
            cap = CAPACITY_FACTOR * n // N_EXPERTS
            pos, gate = _select(aff_t, tok0, n, cap)
            moe = _experts(h2, pos, gate, wg, wu, wd, tok0, n, cap)
            xc = _moe_residual(xc, moe, mt, tok0, stream)
    return _final_norm(xc, final_norm.reshape(1, d), n_lat)
```

```python
import functools

import numpy as np
import jax
import jax.numpy as jnp
from jax import lax
from jax.experimental import pallas as pl
from jax.experimental.pallas import tpu as pltpu

F32 = jnp.float32
BF16 = jnp.bfloat16

GRID_W = 64
NORM_EPS = 1e-6
LOG2_E = float(np.log2(np.e))
ATT_HEADS = 8
ATT_KV_HEADS = 2
HEAD_DIM = 64
ATT_GROUP = ATT_HEADS // ATT_KV_HEADS
ATT_WIDTH = ATT_HEADS * HEAD_DIM
ATT_KV_WIDTH = ATT_KV_HEADS * HEAD_DIM
ROPE_THETA = 10000.0
RWKV_HEADS = 8
RWKV_WIDTH = RWKV_HEADS * HEAD_DIM
DECAY_RANK = 64
ICLR_RANK = 64
GATE_RANK = 128
RWKV_GN_EPS = 64e-5
RWKV_SEG = 3 * RWKV_WIDTH + 2 * DECAY_RANK + 2 * ICLR_RANK + GATE_RANK
CONV_WIDTH = 512
N_EXPERTS = 16
CAPACITY_FACTOR = 2

ROW_TILE = 256
CHUNK = 64
QUAD = 4 * HEAD_DIM
SCAN_BATCH = 4
HALO = 8
SELECT_BLOCK = 256
EXPERT_ROWS = 256
VMEM_LIMIT = 56 * 1024 * 1024

NT = (((1,), (1,)), ((), ()))
TN = (((0,), (0,)), ((), ()))
NN = (((1,), (0,)), ((), ()))


def _params(*sem):
    return pltpu.CompilerParams(dimension_semantics=sem, vmem_limit_bytes=VMEM_LIMIT)


def _layer_spec(a, layer):
    return pl.BlockSpec((1,) + a.shape[1:], lambda *_: (layer,) + (0,) * (a.ndim - 1))


def _dg(a, b, dn=NN):
    return lax.dot_general(a, b, dn, preferred_element_type=F32)


def _split2(x):
    hi = x.astype(BF16)
    lo = (x - hi.astype(F32)).astype(BF16)
    return hi, lo


def _mm1(a, b, dn=NN):
    return _dg(a.astype(BF16), b.astype(BF16), dn)


def _mm3(a, b, dn=NN):
    ah, al = _split2(a)
    bh, bl = _split2(b)
    return _dg(ah, bh, dn) + (_dg(ah, bl, dn) + _dg(al, bh, dn))


def _mm_exact_lhs(a_bf16, b, dn=NN):
    hi, lo = _split2(b)
    return _dg(a_bf16, hi, dn) + _dg(a_bf16, lo, dn)


def _head_sum(x, seg):
    return _dg(x.astype(BF16), seg)


def _sigmoid(x):
    return 1.0 / (1.0 + jnp.exp(-x))


def _rms(x):
    return x * lax.rsqrt(jnp.mean(x * x, axis=-1, keepdims=True) + NORM_EPS)


def _ada_kernel(c_ref, w_ref, b_ref, o_ref):
    c = c_ref[...]
    s = c * _sigmoid(c)
    o_ref[0] = _mm3(s, w_ref[0]) + b_ref[0]


def _ada_table(cc, ada_w, ada_b):
    n_layers, d, six_d = ada_w.shape
    rows = cc.shape[0]
    tn = six_d // 4
    return pl.pallas_call(
        _ada_kernel,
        grid=(n_layers, six_d // tn),
        in_specs=[pl.BlockSpec((rows, d), lambda l, j: (0, 0)),
                  pl.BlockSpec((1, d, tn), lambda l, j: (l, 0, j)),
                  pl.BlockSpec((1, 1, tn), lambda l, j: (l, 0, j))],
        out_specs=pl.BlockSpec((1, rows, tn), lambda l, j: (l, 0, j)),
        out_shape=jax.ShapeDtypeStruct((n_layers, rows, six_d), F32),
        compiler_params=_params("parallel", "parallel"),
        name="ada_table",
    )(cc, ada_w, ada_b.reshape(n_layers, 1, six_d))


def _swap16(x):
    w = x.shape[-1]
    lane = lax.broadcasted_iota(jnp.int32, x.shape, x.ndim - 1)
    return jnp.where((lane & 16) == 0, pltpu.roll(x, w - 16, x.ndim - 1), pltpu.roll(x, 16, x.ndim - 1))


def _att_heads(a, cos, sin, q_gain, k_gain, seg):
    def norm_rope(u, gain, cos_u, sin_u, seg_u):
        ms = _head_sum(u * u, seg_u) * (1.0 / HEAD_DIM)
        un = u * lax.rsqrt(ms + NORM_EPS) * gain
        return un * cos_u + _swap16(un) * sin_u

    kw = ATT_KV_WIDTH
    q = norm_rope(a[:, :ATT_WIDTH], q_gain, cos, sin, seg)
    k = norm_rope(a[:, ATT_WIDTH:ATT_WIDTH + kw], k_gain, cos[:, :kw], sin[:, :kw], seg[:kw, :kw])
    q = (q * (HEAD_DIM ** -0.5 * LOG2_E)).astype(BF16)
    ones = jnp.ones((a.shape[0], HEAD_DIM), BF16)
    v = a[:, ATT_WIDTH + kw:].astype(BF16)
    v1 = jnp.concatenate([t for h in range(ATT_KV_HEADS) for t in (v[:, h * HEAD_DIM:(h + 1) * HEAD_DIM], ones)],
                         axis=1)
    return q, k.astype(BF16), v1


def _in_proj_kernel(x_ref, gain_ref, mt_ref, cos_ref, sin_ref, qg_ref, kg_ref, seg_ref, wa_ref, wr_ref, wc_ref, wg_ref,
                    qvk_ref, rw_ref, cv_ref, gt_ref):
    h = _rms(x_ref[0]) * gain_ref[...]
    h = (h * (1.0 + mt_ref[0, 0, 1:2]) + mt_ref[0, 0, 0:1]).astype(BF16)
    q, k, v1 = _att_heads(_dg(h, wa_ref[0]), cos_ref[...], sin_ref[...], qg_ref[...], kg_ref[...], seg_ref[...])
    qvk_ref[0] = jnp.concatenate([q, v1, k], axis=1)
    rw_ref[0] = _dg(h, wr_ref[0]).astype(BF16)
    cv_ref[0] = _dg(h, wc_ref[0]).astype(BF16)
    gt_ref[0] = _dg(h, wg_ref[0]).astype(BF16)


def _in_proj(xc, gain, mt, cos_t, sin_t, q_gain_t, k_gain_t, seg, layer, weights, n_lat_tiles):
    b, s, d = xc.shape
    tok = lambda n: pl.BlockSpec((1, ROW_TILE, n), lambda i, j: (i, j, 0))
    full = lambda a: pl.BlockSpec(a.shape, lambda i, j: (0,) * a.ndim)
    table = pl.BlockSpec((ROW_TILE, ATT_WIDTH), lambda i, j: (j, 0))
    widths = [ATT_WIDTH + 3 * ATT_KV_WIDTH] + [w.shape[2] for w in weights[1:]]
    return pl.pallas_call(
        _in_proj_kernel,
        grid=(b, s // ROW_TILE),
        in_specs=[tok(d), full(gain), pl.BlockSpec((1, 1, 6, d), lambda i, j: (i, j // n_lat_tiles, 0, 0)),
                  table, table, full(q_gain_t), full(k_gain_t), full(seg)]
                 + [_layer_spec(w, layer) for w in weights],
        out_specs=[tok(n) for n in widths],
        out_shape=[jax.ShapeDtypeStruct((b, s, n), BF16) for n in widths],
        compiler_params=_params("parallel", "parallel"),
        name="in_proj",
    )(xc, gain, mt, cos_t, sin_t, q_gain_t, k_gain_t, seg, *weights)


def _attn_kernel(q_ref, k_ref, v_ref, o_ref, *, n_lat, n_lat_tiles):
    j = pl.program_id(1)
    s_all = k_ref.shape[1]

    def run(k0, k1):
        def scores(h):
            kvh = h // ATT_GROUP
            k = k_ref[0, k0:k1, kvh * HEAD_DIM:(kvh + 1) * HEAD_DIM]
            return _dg(q_ref[0, :, h * HEAD_DIM:(h + 1) * HEAD_DIM], k, NT)

        s_next = scores(0)
        for h in range(ATT_HEADS):
            s = s_next
            if h + 1 < ATT_HEADS:
                s_next = scores(h + 1)
            kvh = h // ATT_GROUP
            v1 = v_ref[0, k0:k1, 2 * kvh * HEAD_DIM:2 * (kvh + 1) * HEAD_DIM]
            p = jnp.exp2((s - jnp.max(s, axis=-1, keepdims=True)).astype(BF16))
            ov = _dg(p, v1)
            o = ov[:, :HEAD_DIM] / ov[:, HEAD_DIM:]
            o_ref[0, :, h * HEAD_DIM:(h + 1) * HEAD_DIM] = o.astype(BF16)

    @pl.when(j < n_lat_tiles)
    def _():
        run(0, s_all)

    @pl.when(j >= n_lat_tiles)
    def _():
        run(n_lat, s_all)


def _attention(qvk, n_lat):
    b, s, _ = qvk.shape
    v_w = 2 * ATT_KV_WIDTH
    return pl.pallas_call(
        functools.partial(_attn_kernel, n_lat=n_lat, n_lat_tiles=n_lat // ROW_TILE),
        grid=(b, s // ROW_TILE),
        in_specs=[pl.BlockSpec((1, ROW_TILE, ATT_WIDTH), lambda i, j: (i, j, 0)),
                  pl.BlockSpec((1, s, ATT_KV_WIDTH), lambda i, j: (i, 0, (ATT_WIDTH + v_w) // ATT_KV_WIDTH)),
                  pl.BlockSpec((1, s, v_w), lambda i, j: (i, 0, ATT_WIDTH // v_w))],
        out_specs=pl.BlockSpec((1, ROW_TILE, ATT_WIDTH), lambda i, j: (i, j, 0)),
        out_shape=jax.ShapeDtypeStruct((b, s, ATT_WIDTH), BF16),
        compiler_params=_params("parallel", "parallel"),
        name="attention",
    )(qvk, qvk, qvk)


def _neighbours(x, halo_prev, halo_next, first, last):
    rows = x.shape[0]
    ridx = lax.broadcasted_iota(jnp.int32, x.shape, 0)
    row_p = jnp.where(first, 0.0, halo_prev[halo_prev.shape[0] - 1:])
    row_n = jnp.where(last, 0.0, halo_next[0:1])
    prev = jnp.where(ridx == 0, row_p, pltpu.roll(x, 1, 0))
    nxt = jnp.where(ridx == rows - 1, row_n, pltpu.roll(x, rows - 1, 0))
    return prev, nxt


def _tile_ends(j, n_lat_tiles, n_tiles):
    first = jnp.logical_or(j == 0, j == n_lat_tiles)
    last = jnp.logical_or(j == n_lat_tiles - 1, j == n_tiles - 1)
    return first, last


def _halo_specs(width, n_tiles, dtype):
    halo = HALO * (4 // jnp.dtype(dtype).itemsize)
    per = ROW_TILE // halo
    prev = pl.BlockSpec((1, halo, width), lambda i, j: (i, jnp.maximum(j * per - 1, 0), 0))
    nxt = pl.BlockSpec((1, halo, width), lambda i, j: (i, jnp.minimum((j + 1) * per, n_tiles * per - 1), 0))
    return prev, nxt


def _rwkv_prep_kernel(x_ref, hp_ref, hn_ref, mu_ref, w2_ref, w0_ref, a2_ref, a0_ref, g2_ref, kk_ref, ka_ref,
                      seg_ref, rkvg_ref, nkk_ref, lw_ref, kd_ref, bb_ref, *, n_lat_tiles, n_tiles):
    j = pl.program_id(1)
    first, last = _tile_ends(j, n_lat_tiles, n_tiles)
    x = x_ref[0].astype(F32)
    prev, nxt = _neighbours(x, hp_ref[0].astype(F32), hn_ref[0].astype(F32), first, last)
    xs = x + mu_ref[...] * (0.5 * (prev + nxt) - x)
    w = RWKV_WIDTH
    r = xs[:, 0:w]
    k = xs[:, w:2 * w]
    v = xs[:, 2 * w:3 * w]
    o = 3 * w
    w_lo = xs[:, o:o + 2 * DECAY_RANK]
    a_lo = xs[:, o + 2 * DECAY_RANK:o + 2 * DECAY_RANK + 2 * ICLR_RANK]
    g_lo = xs[:, o + 2 * DECAY_RANK + 2 * ICLR_RANK:]
    w_pre = w0_ref[...] + _mm1(jnp.tanh(w_lo), w2_ref[...])
    logw = -_sigmoid(w_pre) * float(np.exp(-0.5))
    a = _sigmoid(a0_ref[...] + _mm1(a_lo, a2_ref[...]))
    g = _mm1(_sigmoid(g_lo), g2_ref[...])
    kk = k * kk_ref[...]
    kk = kk * lax.rsqrt(_head_sum(kk * kk, seg_ref[...]) + 1e-12)
    for i, t in enumerate((r, k, v, g)):
        rkvg_ref[0, :, i * w:(i + 1) * w] = t.astype(BF16)
    nkk_ref[0] = (-kk).astype(BF16)
    for d in range(2):
        a_d = a[:, d * w:(d + 1) * w]
        lw_ref[0, d] = logw[:, d * w:(d + 1) * w]
        kd_ref[0, d] = (k * (1.0 + (a_d - 1.0) * ka_ref[...])).astype(BF16)
        bb_ref[0, d] = (kk * a_d).astype(BF16)


def _rwkv_prep(rw, mu, w2cat, w0, a2cat, a0, g2, k_k, k_a, seg, n_lat_tiles):
    b, s, wseg = rw.shape
    n_tiles = s // ROW_TILE
    w = RWKV_WIDTH
    hp, hn = _halo_specs(wseg, n_tiles, rw.dtype)
    full = lambda shape: pl.BlockSpec(shape, lambda i, j: (0,) * len(shape))
    tok = pl.BlockSpec((1, ROW_TILE, w), lambda i, j: (i, j, 0))
    tok2 = pl.BlockSpec((1, 2, ROW_TILE, w), lambda i, j: (i, 0, j, 0))
    one = jax.ShapeDtypeStruct((b, s, w), BF16)
    two = jax.ShapeDtypeStruct((b, 2, s, w), BF16)
    two_f32 = jax.ShapeDtypeStruct((b, 2, s, w), F32)
    return pl.pallas_call(
        functools.partial(_rwkv_prep_kernel, n_lat_tiles=n_lat_tiles, n_tiles=n_tiles),
        grid=(b, n_tiles),
        in_specs=[pl.BlockSpec((1, ROW_TILE, wseg), lambda i, j: (i, j, 0)), hp, hn,
                  full((1, wseg)), full(w2cat.shape), full((1, 2 * w)), full(a2cat.shape), full((1, 2 * w)),
                  full(g2.shape), full((1, w)), full((1, w)), full(seg.shape)],
        out_specs=[pl.BlockSpec((1, ROW_TILE, 4 * w), lambda i, j: (i, j, 0)), tok, tok2, tok2, tok2],
        out_shape=[jax.ShapeDtypeStruct((b, s, 4 * w), BF16), one, two_f32, two, two],
        compiler_params=_params("parallel", "parallel"),
        name="rwkv_prep",
    )(rw, rw, rw, mu, w2cat, w0, a2cat, a0, g2, k_k, k_a, seg)


def _block_diag(y, head_masks):
    return jnp.concatenate([jnp.where(m, y, 0.0) for m in head_masks], axis=0)


def _scan_chain(sgn, r, v, nkk, lw, kd, bb, s0):
    row = lax.broadcasted_iota(jnp.int32, (CHUNK, QUAD), 0)
    lane = lax.broadcasted_iota(jnp.int32, (CHUNK, QUAD), 1)
    rel = ((lane % CHUNK) - row) * sgn
    strict = rel < 0
    incl = rel <= 0
    eye = jnp.where(rel == 0, 1.0, 0.0)
    head_masks = [(lane // HEAD_DIM) == h for h in range(4)]
    bd_mask = (lax.broadcasted_iota(jnp.int32, (QUAD, QUAD), 0) // HEAD_DIM
               == lax.broadcasted_iota(jnp.int32, (QUAD, QUAD), 1) // HEAD_DIM)
    t_row = lax.broadcasted_iota(jnp.int32, (CHUNK, CHUNK), 0)
    t_col = lax.broadcasted_iota(jnp.int32, (CHUNK, CHUNK), 1)
    tri = jnp.where((t_col - t_row) * sgn <= 0, 1.0, 0.0).astype(BF16)
    last_row = CHUNK - 1 if sgn > 0 else 0
    bd = lambda t: _block_diag(t, head_masks).astype(BF16)

    cum = _mm_exact_lhs(tri, lw)
    yield
    cum_end = cum[last_row:last_row + 1]
    w_inv = jnp.exp(-cum)
    w_rem = jnp.exp(cum_end - cum)
    ar = jnp.concatenate([nkk * jnp.exp(cum - lw), r * jnp.exp(cum)], axis=0).astype(BF16)
    xb = _dg(ar, bd(bb * w_inv), NT)
    xk = _dg(ar, bd(kd * w_inv), NT)
    ars = _dg(ar, s0.astype(BF16), NT)
    yield
    n_ab = jnp.where(strict, xb[:CHUNK], 0.0)
    l_ak = jnp.where(strict, xk[:CHUNK], 0.0)
    g_rb = jnp.where(incl, xb[CHUNK:], 0.0)
    g_rk = jnp.where(incl, xk[CHUNK:], 0.0)

    m = eye + n_ab
    p = _dg(n_ab.astype(BF16), bd(n_ab))
    lg = _dg(jnp.concatenate([l_ak, g_rk], axis=0).astype(BF16), bd(v))
    yield
    for _ in range(int(np.log2(CHUNK)) - 2):
        both = _dg(jnp.concatenate([p, m], axis=0).astype(BF16), bd(p))
        yield
        p = both[:CHUNK]
        m = m + both[CHUNK:]
    m = m + _dg(m.astype(BF16), bd(p))
    yield
    x0 = ars[:CHUNK] + lg[:CHUNK]
    u = _dg(m.astype(BF16), bd(x0))
    yield
    y = ars[CHUNK:] + _dg(g_rb.astype(BF16), bd(u)) + lg[CHUNK:]
    z = _dg(jnp.concatenate([u, v], axis=0).astype(BF16),
            jnp.concatenate([bb * w_rem, kd * w_rem], axis=0).astype(BF16), TN)
    yield
    return y, s0 * jnp.exp(cum_end) + jnp.where(bd_mask, z, 0.0)


def _run_interleaved(chains):
    results = [None] * len(chains)
    live = list(enumerate(chains))
    while live:
        still = []
        for idx, g in live:
            try:
                next(g)
                still.append((idx, g))
            except StopIteration as stop:
                results[idx] = stop.value
        live = still
    return results


def _rwkv_scan_kernel(rf_ref, vf_ref, nf_ref, lwf_ref, kdf_ref, bbf_ref, rb_ref, vb_ref, nb_ref, lwb_ref, kdb_ref,
                      bbb_ref, yf_ref, yb_ref, s_ref):
    @pl.when(pl.program_id(1) == 0)
    def _():
        s_ref[...] = jnp.zeros_like(s_ref)

    dirs = ((1, rf_ref, vf_ref, nf_ref, lwf_ref, kdf_ref, bbf_ref, yf_ref),
            (-1, rb_ref, vb_ref, nb_ref, lwb_ref, kdb_ref, bbb_ref, yb_ref))
    work = []
    for i in range(SCAN_BATCH):
        for d, (sgn, r_ref, v_ref, n_ref, lw_ref, kd_ref, bb_ref, y_ref) in enumerate(dirs):
            for q in range(RWKV_WIDTH // QUAD):
                ql = slice(q * QUAD, (q + 1) * QUAD)
                f32 = lambda t: t.astype(F32)
                args = (f32(r_ref[i, :, ql]), f32(v_ref[i, :, ql]), f32(n_ref[i, :, ql]), lw_ref[i, 0, :, ql],
                        f32(kd_ref[i, 0, :, ql]), f32(bb_ref[i, 0, :, ql]), s_ref[i, d, q])
                work.append((sgn, args, y_ref, (i, slice(None), ql), (i, d, q)))
    results = _run_interleaved([_scan_chain(sgn, *args) for sgn, args, _, _, _ in work])
    for (_, _, y_ref, y_idx, s_idx), (y, s_new) in zip(work, results):
        y_ref[y_idx] = y
        s_ref[s_idx] = s_new


def _rwkv_scan(rkvg, nkk, lw, kd, bb, n_lat):
    b, s, w = nkk.shape
    nc = s // CHUNK
    nc_lat = n_lat // CHUNK
    nc_ctx = nc - nc_lat
    nb = SCAN_BATCH
    assert b % nb == 0

    fwd = lambda c: jnp.where(c < nc_ctx, nc_lat + c, c - nc_ctx)
    bwd = lambda c: jnp.where(c < nc_ctx, nc - 1 - c, nc_lat - 1 - (c - nc_ctx))
    tok = lambda order, col=0: pl.BlockSpec((nb, CHUNK, w), lambda i, c: (i, order(c), col))
    tok2 = lambda order, d: pl.BlockSpec((nb, 1, CHUNK, w), lambda i, c: (i, d, order(c), 0))
    out = jax.ShapeDtypeStruct((b, s, w), F32)
    r_col, v_col = 0, 2
    return pl.pallas_call(
        _rwkv_scan_kernel,
        grid=(b // nb, nc),
        in_specs=[tok(fwd, r_col), tok(fwd, v_col), tok(fwd), tok2(fwd, 0), tok2(fwd, 0), tok2(fwd, 0),
                  tok(bwd, r_col), tok(bwd, v_col), tok(bwd), tok2(bwd, 1), tok2(bwd, 1), tok2(bwd, 1)],
        out_specs=[tok(fwd), tok(bwd)],
        out_shape=[out, out],
        scratch_shapes=[pltpu.VMEM((nb, 2, w // QUAD, QUAD, QUAD), F32)],
        compiler_params=_params("parallel", "arbitrary"),
        name="rwkv_scan",
    )(rkvg, rkvg, nkk, lw, kd, bb, rkvg, rkvg, nkk, lw, kd, bb)


def _rwkv_readout(y, r, k, v, g, r_k, gn_w, gn_b, seg):
    mean = _head_sum(y, seg) * (1.0 / HEAD_DIM)
    yc = y - mean
    var = _head_sum(yc * yc, seg) * (1.0 / HEAD_DIM)
    yn = yc * lax.rsqrt(var + RWKV_GN_EPS)
    bonus = _head_sum(r * k * r_k, seg) * v
    return (yn * gn_w + gn_b + bonus) * g


def _route(x, gain, shift, scale, w_router_t):
    h = _rms(x) * gain
    h = h * (1.0 + scale) + shift
    logits = _mm3(w_router_t, h, NT)
    e = jnp.exp(logits - jnp.max(logits, axis=0, keepdims=True))
    return h, e / jnp.sum(e, axis=0, keepdims=True)


def _merge_kernel(x_ref, oa_ref, yf_ref, yb_ref, rkvg_ref, cv_ref, hp_ref, hn_ref, gt_ref, mt_ref,
                  rk_ref, gw_ref, gb_ref, seg_ref, cw_ref, gain2_ref, wrt_ref, wa_ref, wr_ref, wc_ref, wo_ref,
                  o_ref, h_ref, aff_ref, *, n_lat_tiles, n_tiles):
    j = pl.program_id(1)
    first, last = _tile_ends(j, n_lat_tiles, n_tiles)
    cw = CONV_WIDTH
    d = x_ref.shape[2]
    w = RWKV_WIDTH
    r, k, v, g = [rkvg_ref[0, :, i * w:(i + 1) * w].astype(F32) for i in range(4)]
    o_rw = _rwkv_readout(yf_ref[0] + yb_ref[0], r, k, v, g, rk_ref[...], gw_ref[...], gb_ref[...], seg_ref[...])
    cv = cv_ref[0].astype(F32)
    hp = hp_ref[0].astype(F32)
    hn = hn_ref[0].astype(F32)
    z = cv[:, cw:2 * cw] * cv[:, 2 * cw:]
    zp = hp[:, cw:2 * cw] * hp[:, 2 * cw:]
    zn = hn[:, cw:2 * cw] * hn[:, 2 * cw:]
    z_prev, z_next = _neighbours(z, zp, zn, first, last)
    o_cv = cv[:, :cw] * (cw_ref[0:1] * z_prev + cw_ref[1:2] * z + cw_ref[2:3] * z_next)
    gt = gt_ref[0].astype(F32)
    m = (_sigmoid(gt[:, :d]) * _dg(oa_ref[0], wa_ref[0])
         + _sigmoid(gt[:, d:2 * d]) * _dg(o_rw.astype(BF16), wr_ref[0])
         + _sigmoid(gt[:, 2 * d:]) * _dg(o_cv.astype(BF16), wc_ref[0]))
    y = _dg(m.astype(BF16), wo_ref[0])
    x_new = x_ref[0] + mt_ref[0, 0, 2:3] * y
    o_ref[0] = x_new
    h, aff = _route(x_new, gain2_ref[...], mt_ref[0, 0, 3:4], mt_ref[0, 0, 4:5], wrt_ref[...])
    h_ref[0] = h.astype(BF16)
    aff_ref[0] = aff


def _merge(xc, o_att, y_f, y_b, rkvg, cv, gt, mt, r_k, gn_w, gn_b, seg, conv_w, gain2, w_router_t, layer, stacked,
           n_lat_tiles):
    b, s, d = xc.shape
    n_tiles = s // ROW_TILE
    ne = w_router_t.shape[0]
    hp, hn = _halo_specs(cv.shape[2], n_tiles, cv.dtype)
    tok = lambda a: pl.BlockSpec((1, ROW_TILE, a.shape[2]), lambda i, j: (i, j, 0))
    full = lambda a: pl.BlockSpec(a.shape, lambda i, j: (0,) * a.ndim)
    consts = (r_k, gn_w, gn_b, seg, conv_w, gain2, w_router_t)
    return pl.pallas_call(
        functools.partial(_merge_kernel, n_lat_tiles=n_lat_tiles, n_tiles=n_tiles),
        grid=(b, n_tiles),
        in_specs=[tok(a) for a in (xc, o_att, y_f, y_b, rkvg, cv)] + [hp, hn, tok(gt),
                  pl.BlockSpec((1, 1, 6, d), lambda i, j: (i, j // n_lat_tiles, 0, 0))]
                 + [full(a) for a in consts] + [_layer_spec(a, layer) for a in stacked],
        out_specs=[tok(xc), tok(xc), pl.BlockSpec((1, ne, ROW_TILE), lambda i, j: (i, 0, j))],
        out_shape=[jax.ShapeDtypeStruct((b, s, d), F32), jax.ShapeDtypeStruct((b, s, d), BF16),
                   jax.ShapeDtypeStruct((b, ne, s), F32)],
        compiler_params=_params("parallel", "parallel"),
        name="merge",
    )(xc, o_att, y_f, y_b, rkvg, cv, cv, cv, gt, mt, *consts, *stacked)


def _select_kernel(aff_ref, pos_ref, gate_ref, *, cap):
    a = aff_ref[0]
    ne, n = a.shape
    bits = pltpu.bitcast(a, jnp.int32)

    def count(mask):
        return jnp.sum(jnp.where(mask, 1.0, 0.0), axis=1, keepdims=True)

    def body(_, carry):
        lo, hi = carry
        mid = lo + ((hi - lo + 1) >> 1)
        ok = count(bits >= mid) >= cap
        return jnp.where(ok, mid, lo), jnp.where(ok, hi, mid - 1)

    lo0 = jnp.zeros((ne, 1), jnp.int32)
    hi0 = jnp.full((ne, 1), 0x7F800000, jnp.int32)
    thr, _ = lax.fori_loop(0, 32, body, (lo0, hi0))
    gt = bits > thr
    eq = bits == thr
    need = cap - count(gt)

    def tokens_before(mask):
        m = jnp.where(mask, 1.0, 0.0).astype(BF16)
        blk = min(n, SELECT_BLOCK)
        cols = []
        for j in range(n // blk):
            s_idx = lax.broadcasted_iota(jnp.int32, (n, blk), 0)
            t_idx = lax.broadcasted_iota(jnp.int32, (n, blk), 1) + j * blk
            cols.append(_dg(m, jnp.where(s_idx < t_idx, 1.0, 0.0).astype(BF16)))
        return jnp.concatenate(cols, axis=1)

    sel = jnp.logical_or(gt, jnp.logical_and(eq, tokens_before(eq) < need))
    pos_ref[0] = jnp.where(sel, tokens_before(sel).astype(jnp.int32), -1)
    gate_ref[0] = jnp.where(sel, a, 0.0)


def _select(aff_t, tok0, n, cap):
    b, ne, _ = aff_t.shape
    blk = tok0 // n
    return pl.pallas_call(
        functools.partial(_select_kernel, cap=cap),
        grid=(b,),
        in_specs=[pl.BlockSpec((1, ne, n), lambda i: (i, 0, blk))],
        out_specs=[pl.BlockSpec((1, ne, n), lambda i: (i, 0, 0)), pl.BlockSpec((1, ne, n), lambda i: (i, 0, 0))],
        out_shape=[jax.ShapeDtypeStruct((b, ne, n), jnp.int32), jax.ShapeDtypeStruct((b, ne, n), F32)],
        compiler_params=_params("parallel"),
        name="moe_select",
    )(aff_t)


def _expert_kernel(h_ref, pos_ref, gate_ref, wg_ref, wu_ref, wd_ref, o_ref, *, cap):
    e = pl.program_id(1)

    @pl.when(e == 0)
    def _():
        o_ref[...] = jnp.zeros_like(o_ref)

    nb, n, _ = h_ref.shape
    slot = lax.broadcasted_iota(jnp.int32, (cap, n), 0)
    onehots, xes, gates = [], [], []
    for i in range(nb):
        hit = pos_ref[i, 0] == slot
        onehot = jnp.where(hit, 1.0, 0.0).astype(BF16)
        onehots.append(onehot)
        xes.append(_dg(onehot, h_ref[i]).astype(BF16))
        gates.append(jnp.sum(jnp.where(hit, gate_ref[i, 0], 0.0), axis=1, keepdims=True))
    xe = jnp.concatenate(xes, axis=0)
    hg = _dg(xe, wg_ref[0])
    hu = _dg(xe, wu_ref[0])
    hid = (hg * _sigmoid(hg) * hu).astype(BF16)
    ye = (_dg(hid, wd_ref[0]) * jnp.concatenate(gates, axis=0)).astype(BF16)
    for i in range(nb):
        o_ref[i] += _dg(onehots[i], ye[i * cap:(i + 1) * cap], TN)


def _experts(h, pos, gate, wg, wu, wd, layer, tok0, n, cap):
    b, s, d = h.shape
    ne = pos.shape[1]
    f = wg.shape[2]
    blk = tok0 // n
    nb = max(1, min(b // 2, EXPERT_ROWS // cap))
    assert b % nb == 0
    sel = pl.BlockSpec((nb, 1, 1, n), lambda i, e: (i, e, 0, 0))
    return pl.pallas_call(
        functools.partial(_expert_kernel, cap=cap),
        grid=(b // nb, ne),
        in_specs=[pl.BlockSpec((nb, n, d), lambda i, e: (i, blk, 0)), sel, sel,
                  pl.BlockSpec((1, d, f), lambda i, e: (layer * ne + e, 0, 0)),
                  pl.BlockSpec((1, d, f), lambda i, e: (layer * ne + e, 0, 0)),
                  pl.BlockSpec((1, f, d), lambda i, e: (layer * ne + e, 0, 0))],
        out_specs=pl.BlockSpec((nb, n, d), lambda i, e: (i, 0, 0)),
        out_shape=jax.ShapeDtypeStruct((b, n, d), F32),
        compiler_params=_params("parallel", "arbitrary"),
        name="moe_experts",
    )(h, pos.reshape(b, ne, 1, n), gate.reshape(b, ne, 1, n), wg, wu, wd)


def _residual_kernel(x_ref, m_ref, mt_ref, o_ref):
    o_ref[0] = x_ref[0] + mt_ref[0, 0, 5:6] * m_ref[0]


def _moe_residual(xc, moe, mt, tok0, stream):
    b, s, d = xc.shape
    n = moe.shape[1]
    t0 = tok0 // ROW_TILE
    return pl.pallas_call(
        _residual_kernel,
        grid=(b, n // ROW_TILE),
        in_specs=[pl.BlockSpec((1, ROW_TILE, d), lambda i, j: (i, t0 + j, 0)),
                  pl.BlockSpec((1, ROW_TILE, d), lambda i, j: (i, j, 0)),
                  pl.BlockSpec((1, 1, 6, d), lambda i, j: (i, stream, 0, 0))],
        out_specs=pl.BlockSpec((1, ROW_TILE, d), lambda i, j: (i, t0 + j, 0)),
        out_shape=jax.ShapeDtypeStruct((b, s, d), F32),
        input_output_aliases={0: 0},
        compiler_params=_params("parallel", "parallel"),
        name="moe_residual",
    )(xc, moe, mt)


def _final_norm_kernel(x_ref, g_ref, o_ref):
    o_ref[0] = _rms(x_ref[0]) * g_ref[...]


def _final_norm(xc, gain, n_lat):
    b, s, d = xc.shape
    return pl.pallas_call(
        _final_norm_kernel,
        grid=(b, n_lat // ROW_TILE),
        in_specs=[pl.BlockSpec((1, ROW_TILE, d), lambda i, j: (i, j, 0)),
                  pl.BlockSpec((1, d), lambda i, j: (0, 0))],
        out_specs=pl.BlockSpec((1, ROW_TILE, d), lambda i, j: (i, j, 0)),
        out_shape=jax.ShapeDtypeStruct((b, n_lat, d), F32),
        compiler_params=_params("parallel", "parallel"),
        name="final_norm",
    )(xc, gain)


def _rope_tables(n_lat, n_ctx):
    rows = n_lat // GRID_W
    row = jnp.repeat(jnp.arange(rows, dtype=F32), GRID_W)
    col = jnp.tile(jnp.arange(GRID_W, dtype=F32), rows)
    axis_dim = HEAD_DIM // 2
    inv_freq = ROPE_THETA ** (-jnp.arange(0, axis_dim, 2, dtype=F32) / axis_dim)
    ang_r = row[:, None] * inv_freq[None, :]
    ang_c = col[:, None] * inv_freq[None, :]
    cos_h = jnp.concatenate([jnp.cos(ang_r), jnp.cos(ang_r), jnp.cos(ang_c), jnp.cos(ang_c)], axis=1)
    sin_h = jnp.concatenate([-jnp.sin(ang_r), jnp.sin(ang_r), -jnp.sin(ang_c), jnp.sin(ang_c)], axis=1)
    cos_h = jnp.concatenate([cos_h, jnp.ones((n_ctx, HEAD_DIM), F32)], axis=0)
    sin_h = jnp.concatenate([sin_h, jnp.zeros((n_ctx, HEAD_DIM), F32)], axis=0)
    return jnp.tile(cos_h, (1, ATT_HEADS)), jnp.tile(sin_h, (1, ATT_HEADS))


def _head_sum_matrix(width):
    idx = np.arange(width) // HEAD_DIM
    return jnp.asarray(idx[:, None] == idx[None, :], dtype=BF16)


def _two_dir_lowrank(w2):
    _, rank, w = w2.shape
    z = jnp.zeros((rank, w), w2.dtype)
    return jnp.concatenate([jnp.concatenate([w2[0], z], axis=1), jnp.concatenate([z, w2[1]], axis=1)], axis=0)


def kernel(x, c, ctx, c_ctx, ada_w, ada_b, norm1, w_in, q_gain, k_gain, shift_mu, decay_w0, decay_w2, iclr_a0, iclr_a2, gate_g2, rwkv_kk, rwkv_ka, rwkv_rk, rwkv_gn_w, rwkv_gn_b, conv_w, w_br_att, w_br_rwkv, w_br_conv, w_out, norm2, w_router, exp_gate, exp_up, exp_down, final_norm):
    b, n_lat, d = x.shape
    n_ctx = ctx.shape[1]
    depth = ada_w.shape[0]
    assert n_lat % ROW_TILE == 0 and n_ctx % ROW_TILE == 0 and n_lat % n_ctx == 0
    n_lat_tiles = n_lat // ROW_TILE

    pad = (-(b + 1)) % 8
    cc = jnp.concatenate([c, c_ctx[None, :], jnp.zeros((pad, d), F32)], axis=0)
    mods = _ada_table(cc, ada_w, ada_b)

    cos_t, sin_t = _rope_tables(n_lat, n_ctx)
    seg = _head_sum_matrix(ATT_WIDTH)
    att_w = ATT_WIDTH + 2 * ATT_KV_WIDTH
    cv_w = 3 * CONV_WIDTH
    offs = np.cumsum([0, att_w, RWKV_SEG, cv_w, 3 * d])

    w_groups = [w_in[:, :, offs[i]:offs[i + 1]].astype(BF16) for i in range(4)]
    w_tail = [t.astype(BF16) for t in (w_br_att, w_br_rwkv, w_br_conv, w_out)]
    ne, f = exp_gate.shape[1], exp_gate.shape[3]
    wg = exp_gate.astype(BF16).reshape(depth * ne, d, f)
    wu = exp_up.astype(BF16).reshape(depth * ne, d, f)
    wd = exp_down.astype(BF16).reshape(depth * ne, f, d)

    xc = jnp.concatenate([x, ctx], axis=1)
    for l in range(depth):
        mod_lat = mods[l, :b].reshape(b, 1, 6, d)
        mod_ctx = jnp.broadcast_to(mods[l, b].reshape(1, 1, 6, d), (b, 1, 6, d))
        mt = jnp.concatenate([mod_lat, mod_ctx], axis=1)

        gain1 = norm1[l].reshape(1, d)
        qvk, rw, cv, gt = _in_proj(
            xc, gain1, mt, cos_t, sin_t, jnp.tile(q_gain[l], ATT_HEADS).reshape(1, -1),
            jnp.tile(k_gain[l], ATT_KV_HEADS).reshape(1, -1), seg, l, w_groups, n_lat_tiles)
        o_att = _attention(qvk, n_lat)

        rkvg, nkk, lw, kd, bb = _rwkv_prep(
            rw, shift_mu[l].reshape(1, -1), _two_dir_lowrank(decay_w2[l]), decay_w0[l].reshape(1, -1),
            _two_dir_lowrank(iclr_a2[l]), iclr_a0[l].reshape(1, -1), gate_g2[l],
            rwkv_kk[l].reshape(1, -1), rwkv_ka[l].reshape(1, -1), seg, n_lat_tiles)
        y_f, y_b = _rwkv_scan(rkvg, nkk, lw, kd, bb, n_lat)

        xc, h2, aff_t = _merge(
            xc, o_att, y_f, y_b, rkvg, cv, gt, mt, rwkv_rk[l].reshape(1, -1), rwkv_gn_w[l].reshape(1, -1),
            rwkv_gn_b[l].reshape(1, -1), seg, conv_w[l], norm2[l].reshape(1, d), w_router[l].T, l, w_tail,
            n_lat_tiles)
        streams = [(0, n_lat, 0)] + ([(n_lat, n_ctx, 1)] if l < depth - 1 else [])
        for tok0, n, stream in streams:
            cap = CAPACITY_FACTOR * n // N_EXPERTS
            pos, gate = _select(aff_t, tok0, n, cap)
            moe = _experts(h2, pos, gate, wg, wu, wd, l, tok0, n, cap)
            xc = _moe_residual(xc, moe, mt, tok0, stream)
    return _final_norm(xc, final_norm.reshape(1, d), n_lat)
```

```python
import functools

import numpy as np
import jax
import jax.numpy as jnp
from jax import lax
from jax.experimental import pallas as pl
from jax.experimental.pallas import tpu as pltpu

F32 = jnp.float32
BF16 = jnp.bfloat16

GRID_W = 64
NORM_EPS = 1e-6
LOG2_E = float(np.log2(np.e))
ATT_HEADS = 8
ATT_KV_HEADS = 2
HEAD_DIM = 64
ATT_GROUP = ATT_HEADS // ATT_KV_HEADS
ATT_WIDTH = ATT_HEADS * HEAD_DIM
ATT_KV_WIDTH = ATT_KV_HEADS * HEAD_DIM
ROPE_THETA = 10000.0
RWKV_HEADS = 8
RWKV_WIDTH = RWKV_HEADS * HEAD_DIM
DECAY_RANK = 64
ICLR_RANK = 64
GATE_RANK = 128
RWKV_GN_EPS = 64e-5
RWKV_SEG = 3 * RWKV_WIDTH + 2 * DECAY_RANK + 2 * ICLR_RANK + GATE_RANK
CONV_WIDTH = 512
N_EXPERTS = 16
CAPACITY_FACTOR = 2

ROW_TILE = 256
CHUNK = 64
QUAD = 4 * HEAD_DIM
SCAN_BATCH = 4
SCAN_CHUNKS = 2
TOKEN_BATCH = 2
HALO = 8
SELECT_BLOCK = 256
EXPERT_ROWS = 256
VMEM_LIMIT = 56 * 1024 * 1024

NT = (((1,), (1,)), ((), ()))
TN = (((0,), (0,)), ((), ()))
NN = (((1,), (0,)), ((), ()))


def _params(*sem):
    return pltpu.CompilerParams(dimension_semantics=sem, vmem_limit_bytes=VMEM_LIMIT)


def _layer_spec(a, layer):
    return pl.BlockSpec((1,) + a.shape[1:], lambda *_: (layer,) + (0,) * (a.ndim - 1))


def _dg(a, b, dn=NN):
    return lax.dot_general(a, b, dn, preferred_element_type=F32)


def _split2(x):
    hi = x.astype(BF16)
    lo = (x - hi.astype(F32)).astype(BF16)
    return hi, lo


def _mm1(a, b, dn=NN):
    return _dg(a.astype(BF16), b.astype(BF16), dn)


def _mm3(a, b, dn=NN):
    ah, al = _split2(a)
    bh, bl = _split2(b)
    return _dg(ah, bh, dn) + (_dg(ah, bl, dn) + _dg(al, bh, dn))


def _mm_exact_lhs(a_bf16, b, dn=NN):
    hi, lo = _split2(b)
    return _dg(a_bf16, hi, dn) + _dg(a_bf16, lo, dn)


def _head_sum(x, seg):
    return _dg(x.astype(BF16), seg)


def _sigmoid(x):
    return 1.0 / (1.0 + jnp.exp(-x))


def _rms(x):
    return x * lax.rsqrt(jnp.mean(x * x, axis=-1, keepdims=True) + NORM_EPS)


def _ada_kernel(c_ref, w_ref, b_ref, o_ref):
    c = c_ref[...]
    s = c * _sigmoid(c)
    o_ref[0] = _mm3(s, w_ref[0]) + b_ref[0]


def _ada_table(cc, ada_w, ada_b):
    n_layers, d, six_d = ada_w.shape
    rows = cc.shape[0]
    tn = six_d // 4
    return pl.pallas_call(
        _ada_kernel,
        grid=(n_layers, six_d // tn),
        in_specs=[pl.BlockSpec((rows, d), lambda l, j: (0, 0)),
                  pl.BlockSpec((1, d, tn), lambda l, j: (l, 0, j)),
                  pl.BlockSpec((1, 1, tn), lambda l, j: (l, 0, j))],
        out_specs=pl.BlockSpec((1, rows, tn), lambda l, j: (l, 0, j)),
        out_shape=jax.ShapeDtypeStruct((n_layers, rows, six_d), F32),
        compiler_params=_params("parallel", "parallel"),
        name="ada_table",
    )(cc, ada_w, ada_b.reshape(n_layers, 1, six_d))


def _swap16(x):
    w = x.shape[-1]
    lane = lax.broadcasted_iota(jnp.int32, x.shape, x.ndim - 1)
    return jnp.where((lane & 16) == 0, pltpu.roll(x, w - 16, x.ndim - 1), pltpu.roll(x, 16, x.ndim - 1))


def _att_heads(a, cos, sin, q_gain, k_gain, seg):
    def norm_rope(u, gain, cos_u, sin_u, seg_u):
        ms = _head_sum(u * u, seg_u) * (1.0 / HEAD_DIM)
        un = u * lax.rsqrt(ms + NORM_EPS) * gain
        return un * cos_u + _swap16(un) * sin_u

    kw = ATT_KV_WIDTH
    q = norm_rope(a[:, :ATT_WIDTH], q_gain, cos, sin, seg)
    k = norm_rope(a[:, ATT_WIDTH:ATT_WIDTH + kw], k_gain, cos[:, :kw], sin[:, :kw], seg[:kw, :kw])
    q = (q * (HEAD_DIM ** -0.5 * LOG2_E)).astype(BF16)
    ones = jnp.ones((a.shape[0], HEAD_DIM), BF16)
    v = a[:, ATT_WIDTH + kw:].astype(BF16)
    v1 = jnp.concatenate([t for h in range(ATT_KV_HEADS) for t in (v[:, h * HEAD_DIM:(h + 1) * HEAD_DIM], ones)],
                         axis=1)
    return q, k.astype(BF16), v1


def _in_proj_kernel(x_ref, gain_ref, mt_ref, cos_ref, sin_ref, qg_ref, kg_ref, seg_ref, wa_ref, wr_ref, wc_ref, wg_ref,
                    qvk_ref, rw_ref, cv_ref, gt_ref):
    for i in range(x_ref.shape[0]):
        h = _rms(x_ref[i]) * gain_ref[...]
        h = (h * (1.0 + mt_ref[i, 0, 1:2]) + mt_ref[i, 0, 0:1]).astype(BF16)
        q, k, v1 = _att_heads(_dg(h, wa_ref[0]), cos_ref[...], sin_ref[...], qg_ref[...], kg_ref[...], seg_ref[...])
        qvk_ref[i] = jnp.concatenate([q, v1, k], axis=1)
        rw_ref[i] = _dg(h, wr_ref[0]).astype(BF16)
        cv_ref[i] = _dg(h, wc_ref[0]).astype(BF16)
        gt_ref[i] = _dg(h, wg_ref[0]).astype(BF16)


def _in_proj(xc, gain, mt, cos_t, sin_t, q_gain_t, k_gain_t, seg, layer, weights, n_lat_tiles):
    b, s, d = xc.shape
    nb = TOKEN_BATCH
    tok = lambda n: pl.BlockSpec((nb, ROW_TILE, n), lambda i, j: (i, j, 0))
    full = lambda a: pl.BlockSpec(a.shape, lambda i, j: (0,) * a.ndim)
    table = pl.BlockSpec((ROW_TILE, ATT_WIDTH), lambda i, j: (j, 0))
    widths = [ATT_WIDTH + 3 * ATT_KV_WIDTH] + [w.shape[2] for w in weights[1:]]
    return pl.pallas_call(
        _in_proj_kernel,
        grid=(b // nb, s // ROW_TILE),
        in_specs=[tok(d), full(gain), pl.BlockSpec((nb, 1, 6, d), lambda i, j: (i, j // n_lat_tiles, 0, 0)),
                  table, table, full(q_gain_t), full(k_gain_t), full(seg)]
                 + [_layer_spec(w, layer) for w in weights],
        out_specs=[tok(n) for n in widths],
        out_shape=[jax.ShapeDtypeStruct((b, s, n), BF16) for n in widths],
        compiler_params=_params("parallel", "parallel"),
        name="in_proj",
    )(xc, gain, mt, cos_t, sin_t, q_gain_t, k_gain_t, seg, *weights)


def _attn_kernel(q_ref, k_ref, v_ref, o_ref, *, n_lat, n_lat_tiles):
    j = pl.program_id(1)
    s_all = k_ref.shape[1]

    def run(k0, k1):
        def scores(h):
            kvh = h // ATT_GROUP
            k = k_ref[0, k0:k1, kvh * HEAD_DIM:(kvh + 1) * HEAD_DIM]
            return _dg(q_ref[0, :, h * HEAD_DIM:(h + 1) * HEAD_DIM], k, NT)

        s_next = scores(0)
        for h in range(ATT_HEADS):
            s = s_next
            if h + 1 < ATT_HEADS:
                s_next = scores(h + 1)
            kvh = h // ATT_GROUP
            v1 = v_ref[0, k0:k1, 2 * kvh * HEAD_DIM:2 * (kvh + 1) * HEAD_DIM]
            p = jnp.exp2((s - jnp.max(s, axis=-1, keepdims=True)).astype(BF16))
            ov = _dg(p, v1)
            o = ov[:, :HEAD_DIM] / ov[:, HEAD_DIM:]
            o_ref[0, :, h * HEAD_DIM:(h + 1) * HEAD_DIM] = o.astype(BF16)

    @pl.when(j < n_lat_tiles)
    def _():
        run(0, s_all)

    @pl.when(j >= n_lat_tiles)
    def _():
        run(n_lat, s_all)


def _attention(qvk, n_lat):
    b, s, _ = qvk.shape
    v_w = 2 * ATT_KV_WIDTH
    return pl.pallas_call(
        functools.partial(_attn_kernel, n_lat=n_lat, n_lat_tiles=n_lat // ROW_TILE),
        grid=(b, s // ROW_TILE),
        in_specs=[pl.BlockSpec((1, ROW_TILE, ATT_WIDTH), lambda i, j: (i, j, 0)),
                  pl.BlockSpec((1, s, ATT_KV_WIDTH), lambda i, j: (i, 0, (ATT_WIDTH + v_w) // ATT_KV_WIDTH)),
                  pl.BlockSpec((1, s, v_w), lambda i, j: (i, 0, ATT_WIDTH // v_w))],
        out_specs=pl.BlockSpec((1, ROW_TILE, ATT_WIDTH), lambda i, j: (i, j, 0)),
        out_shape=jax.ShapeDtypeStruct((b, s, ATT_WIDTH), BF16),
        compiler_params=_params("parallel", "parallel"),
        name="attention",
    )(qvk, qvk, qvk)


def _neighbours(x, halo_prev, halo_next, first, last):
    rows = x.shape[0]
    ridx = lax.broadcasted_iota(jnp.int32, x.shape, 0)
    row_p = jnp.where(first, 0.0, halo_prev[halo_prev.shape[0] - 1:])
    row_n = jnp.where(last, 0.0, halo_next[0:1])
    prev = jnp.where(ridx == 0, row_p, pltpu.roll(x, 1, 0))
    nxt = jnp.where(ridx == rows - 1, row_n, pltpu.roll(x, rows - 1, 0))
    return prev, nxt


def _tile_ends(j, n_lat_tiles, n_tiles):
    first = jnp.logical_or(j == 0, j == n_lat_tiles)
    last = jnp.logical_or(j == n_lat_tiles - 1, j == n_tiles - 1)
    return first, last


def _halo_specs(width, n_tiles, dtype):
    halo = HALO * (4 // jnp.dtype(dtype).itemsize)
    per = ROW_TILE // halo
    nb = TOKEN_BATCH
    prev = pl.BlockSpec((nb, halo, width), lambda i, j: (i, jnp.maximum(j * per - 1, 0), 0))
    nxt = pl.BlockSpec((nb, halo, width), lambda i, j: (i, jnp.minimum((j + 1) * per, n_tiles * per - 1), 0))
    return prev, nxt


def _rwkv_prep_kernel(x_ref, hp_ref, hn_ref, mu_ref, w2_ref, w0_ref, a2_ref, a0_ref, g2_ref, kk_ref, ka_ref,
                      seg_ref, rkvg_ref, nkk_ref, lw_ref, kd_ref, bb_ref, *, n_lat_tiles, n_tiles):
    j = pl.program_id(1)
    first, last = _tile_ends(j, n_lat_tiles, n_tiles)
    w = RWKV_WIDTH
    for i in range(x_ref.shape[0]):
        x = x_ref[i].astype(F32)
        prev, nxt = _neighbours(x, hp_ref[i].astype(F32), hn_ref[i].astype(F32), first, last)
        xs = x + mu_ref[...] * (0.5 * (prev + nxt) - x)
        r = xs[:, 0:w]
        k = xs[:, w:2 * w]
        v = xs[:, 2 * w:3 * w]
        o = 3 * w
        w_lo = xs[:, o:o + 2 * DECAY_RANK]
        a_lo = xs[:, o + 2 * DECAY_RANK:o + 2 * DECAY_RANK + 2 * ICLR_RANK]
        g_lo = xs[:, o + 2 * DECAY_RANK + 2 * ICLR_RANK:]
        w_pre = w0_ref[...] + _mm1(jnp.tanh(w_lo), w2_ref[...])
        logw = -_sigmoid(w_pre) * float(np.exp(-0.5))
        a = _sigmoid(a0_ref[...] + _mm1(a_lo, a2_ref[...]))
        g = _mm1(_sigmoid(g_lo), g2_ref[...])
        kk = k * kk_ref[...]
        kk = kk * lax.rsqrt(_head_sum(kk * kk, seg_ref[...]) + 1e-12)
        for c, t in enumerate((r, k, v, g)):
            rkvg_ref[i, :, c * w:(c + 1) * w] = t.astype(BF16)
        nkk_ref[i] = (-kk).astype(BF16)
        for d in range(2):
            a_d = a[:, d * w:(d + 1) * w]
            lw_ref[i, d] = logw[:, d * w:(d + 1) * w]
            kd_ref[i, d] = (k * (1.0 + (a_d - 1.0) * ka_ref[...])).astype(BF16)
            bb_ref[i, d] = (kk * a_d).astype(BF16)


def _rwkv_prep(rw, mu, w2cat, w0, a2cat, a0, g2, k_k, k_a, seg, n_lat_tiles):
    b, s, wseg = rw.shape
    n_tiles = s // ROW_TILE
    w = RWKV_WIDTH
    nb = TOKEN_BATCH
    hp, hn = _halo_specs(wseg, n_tiles, rw.dtype)
    full = lambda shape: pl.BlockSpec(shape, lambda i, j: (0,) * len(shape))
    tok = pl.BlockSpec((nb, ROW_TILE, w), lambda i, j: (i, j, 0))
    tok2 = pl.BlockSpec((nb, 2, ROW_TILE, w), lambda i, j: (i, 0, j, 0))
    one = jax.ShapeDtypeStruct((b, s, w), BF16)
    two = jax.ShapeDtypeStruct((b, 2, s, w), BF16)
    two_f32 = jax.ShapeDtypeStruct((b, 2, s, w), F32)
    return pl.pallas_call(
        functools.partial(_rwkv_prep_kernel, n_lat_tiles=n_lat_tiles, n_tiles=n_tiles),
        grid=(b // nb, n_tiles),
        in_specs=[pl.BlockSpec((nb, ROW_TILE, wseg), lambda i, j: (i, j, 0)), hp, hn,
                  full((1, wseg)), full(w2cat.shape), full((1, 2 * w)), full(a2cat.shape), full((1, 2 * w)),
                  full(g2.shape), full((1, w)), full((1, w)), full(seg.shape)],
        out_specs=[pl.BlockSpec((nb, ROW_TILE, 4 * w), lambda i, j: (i, j, 0)), tok, tok2, tok2, tok2],
        out_shape=[jax.ShapeDtypeStruct((b, s, 4 * w), BF16), one, two_f32, two, two],
        compiler_params=_params("parallel", "parallel"),
        name="rwkv_prep",
    )(rw, rw, rw, mu, w2cat, w0, a2cat, a0, g2, k_k, k_a, seg)


def _block_diag(y, head_masks):
    return jnp.concatenate([jnp.where(m, y, 0.0) for m in head_masks], axis=0)


def _scan_chain(sgn, r, v, nkk, lw, kd, bb, s0):
    row = lax.broadcasted_iota(jnp.int32, (CHUNK, QUAD), 0)
    lane = lax.broadcasted_iota(jnp.int32, (CHUNK, QUAD), 1)
    rel = ((lane % CHUNK) - row) * sgn
    strict = rel < 0
    incl = rel <= 0
    eye = jnp.where(rel == 0, 1.0, 0.0)
    head_masks = [(lane // HEAD_DIM) == h for h in range(4)]
    bd_mask = (lax.broadcasted_iota(jnp.int32, (QUAD, QUAD), 0) // HEAD_DIM
               == lax.broadcasted_iota(jnp.int32, (QUAD, QUAD), 1) // HEAD_DIM)
    t_row = lax.broadcasted_iota(jnp.int32, (CHUNK, CHUNK), 0)
    t_col = lax.broadcasted_iota(jnp.int32, (CHUNK, CHUNK), 1)
    tri = jnp.where((t_col - t_row) * sgn <= 0, 1.0, 0.0).astype(BF16)
    last_row = CHUNK - 1 if sgn > 0 else 0
    bd = lambda t: _block_diag(t, head_masks).astype(BF16)

    cum = _mm_exact_lhs(tri, lw)
    yield
    cum_end = cum[last_row:last_row + 1]
    w_inv = jnp.exp(-cum)
    w_rem = jnp.exp(cum_end - cum)
    ar = jnp.concatenate([nkk * jnp.exp(cum - lw), r * jnp.exp(cum)], axis=0).astype(BF16)
    xb = _dg(ar, bd(bb * w_inv), NT)
    xk = _dg(ar, bd(kd * w_inv), NT)
    ars = _dg(ar, s0.astype(BF16), NT)
    yield
    n_ab = jnp.where(strict, xb[:CHUNK], 0.0)
    l_ak = jnp.where(strict, xk[:CHUNK], 0.0)
    g_rb = jnp.where(incl, xb[CHUNK:], 0.0)
    g_rk = jnp.where(incl, xk[CHUNK:], 0.0)

    m = eye + n_ab
    p = _dg(n_ab.astype(BF16), bd(n_ab))
    lg = _dg(jnp.concatenate([l_ak, g_rk], axis=0).astype(BF16), bd(v))
    yield
    for _ in range(int(np.log2(CHUNK)) - 2):
        both = _dg(jnp.concatenate([p, m], axis=0).astype(BF16), bd(p))
        yield
        p = both[:CHUNK]
        m = m + both[CHUNK:]
    m = m + _dg(m.astype(BF16), bd(p))
    yield
    x0 = ars[:CHUNK] + lg[:CHUNK]
    u = _dg(m.astype(BF16), bd(x0))
    yield
    y = ars[CHUNK:] + _dg(g_rb.astype(BF16), bd(u)) + lg[CHUNK:]
    z = _dg(jnp.concatenate([u, v], axis=0).astype(BF16),
            jnp.concatenate([bb * w_rem, kd * w_rem], axis=0).astype(BF16), TN)
    yield
    return y, s0 * jnp.exp(cum_end) + jnp.where(bd_mask, z, 0.0)


def _run_interleaved(chains):
    results = [None] * len(chains)
    live = list(enumerate(chains))
    while live:
        still = []
        for idx, g in live:
            try:
                next(g)
                still.append((idx, g))
            except StopIteration as stop:
                results[idx] = stop.value
        live = still
    return results


def _rwkv_scan_kernel(pf_ref, nf_ref, lwf_ref, kdf_ref, bbf_ref, pb_ref, nb_ref, lwb_ref, kdb_ref, bbb_ref,
                      yf_ref, yb_ref, s_ref):
    @pl.when(pl.program_id(1) == 0)
    def _():
        s_ref[...] = jnp.zeros_like(s_ref)

    w = RWKV_WIDTH
    f32 = lambda t: t.astype(F32)
    dirs = ((1, pf_ref, nf_ref, lwf_ref, kdf_ref, bbf_ref, yf_ref),
            (-1, pb_ref, nb_ref, lwb_ref, kdb_ref, bbb_ref, yb_ref))
    keys = [(i, d, q) for i in range(SCAN_BATCH) for d in range(2) for q in range(w // QUAD)]
    state = {key: s_ref[key] for key in keys}
    for step in range(SCAN_CHUNKS):
        work = []
        for i, d, q in keys:
            sgn, p_ref, n_ref, lw_ref, kd_ref, bb_ref, y_ref = dirs[d]
            c = step if sgn > 0 else SCAN_CHUNKS - 1 - step
            rows = slice(c * CHUNK, (c + 1) * CHUNK)
            ql = slice(q * QUAD, (q + 1) * QUAD)
            vl = slice(2 * w + q * QUAD, 2 * w + (q + 1) * QUAD)
            args = (f32(p_ref[i, rows, ql]), f32(p_ref[i, rows, vl]), f32(n_ref[i, rows, ql]),
                    lw_ref[i, 0, rows, ql], f32(kd_ref[i, 0, rows, ql]), f32(bb_ref[i, 0, rows, ql]), state[(i, d, q)])
            work.append((sgn, args, y_ref, (i, rows, ql)))
        results = _run_interleaved([_scan_chain(sgn, *args) for sgn, args, _, _ in work])
        for key, (_, _, y_ref, y_idx), (y, s_new) in zip(keys, work, results):
            y_ref[y_idx] = y
            state[key] = s_new
    for key in keys:
        s_ref[key] = state[key]


def _rwkv_scan(rkvg, nkk, lw, kd, bb, n_lat):
    b, s, w = nkk.shape
    rows = SCAN_CHUNKS * CHUNK
    assert s % rows == 0 and n_lat % rows == 0 and b % SCAN_BATCH == 0
    nc = s // rows
    nc_lat = n_lat // rows
    nc_ctx = nc - nc_lat
    nb = SCAN_BATCH

    fwd = lambda c: jnp.where(c < nc_ctx, nc_lat + c, c - nc_ctx)
    bwd = lambda c: jnp.where(c < nc_ctx, nc - 1 - c, nc_lat - 1 - (c - nc_ctx))
    tok = lambda order, a: pl.BlockSpec((nb, rows, a.shape[2]), lambda i, c: (i, order(c), 0))
    tok2 = lambda order, d: pl.BlockSpec((nb, 1, rows, w), lambda i, c: (i, d, order(c), 0))
    out = jax.ShapeDtypeStruct((b, s, w), F32)
    return pl.pallas_call(
        _rwkv_scan_kernel,
        grid=(b // nb, nc),
        in_specs=[tok(fwd, rkvg), tok(fwd, nkk), tok2(fwd, 0), tok2(fwd, 0), tok2(fwd, 0),
                  tok(bwd, rkvg), tok(bwd, nkk), tok2(bwd, 1), tok2(bwd, 1), tok2(bwd, 1)],
        out_specs=[tok(fwd, nkk), tok(bwd, nkk)],
        out_shape=[out, out],
        scratch_shapes=[pltpu.VMEM((nb, 2, w // QUAD, QUAD, QUAD), F32)],
        compiler_params=_params("parallel", "arbitrary"),
        name="rwkv_scan",
    )(rkvg, nkk, lw, kd, bb, rkvg, nkk, lw, kd, bb)


def _rwkv_readout(y, r, k, v, g, r_k, gn_w, gn_b, seg):
    mean = _head_sum(y, seg) * (1.0 / HEAD_DIM)
    yc = y - mean
    var = _head_sum(yc * yc, seg) * (1.0 / HEAD_DIM)
    yn = yc * lax.rsqrt(var + RWKV_GN_EPS)
    bonus = _head_sum(r * k * r_k, seg) * v
    return (yn * gn_w + gn_b + bonus) * g


def _route(x, gain, shift, scale, w_router_t):
    h = _rms(x) * gain
    h = h * (1.0 + scale) + shift
    logits = _mm3(w_router_t, h, NT)
    e = jnp.exp(logits - jnp.max(logits, axis=0, keepdims=True))
    return h, e / jnp.sum(e, axis=0, keepdims=True)


def _merge_kernel(x_ref, oa_ref, yf_ref, yb_ref, rkvg_ref, cv_ref, hp_ref, hn_ref, gt_ref, mt_ref,
                  rk_ref, gw_ref, gb_ref, seg_ref, cw_ref, gain2_ref, wrt_ref, wa_ref, wr_ref, wc_ref, wo_ref,
                  o_ref, h_ref, aff_ref, *, n_lat_tiles, n_tiles):
    j = pl.program_id(1)
    first, last = _tile_ends(j, n_lat_tiles, n_tiles)
    cw = CONV_WIDTH
    d = x_ref.shape[2]
    w = RWKV_WIDTH
    for i in range(x_ref.shape[0]):
        r, k, v, g = [rkvg_ref[i, :, c * w:(c + 1) * w].astype(F32) for c in range(4)]
        o_rw = _rwkv_readout(yf_ref[i] + yb_ref[i], r, k, v, g, rk_ref[...], gw_ref[...], gb_ref[...], seg_ref[...])
        cv = cv_ref[i].astype(F32)
        hp = hp_ref[i].astype(F32)
        hn = hn_ref[i].astype(F32)
        z = cv[:, cw:2 * cw] * cv[:, 2 * cw:]
        zp = hp[:, cw:2 * cw] * hp[:, 2 * cw:]
        zn = hn[:, cw:2 * cw] * hn[:, 2 * cw:]
        z_prev, z_next = _neighbours(z, zp, zn, first, last)
        o_cv = cv[:, :cw] * (cw_ref[0:1] * z_prev + cw_ref[1:2] * z + cw_ref[2:3] * z_next)
        gt = gt_ref[i].astype(F32)
        m = (_sigmoid(gt[:, :d]) * _dg(oa_ref[i], wa_ref[0])
             + _sigmoid(gt[:, d:2 * d]) * _dg(o_rw.astype(BF16), wr_ref[0])
             + _sigmoid(gt[:, 2 * d:]) * _dg(o_cv.astype(BF16), wc_ref[0]))
        y = _dg(m.astype(BF16), wo_ref[0])
        x_new = x_ref[i] + mt_ref[i, 0, 2:3] * y
        o_ref[i] = x_new
        h, aff = _route(x_new, gain2_ref[...], mt_ref[i, 0, 3:4], mt_ref[i, 0, 4:5], wrt_ref[...])
        h_ref[i] = h.astype(BF16)
        aff_ref[i] = aff


def _merge(xc, o_att, y_f, y_b, rkvg, cv, gt, mt, r_k, gn_w, gn_b, seg, conv_w, gain2, w_router_t, layer, stacked,
           n_lat_tiles):
    b, s, d = xc.shape
    n_tiles = s // ROW_TILE
    ne = w_router_t.shape[0]
    nb = TOKEN_BATCH
    hp, hn = _halo_specs(cv.shape[2], n_tiles, cv.dtype)
    tok = lambda a: pl.BlockSpec((nb, ROW_TILE, a.shape[2]), lambda i, j: (i, j, 0))
    full = lambda a: pl.BlockSpec(a.shape, lambda i, j: (0,) * a.ndim)
    consts = (r_k, gn_w, gn_b, seg, conv_w, gain2, w_router_t)
    return pl.pallas_call(
        functools.partial(_merge_kernel, n_lat_tiles=n_lat_tiles, n_tiles=n_tiles),
        grid=(b // nb, n_tiles),
        in_specs=[tok(a) for a in (xc, o_att, y_f, y_b, rkvg, cv)] + [hp, hn, tok(gt),
                  pl.BlockSpec((nb, 1, 6, d), lambda i, j: (i, j // n_lat_tiles, 0, 0))]
                 + [full(a) for a in consts] + [_layer_spec(a, layer) for a in stacked],
        out_specs=[tok(xc), tok(xc), pl.BlockSpec((nb, ne, ROW_TILE), lambda i, j: (i, 0, j))],
        out_shape=[jax.ShapeDtypeStruct((b, s, d), F32), jax.ShapeDtypeStruct((b, s, d), BF16),
                   jax.ShapeDtypeStruct((b, ne, s), F32)],
        compiler_params=_params("parallel", "parallel"),
        name="merge",
    )(xc, o_att, y_f, y_b, rkvg, cv, cv, cv, gt, mt, *consts, *stacked)


def _select_kernel(aff_ref, pos_ref, gate_ref, *, cap):
    a = aff_ref[0]
    ne, n = a.shape
    bits = pltpu.bitcast(a, jnp.int32)

    def count(mask):
        return jnp.sum(jnp.where(mask, 1.0, 0.0), axis=1, keepdims=True)

    def body(_, carry):
        lo, hi = carry
        mid = lo + ((hi - lo + 1) >> 1)
        ok = count(bits >= mid) >= cap
        return jnp.where(ok, mid, lo), jnp.where(ok, hi, mid - 1)

    lo0 = jnp.zeros((ne, 1), jnp.int32)
    hi0 = jnp.full((ne, 1), 0x7F800000, jnp.int32)
    thr, _ = lax.fori_loop(0, 32, body, (lo0, hi0))
    gt = bits > thr
    eq = bits == thr
    need = cap - count(gt)

    def tokens_before(mask):
        m = jnp.where(mask, 1.0, 0.0).astype(BF16)
        blk = min(n, SELECT_BLOCK)
        cols = []
        for j in range(n // blk):
            s_idx = lax.broadcasted_iota(jnp.int32, (n, blk), 0)
            t_idx = lax.broadcasted_iota(jnp.int32, (n, blk), 1) + j * blk
            cols.append(_dg(m, jnp.where(s_idx < t_idx, 1.0, 0.0).astype(BF16)))
        return jnp.concatenate(cols, axis=1)

    sel = jnp.logical_or(gt, jnp.logical_and(eq, tokens_before(eq) < need))
    pos_ref[0] = jnp.where(sel, tokens_before(sel).astype(jnp.int32), -1)
    gate_ref[0] = jnp.where(sel, a, 0.0)


def _select(aff_t, tok0, n, cap):
    b, ne, _ = aff_t.shape
    blk = tok0 // n
    return pl.pallas_call(
        functools.partial(_select_kernel, cap=cap),
        grid=(b,),
        in_specs=[pl.BlockSpec((1, ne, n), lambda i: (i, 0, blk))],
        out_specs=[pl.BlockSpec((1, ne, n), lambda i: (i, 0, 0)), pl.BlockSpec((1, ne, n), lambda i: (i, 0, 0))],
        out_shape=[jax.ShapeDtypeStruct((b, ne, n), jnp.int32), jax.ShapeDtypeStruct((b, ne, n), F32)],
        compiler_params=_params("parallel"),
        name="moe_select",
    )(aff_t)


def _expert_kernel(h_ref, pos_ref, gate_ref, wg_ref, wu_ref, wd_ref, o_ref, *, cap):
    e = pl.program_id(1)

    @pl.when(e == 0)
    def _():
        o_ref[...] = jnp.zeros_like(o_ref)

    nb, n, _ = h_ref.shape
    slot = lax.broadcasted_iota(jnp.int32, (cap, n), 0)
    onehots, xes, gates = [], [], []
    for i in range(nb):
        hit = pos_ref[i, 0] == slot
        onehot = jnp.where(hit, 1.0, 0.0).astype(BF16)
        onehots.append(onehot)
        xes.append(_dg(onehot, h_ref[i]).astype(BF16))
        gates.append(jnp.sum(jnp.where(hit, gate_ref[i, 0], 0.0), axis=1, keepdims=True))
    xe = jnp.concatenate(xes, axis=0)
    hg = _dg(xe, wg_ref[0])
    hu = _dg(xe, wu_ref[0])
    hid = (hg * _sigmoid(hg) * hu).astype(BF16)
    ye = (_dg(hid, wd_ref[0]) * jnp.concatenate(gates, axis=0)).astype(BF16)
    for i in range(nb):
        o_ref[i] += _dg(onehots[i], ye[i * cap:(i + 1) * cap], TN)


def _experts(h, pos, gate, wg, wu, wd, layer, tok0, n, cap):
    b, s, d = h.shape
    ne = pos.shape[1]
    f = wg.shape[2]
    blk = tok0 // n
    nb = max(1, min(b // 2, EXPERT_ROWS // cap))
    assert b % nb == 0
    sel = pl.BlockSpec((nb, 1, 1, n), lambda i, e: (i, e, 0, 0))
    return pl.pallas_call(
        functools.partial(_expert_kernel, cap=cap),
        grid=(b // nb, ne),
        in_specs=[pl.BlockSpec((nb, n, d), lambda i, e: (i, blk, 0)), sel, sel,
                  pl.BlockSpec((1, d, f), lambda i, e: (layer * ne + e, 0, 0)),
                  pl.BlockSpec((1, d, f), lambda i, e: (layer * ne + e, 0, 0)),
                  pl.BlockSpec((1, f, d), lambda i, e: (layer * ne + e, 0, 0))],
        out_specs=pl.BlockSpec((nb, n, d), lambda i, e: (i, 0, 0)),
        out_shape=jax.ShapeDtypeStruct((b, n, d), F32),
        compiler_params=_params("parallel", "arbitrary"),
        name="moe_experts",
    )(h, pos.reshape(b, ne, 1, n), gate.reshape(b, ne, 1, n), wg, wu, wd)


def _residual_kernel(x_ref, m_ref, mt_ref, o_ref):
    o_ref[0] = x_ref[0] + mt_ref[0, 0, 5:6] * m_ref[0]


def _moe_residual(xc, moe, mt, tok0, stream):
    b, s, d = xc.shape
    n = moe.shape[1]
    t0 = tok0 // ROW_TILE
    return pl.pallas_call(
        _residual_kernel,
        grid=(b, n // ROW_TILE),
        in_specs=[pl.BlockSpec((1, ROW_TILE, d), lambda i, j: (i, t0 + j, 0)),
                  pl.BlockSpec((1, ROW_TILE, d), lambda i, j: (i, j, 0)),
                  pl.BlockSpec((1, 1, 6, d), lambda i, j: (i, stream, 0, 0))],
        out_specs=pl.BlockSpec((1, ROW_TILE, d), lambda i, j: (i, t0 + j, 0)),
        out_shape=jax.ShapeDtypeStruct((b, s, d), F32),
        input_output_aliases={0: 0},
        compiler_params=_params("parallel", "parallel"),
        name="moe_residual",
    )(xc, moe, mt)


def _final_norm_kernel(x_ref, g_ref, o_ref):
    o_ref[0] = _rms(x_ref[0]) * g_ref[...]


def _final_norm(xc, gain, n_lat):
    b, s, d = xc.shape
    return pl.pallas_call(
        _final_norm_kernel,
        grid=(b, n_lat // ROW_TILE),
        in_specs=[pl.BlockSpec((1, ROW_TILE, d), lambda i, j: (i, j, 0)),
                  pl.BlockSpec((1, d), lambda i, j: (0, 0))],
        out_specs=pl.BlockSpec((1, ROW_TILE, d), lambda i, j: (i, j, 0)),
        out_shape=jax.ShapeDtypeStruct((b, n_lat, d), F32),
        compiler_params=_params("parallel", "parallel"),
        name="final_norm",
    )(xc, gain)


def _rope_tables(n_lat, n_ctx):
    rows = n_lat // GRID_W
    row = jnp.repeat(jnp.arange(rows, dtype=F32), GRID_W)
    col = jnp.tile(jnp.arange(GRID_W, dtype=F32), rows)
    axis_dim = HEAD_DIM // 2
    inv_freq = ROPE_THETA ** (-jnp.arange(0, axis_dim, 2, dtype=F32) / axis_dim)
    ang_r = row[:, None] * inv_freq[None, :]
    ang_c = col[:, None] * inv_freq[None, :]
    cos_h = jnp.concatenate([jnp.cos(ang_r), jnp.cos(ang_r), jnp.cos(ang_c), jnp.cos(ang_c)], axis=1)
    sin_h = jnp.concatenate([-jnp.sin(ang_r), jnp.sin(ang_r), -jnp.sin(ang_c), jnp.sin(ang_c)], axis=1)
    cos_h = jnp.concatenate([cos_h, jnp.ones((n_ctx, HEAD_DIM), F32)], axis=0)
    sin_h = jnp.concatenate([sin_h, jnp.zeros((n_ctx, HEAD_DIM), F32)], axis=0)
    return jnp.tile(cos_h, (1, ATT_HEADS)), jnp.tile(sin_h, (1, ATT_HEADS))


def _head_sum_matrix(width):
    idx = np.arange(width) // HEAD_DIM
    return jnp.asarray(idx[:, None] == idx[None, :], dtype=BF16)


def _two_dir_lowrank(w2):
    _, rank, w = w2.shape
    z = jnp.zeros((rank, w), w2.dtype)
    return jnp.concatenate([jnp.concatenate([w2[0], z], axis=1), jnp.concatenate([z, w2[1]], axis=1)], axis=0)


def kernel(x, c, ctx, c_ctx, ada_w, ada_b, norm1, w_in, q_gain, k_gain, shift_mu, decay_w0, decay_w2, iclr_a0, iclr_a2, gate_g2, rwkv_kk, rwkv_ka, rwkv_rk, rwkv_gn_w, rwkv_gn_b, conv_w, w_br_att, w_br_rwkv, w_br_conv, w_out, norm2, w_router, exp_gate, exp_up, exp_down, final_norm):
    b, n_lat, d = x.shape
    n_ctx = ctx.shape[1]
    depth = ada_w.shape[0]
    assert n_lat % ROW_TILE == 0 and n_ctx % ROW_TILE == 0 and n_lat % n_ctx == 0
    n_lat_tiles = n_lat // ROW_TILE

    pad = (-(b + 1)) % 8
    cc = jnp.concatenate([c, c_ctx[None, :], jnp.zeros((pad, d), F32)], axis=0)
    mods = _ada_table(cc, ada_w, ada_b)

    cos_t, sin_t = _rope_tables(n_lat, n_ctx)
    seg = _head_sum_matrix(ATT_WIDTH)
    att_w = ATT_WIDTH + 2 * ATT_KV_WIDTH
    cv_w = 3 * CONV_WIDTH
    offs = np.cumsum([0, att_w, RWKV_SEG, cv_w, 3 * d])

    w_groups = [w_in[:, :, offs[i]:offs[i + 1]].astype(BF16) for i in range(4)]
    w_tail = [t.astype(BF16) for t in (w_br_att, w_br_rwkv, w_br_conv, w_out)]
    ne, f = exp_gate.shape[1], exp_gate.shape[3]
    wg = exp_gate.astype(BF16).reshape(depth * ne, d, f)
    wu = exp_up.astype(BF16).reshape(depth * ne, d, f)
    wd = exp_down.astype(BF16).reshape(depth * ne, f, d)

    xc = jnp.concatenate([x, ctx], axis=1)
    for l in range(depth):
        mod_lat = mods[l, :b].reshape(b, 1, 6, d)
        mod_ctx = jnp.broadcast_to(mods[l, b].reshape(1, 1, 6, d), (b, 1, 6, d))
        mt = jnp.concatenate([mod_lat, mod_ctx], axis=1)

        gain1 = norm1[l].reshape(1, d)
        qvk, rw, cv, gt = _in_proj(
            xc, gain1, mt, cos_t, sin_t, jnp.tile(q_gain[l], ATT_HEADS).reshape(1, -1),
            jnp.tile(k_gain[l], ATT_KV_HEADS).reshape(1, -1), seg, l, w_groups, n_lat_tiles)
        o_att = _attention(qvk, n_lat)

        rkvg, nkk, lw, kd, bb = _rwkv_prep(
            rw, shift_mu[l].reshape(1, -1), _two_dir_lowrank(decay_w2[l]), decay_w0[l].reshape(1, -1),
            _two_dir_lowrank(iclr_a2[l]), iclr_a0[l].reshape(1, -1), gate_g2[l],
            rwkv_kk[l].reshape(1, -1), rwkv_ka[l].reshape(1, -1), seg, n_lat_tiles)
        y_f, y_b = _rwkv_scan(rkvg, nkk, lw, kd, bb, n_lat)

        xc, h2, aff_t = _merge(
            xc, o_att, y_f, y_b, rkvg, cv, gt, mt, rwkv_rk[l].reshape(1, -1), rwkv_gn_w[l].reshape(1, -1),
            rwkv_gn_b[l].reshape(1, -1), seg, conv_w[l], norm2[l].reshape(1, d), w_router[l].T, l, w_tail,
            n_lat_tiles)
        streams = [(0, n_lat, 0)] + ([(n_lat, n_ctx, 1)] if l < depth - 1 else [])
        for tok0, n, stream in streams:
            cap = CAPACITY_FACTOR * n // N_EXPERTS
            pos, gate = _select(aff_t, tok0, n, cap)
            moe = _experts(h2, pos, gate, wg, wu, wd, l, tok0, n, cap)
            xc = _moe_residual(xc, moe, mt, tok0, stream)
    return _final_norm(xc, final_norm.reshape(1, d), n_lat)
```

```python
import functools

import numpy as np
import jax
import jax.numpy as jnp
from jax import lax
from jax.experimental import pallas as pl
from jax.experimental.pallas import tpu as pltpu

F32 = jnp.float32
BF16 = jnp.bfloat16

GRID_W = 64
NORM_EPS = 1e-6
LOG2_E = float(np.log2(np.e))
ATT_HEADS = 8
ATT_KV_HEADS = 2
HEAD_DIM = 64
ATT_GROUP = ATT_HEADS // ATT_KV_HEADS
ATT_WIDTH = ATT_HEADS * HEAD_DIM
ATT_KV_WIDTH = ATT_KV_HEADS * HEAD_DIM
ROPE_THETA = 10000.0
RWKV_HEADS = 8
RWKV_WIDTH = RWKV_HEADS * HEAD_DIM
DECAY_RANK = 64
ICLR_RANK = 64
GATE_RANK = 128
RWKV_GN_EPS = 64e-5
RWKV_SEG = 3 * RWKV_WIDTH + 2 * DECAY_RANK + 2 * ICLR_RANK + GATE_RANK
CONV_WIDTH = 512
N_EXPERTS = 16
CAPACITY_FACTOR = 2

ROW_TILE = 256
CHUNK = 64
QUAD = 4 * HEAD_DIM
SCAN_BATCH = 4
SCAN_CHUNKS = 2
TOKEN_BATCH = 2
HALO = 8
SELECT_BLOCK = 256
EXPERT_ROWS = 256
VMEM_LIMIT = 56 * 1024 * 1024

NT = (((1,), (1,)), ((), ()))
TN = (((0,), (0,)), ((), ()))
NN = (((1,), (0,)), ((), ()))


def _params(*sem):
    return pltpu.CompilerParams(dimension_semantics=sem, vmem_limit_bytes=VMEM_LIMIT)


def _layer_spec(a, layer):
    return pl.BlockSpec((1,) + a.shape[1:], lambda *_: (layer,) + (0,) * (a.ndim - 1))


def _dg(a, b, dn=NN):
    return lax.dot_general(a, b, dn, preferred_element_type=F32)


def _split2(x):
    hi = x.astype(BF16)
    lo = (x - hi.astype(F32)).astype(BF16)
    return hi, lo


def _mm1(a, b, dn=NN):
    return _dg(a.astype(BF16), b.astype(BF16), dn)


def _mm3(a, b, dn=NN):
    ah, al = _split2(a)
    bh, bl = _split2(b)
    return _dg(ah, bh, dn) + (_dg(ah, bl, dn) + _dg(al, bh, dn))


def _mm_exact_lhs(a_bf16, b, dn=NN):
    hi, lo = _split2(b)
    return _dg(a_bf16, hi, dn) + _dg(a_bf16, lo, dn)


def _head_sum(x, seg):
    return _dg(x.astype(BF16), seg)


def _sigmoid(x):
    return 1.0 / (1.0 + jnp.exp(-x))


def _rms(x):
    return x * lax.rsqrt(jnp.mean(x * x, axis=-1, keepdims=True) + NORM_EPS)


def _ada_kernel(c_ref, w_ref, b_ref, o_ref):
    c = c_ref[...]
    s = c * _sigmoid(c)
    o_ref[0] = _mm3(s, w_ref[0]) + b_ref[0]


def _ada_table(cc, ada_w, ada_b):
    n_layers, d, six_d = ada_w.shape
    rows = cc.shape[0]
    tn = six_d // 4
    return pl.pallas_call(
        _ada_kernel,
        grid=(n_layers, six_d // tn),
        in_specs=[pl.BlockSpec((rows, d), lambda l, j: (0, 0)),
                  pl.BlockSpec((1, d, tn), lambda l, j: (l, 0, j)),
                  pl.BlockSpec((1, 1, tn), lambda l, j: (l, 0, j))],
        out_specs=pl.BlockSpec((1, rows, tn), lambda l, j: (l, 0, j)),
        out_shape=jax.ShapeDtypeStruct((n_layers, rows, six_d), F32),
        compiler_params=_params("parallel", "parallel"),
        name="ada_table",
    )(cc, ada_w, ada_b.reshape(n_layers, 1, six_d))


def _swap16(x):
    w = x.shape[-1]
    lane = lax.broadcasted_iota(jnp.int32, x.shape, x.ndim - 1)
    return jnp.where((lane & 16) == 0, pltpu.roll(x, w - 16, x.ndim - 1), pltpu.roll(x, 16, x.ndim - 1))


def _att_heads(a, cos, sin, q_gain, k_gain, seg):
    def norm_rope(u, gain, cos_u, sin_u, seg_u):
        ms = _head_sum(u * u, seg_u) * (1.0 / HEAD_DIM)
        un = u * lax.rsqrt(ms + NORM_EPS) * gain
        return un * cos_u + _swap16(un) * sin_u

    kw = ATT_KV_WIDTH
    q = norm_rope(a[:, :ATT_WIDTH], q_gain, cos, sin, seg)
    k = norm_rope(a[:, ATT_WIDTH:ATT_WIDTH + kw], k_gain, cos[:, :kw], sin[:, :kw], seg[:kw, :kw])
    q = (q * (HEAD_DIM ** -0.5 * LOG2_E)).astype(BF16)
    ones = jnp.ones((a.shape[0], HEAD_DIM), BF16)
    v = a[:, ATT_WIDTH + kw:].astype(BF16)
    v1 = jnp.concatenate([t for h in range(ATT_KV_HEADS) for t in (v[:, h * HEAD_DIM:(h + 1) * HEAD_DIM], ones)],
                         axis=1)
    return q, k.astype(BF16), v1


def _in_proj_kernel(x_ref, gain_ref, mt_ref, cos_ref, sin_ref, qg_ref, kg_ref, seg_ref, wa_ref, wr_ref, wc_ref, wg_ref,
                    qvk_ref, rw_ref, cv_ref, gt_ref):
    for i in range(x_ref.shape[0]):
        h = _rms(x_ref[i]) * gain_ref[...]
        h = (h * (1.0 + mt_ref[i, 0, 1:2]) + mt_ref[i, 0, 0:1]).astype(BF16)
        q, k, v1 = _att_heads(_dg(h, wa_ref[0]), cos_ref[...], sin_ref[...], qg_ref[...], kg_ref[...], seg_ref[...])
        qvk_ref[i] = jnp.concatenate([q, v1, k], axis=1)
        rw_ref[i] = _dg(h, wr_ref[0]).astype(BF16)
        cv_ref[i] = _dg(h, wc_ref[0]).astype(BF16)
        gt_ref[i] = _dg(h, wg_ref[0]).astype(BF16)


def _in_proj(xc, gain, mt, cos_t, sin_t, q_gain_t, k_gain_t, seg, layer, weights, n_lat_tiles):
    b, s, d = xc.shape
    nb = TOKEN_BATCH
    tok = lambda n: pl.BlockSpec((nb, ROW_TILE, n), lambda i, j: (i, j, 0))
    full = lambda a: pl.BlockSpec(a.shape, lambda i, j: (0,) * a.ndim)
    table = pl.BlockSpec((ROW_TILE, ATT_WIDTH), lambda i, j: (j, 0))
    widths = [ATT_WIDTH + 3 * ATT_KV_WIDTH] + [w.shape[2] for w in weights[1:]]
    return pl.pallas_call(
        _in_proj_kernel,
        grid=(b // nb, s // ROW_TILE),
        in_specs=[tok(d), full(gain), pl.BlockSpec((nb, 1, 6, d), lambda i, j: (i, j // n_lat_tiles, 0, 0)),
                  table, table, full(q_gain_t), full(k_gain_t), full(seg)]
                 + [_layer_spec(w, layer) for w in weights],
        out_specs=[tok(n) for n in widths],
        out_shape=[jax.ShapeDtypeStruct((b, s, n), BF16) for n in widths],
        compiler_params=_params("parallel", "parallel"),
        name="in_proj",
    )(xc, gain, mt, cos_t, sin_t, q_gain_t, k_gain_t, seg, *weights)


def _attn_kernel(q_ref, k_ref, v_ref, o_ref, *, n_lat, n_lat_tiles):
    j = pl.program_id(1)
    s_all = k_ref.shape[1]

    def run(k0, k1):
        def scores(h):
            kvh = h // ATT_GROUP
            k = k_ref[0, k0:k1, kvh * HEAD_DIM:(kvh + 1) * HEAD_DIM]
            return _dg(q_ref[0, :, h * HEAD_DIM:(h + 1) * HEAD_DIM], k, NT)

        s_next = scores(0)
        for h in range(ATT_HEADS):
            s = s_next
            if h + 1 < ATT_HEADS:
                s_next = scores(h + 1)
            kvh = h // ATT_GROUP
            v1 = v_ref[0, k0:k1, 2 * kvh * HEAD_DIM:2 * (kvh + 1) * HEAD_DIM]
            p = jnp.exp2((s - jnp.max(s, axis=-1, keepdims=True)).astype(BF16))
            ov = _dg(p, v1)
            o = ov[:, :HEAD_DIM] / ov[:, HEAD_DIM:]
            o_ref[0, :, h * HEAD_DIM:(h + 1) * HEAD_DIM] = o.astype(BF16)

    @pl.when(j < n_lat_tiles)
    def _():
        run(0, s_all)

    @pl.when(j >= n_lat_tiles)
    def _():
        run(n_lat, s_all)


def _attention(qvk, n_lat):
    b, s, _ = qvk.shape
    v_w = 2 * ATT_KV_WIDTH
    return pl.pallas_call(
        functools.partial(_attn_kernel, n_lat=n_lat, n_lat_tiles=n_lat // ROW_TILE),
        grid=(b, s // ROW_TILE),
        in_specs=[pl.BlockSpec((1, ROW_TILE, ATT_WIDTH), lambda i, j: (i, j, 0)),
                  pl.BlockSpec((1, s, ATT_KV_WIDTH), lambda i, j: (i, 0, (ATT_WIDTH + v_w) // ATT_KV_WIDTH)),
                  pl.BlockSpec((1, s, v_w), lambda i, j: (i, 0, ATT_WIDTH // v_w))],
        out_specs=pl.BlockSpec((1, ROW_TILE, ATT_WIDTH), lambda i, j: (i, j, 0)),
        out_shape=jax.ShapeDtypeStruct((b, s, ATT_WIDTH), BF16),
        compiler_params=_params("parallel", "parallel"),
        name="attention",
    )(qvk, qvk, qvk)


def _neighbours(x, halo_prev, halo_next, first, last):
    rows = x.shape[0]
    ridx = lax.broadcasted_iota(jnp.int32, x.shape, 0)
    row_p = jnp.where(first, 0.0, halo_prev[halo_prev.shape[0] - 1:])
    row_n = jnp.where(last, 0.0, halo_next[0:1])
    prev = jnp.where(ridx == 0, row_p, pltpu.roll(x, 1, 0))
    nxt = jnp.where(ridx == rows - 1, row_n, pltpu.roll(x, rows - 1, 0))
    return prev, nxt


def _tile_ends(j, n_lat_tiles, n_tiles):
    first = jnp.logical_or(j == 0, j == n_lat_tiles)
    last = jnp.logical_or(j == n_lat_tiles - 1, j == n_tiles - 1)
    return first, last


def _halo_specs(width, n_tiles, dtype):
    halo = HALO * (4 // jnp.dtype(dtype).itemsize)
    per = ROW_TILE // halo
    nb = TOKEN_BATCH
    prev = pl.BlockSpec((nb, halo, width), lambda i, j: (i, jnp.maximum(j * per - 1, 0), 0))
    nxt = pl.BlockSpec((nb, halo, width), lambda i, j: (i, jnp.minimum((j + 1) * per, n_tiles * per - 1), 0))
    return prev, nxt


def _rwkv_prep_kernel(x_ref, hp_ref, hn_ref, mu_ref, w2_ref, w0_ref, a2_ref, a0_ref, g2_ref, kk_ref, ka_ref,
                      seg_ref, rv_ref, kg_ref, nkk_ref, lw_ref, kd_ref, bb_ref, *, n_lat_tiles, n_tiles):
    j = pl.program_id(1)
    first, last = _tile_ends(j, n_lat_tiles, n_tiles)
    w = RWKV_WIDTH
    for i in range(x_ref.shape[0]):
        x = x_ref[i].astype(F32)
        prev, nxt = _neighbours(x, hp_ref[i].astype(F32), hn_ref[i].astype(F32), first, last)
        xs = x + mu_ref[...] * (0.5 * (prev + nxt) - x)
        r = xs[:, 0:w]
        k = xs[:, w:2 * w]
        v = xs[:, 2 * w:3 * w]
        o = 3 * w
        w_lo = xs[:, o:o + 2 * DECAY_RANK]
        a_lo = xs[:, o + 2 * DECAY_RANK:o + 2 * DECAY_RANK + 2 * ICLR_RANK]
        g_lo = xs[:, o + 2 * DECAY_RANK + 2 * ICLR_RANK:]
        w_pre = w0_ref[...] + _mm1(jnp.tanh(w_lo), w2_ref[...])
        logw = -_sigmoid(w_pre) * float(np.exp(-0.5))
        a = _sigmoid(a0_ref[...] + _mm1(a_lo, a2_ref[...]))
        g = _mm1(_sigmoid(g_lo), g2_ref[...])
        kk = k * kk_ref[...]
        kk = kk * lax.rsqrt(_head_sum(kk * kk, seg_ref[...]) + 1e-12)
        for c, (t_scan, t_tail) in enumerate(((r, k), (v, g))):
            rv_ref[i, :, c * w:(c + 1) * w] = t_scan.astype(BF16)
            kg_ref[i, :, c * w:(c + 1) * w] = t_tail.astype(BF16)
        nkk_ref[i] = (-kk).astype(BF16)
        for d in range(2):
            a_d = a[:, d * w:(d + 1) * w]
            lw_ref[i, d] = logw[:, d * w:(d + 1) * w]
            kd_ref[i, d] = (k * (1.0 + (a_d - 1.0) * ka_ref[...])).astype(BF16)
            bb_ref[i, d] = (kk * a_d).astype(BF16)


def _rwkv_prep(rw, mu, w2cat, w0, a2cat, a0, g2, k_k, k_a, seg, n_lat_tiles):
    b, s, wseg = rw.shape
    n_tiles = s // ROW_TILE
    w = RWKV_WIDTH
    nb = TOKEN_BATCH
    hp, hn = _halo_specs(wseg, n_tiles, rw.dtype)
    full = lambda shape: pl.BlockSpec(shape, lambda i, j: (0,) * len(shape))
    tok = pl.BlockSpec((nb, ROW_TILE, w), lambda i, j: (i, j, 0))
    tok2 = pl.BlockSpec((nb, 2, ROW_TILE, w), lambda i, j: (i, 0, j, 0))
    pair = pl.BlockSpec((nb, ROW_TILE, 2 * w), lambda i, j: (i, j, 0))
    one = jax.ShapeDtypeStruct((b, s, w), BF16)
    two = jax.ShapeDtypeStruct((b, 2, s, w), BF16)
    two_f32 = jax.ShapeDtypeStruct((b, 2, s, w), F32)
    return pl.pallas_call(
        functools.partial(_rwkv_prep_kernel, n_lat_tiles=n_lat_tiles, n_tiles=n_tiles),
        grid=(b // nb, n_tiles),
        in_specs=[pl.BlockSpec((nb, ROW_TILE, wseg), lambda i, j: (i, j, 0)), hp, hn,
                  full((1, wseg)), full(w2cat.shape), full((1, 2 * w)), full(a2cat.shape), full((1, 2 * w)),
                  full(g2.shape), full((1, w)), full((1, w)), full(seg.shape)],
        out_specs=[pair, pair, tok, tok2, tok2, tok2],
        out_shape=[jax.ShapeDtypeStruct((b, s, 2 * w), BF16)] * 2 + [one, two_f32, two, two],
        compiler_params=_params("parallel", "parallel"),
        name="rwkv_prep",
    )(rw, rw, rw, mu, w2cat, w0, a2cat, a0, g2, k_k, k_a, seg)


def _block_diag(y, head_masks):
    return jnp.concatenate([jnp.where(m, y, 0.0) for m in head_masks], axis=0)


def _scan_chain(sgn, r, v, nkk, lw, kd, bb, s0):
    row = lax.broadcasted_iota(jnp.int32, (CHUNK, QUAD), 0)
    lane = lax.broadcasted_iota(jnp.int32, (CHUNK, QUAD), 1)
    rel = ((lane % CHUNK) - row) * sgn
    strict = rel < 0
    incl = rel <= 0
    eye = jnp.where(rel == 0, 1.0, 0.0)
    head_masks = [(lane // HEAD_DIM) == h for h in range(4)]
    bd_mask = (lax.broadcasted_iota(jnp.int32, (QUAD, QUAD), 0) // HEAD_DIM
               == lax.broadcasted_iota(jnp.int32, (QUAD, QUAD), 1) // HEAD_DIM)
    t_row = lax.broadcasted_iota(jnp.int32, (CHUNK, CHUNK), 0)
    t_col = lax.broadcasted_iota(jnp.int32, (CHUNK, CHUNK), 1)
    tri = jnp.where((t_col - t_row) * sgn <= 0, 1.0, 0.0).astype(BF16)
    last_row = CHUNK - 1 if sgn > 0 else 0
    bd = lambda t: _block_diag(t, head_masks).astype(BF16)

    cum = _mm_exact_lhs(tri, lw)
    yield
    cum_end = cum[last_row:last_row + 1]
    w_inv = jnp.exp(-cum)
    w_rem = jnp.exp(cum_end - cum)
    ar = jnp.concatenate([nkk * jnp.exp(cum - lw), r * jnp.exp(cum)], axis=0).astype(BF16)
    xb = _dg(ar, bd(bb * w_inv), NT)
    xk = _dg(ar, bd(kd * w_inv), NT)
    ars = _dg(ar, s0.astype(BF16), NT)
    yield
    n_ab = jnp.where(strict, xb[:CHUNK], 0.0)
    l_ak = jnp.where(strict, xk[:CHUNK], 0.0)
    g_rb = jnp.where(incl, xb[CHUNK:], 0.0)
    g_rk = jnp.where(incl, xk[CHUNK:], 0.0)

    m = eye + n_ab
    p = _dg(n_ab.astype(BF16), bd(n_ab))
    lg = _dg(jnp.concatenate([l_ak, g_rk], axis=0).astype(BF16), bd(v))
    yield
    for _ in range(int(np.log2(CHUNK)) - 2):
        both = _dg(jnp.concatenate([p, m], axis=0).astype(BF16), bd(p))
        yield
        p = both[:CHUNK]
        m = m + both[CHUNK:]
    m = m + _dg(m.astype(BF16), bd(p))
    yield
    x0 = ars[:CHUNK] + lg[:CHUNK]
    u = _dg(m.astype(BF16), bd(x0))
    yield
    y = ars[CHUNK:] + _dg(g_rb.astype(BF16), bd(u)) + lg[CHUNK:]
    z = _dg(jnp.concatenate([u, v], axis=0).astype(BF16),
            jnp.concatenate([bb * w_rem, kd * w_rem], axis=0).astype(BF16), TN)
    yield
    return y, s0 * jnp.exp(cum_end) + jnp.where(bd_mask, z, 0.0)


def _run_interleaved(chains):
    results = [None] * len(chains)
    live = list(enumerate(chains))
    while live:
        still = []
        for idx, g in live:
            try:
                next(g)
                still.append((idx, g))
            except StopIteration as stop:
                results[idx] = stop.value
        live = still
    return results


def _rwkv_scan_kernel(pf_ref, nf_ref, lwf_ref, kdf_ref, bbf_ref, pb_ref, nb_ref, lwb_ref, kdb_ref, bbb_ref,
                      yf_ref, yb_ref, s_ref):
    @pl.when(pl.program_id(1) == 0)
    def _():
        s_ref[...] = jnp.zeros_like(s_ref)

    w = RWKV_WIDTH
    f32 = lambda t: t.astype(F32)
    dirs = ((1, pf_ref, nf_ref, lwf_ref, kdf_ref, bbf_ref, yf_ref),
            (-1, pb_ref, nb_ref, lwb_ref, kdb_ref, bbb_ref, yb_ref))
    keys = [(i, d, q) for i in range(SCAN_BATCH) for d in range(2) for q in range(w // QUAD)]
    state = {key: s_ref[key] for key in keys}
    for step in range(SCAN_CHUNKS):
        work = []
        for i, d, q in keys:
            sgn, p_ref, n_ref, lw_ref, kd_ref, bb_ref, y_ref = dirs[d]
            c = step if sgn > 0 else SCAN_CHUNKS - 1 - step
            rows = slice(c * CHUNK, (c + 1) * CHUNK)
            ql = slice(q * QUAD, (q + 1) * QUAD)
            vl = slice(w + q * QUAD, w + (q + 1) * QUAD)
            args = (f32(p_ref[i, rows, ql]), f32(p_ref[i, rows, vl]), f32(n_ref[i, rows, ql]),
                    lw_ref[i, 0, rows, ql], f32(kd_ref[i, 0, rows, ql]), f32(bb_ref[i, 0, rows, ql]), state[(i, d, q)])
            work.append((sgn, args, y_ref, (i, rows, ql)))
        results = _run_interleaved([_scan_chain(sgn, *args) for sgn, args, _, _ in work])
        for key, (_, _, y_ref, y_idx), (y, s_new) in zip(keys, work, results):
            y_ref[y_idx] = y.astype(y_ref.dtype)
            state[key] = s_new
    for key in keys:
        s_ref[key] = state[key]


def _rwkv_scan(rv, nkk, lw, kd, bb, n_lat):
    b, s, w = nkk.shape
    rows = SCAN_CHUNKS * CHUNK
    assert s % rows == 0 and n_lat % rows == 0 and b % SCAN_BATCH == 0
    nc = s // rows
    nc_lat = n_lat // rows
    nc_ctx = nc - nc_lat
    nb = SCAN_BATCH

    fwd = lambda c: jnp.where(c < nc_ctx, nc_lat + c, c - nc_ctx)
    bwd = lambda c: jnp.where(c < nc_ctx, nc - 1 - c, nc_lat - 1 - (c - nc_ctx))
    tok = lambda order, a: pl.BlockSpec((nb, rows, a.shape[2]), lambda i, c: (i, order(c), 0))
    tok2 = lambda order, d: pl.BlockSpec((nb, 1, rows, w), lambda i, c: (i, d, order(c), 0))
    out = jax.ShapeDtypeStruct((b, s, w), BF16)
    return pl.pallas_call(
        _rwkv_scan_kernel,
        grid=(b // nb, nc),
        in_specs=[tok(fwd, rv), tok(fwd, nkk), tok2(fwd, 0), tok2(fwd, 0), tok2(fwd, 0),
                  tok(bwd, rv), tok(bwd, nkk), tok2(bwd, 1), tok2(bwd, 1), tok2(bwd, 1)],
        out_specs=[tok(fwd, nkk), tok(bwd, nkk)],
        out_shape=[out, out],
        scratch_shapes=[pltpu.VMEM((nb, 2, w // QUAD, QUAD, QUAD), F32)],
        compiler_params=_params("parallel", "arbitrary"),
        name="rwkv_scan",
    )(rv, nkk, lw, kd, bb, rv, nkk, lw, kd, bb)


def _rwkv_readout(y, r, k, v, g, r_k, gn_w, gn_b, seg):
    mean = _head_sum(y, seg) * (1.0 / HEAD_DIM)
    yc = y - mean
    var = _head_sum(yc * yc, seg) * (1.0 / HEAD_DIM)
    yn = yc * lax.rsqrt(var + RWKV_GN_EPS)
    bonus = _head_sum(r * k * r_k, seg) * v
    return (yn * gn_w + gn_b + bonus) * g


def _route(x, gain, shift, scale, w_router_t):
    h = _rms(x) * gain
    h = h * (1.0 + scale) + shift
    logits = _mm3(w_router_t, h, NT)
    e = jnp.exp(logits - jnp.max(logits, axis=0, keepdims=True))
    return h, e / jnp.sum(e, axis=0, keepdims=True)


def _merge_kernel(x_ref, oa_ref, yf_ref, yb_ref, rv_ref, kg_ref, cv_ref, hp_ref, hn_ref, gt_ref, mt_ref,
                  rk_ref, gw_ref, gb_ref, seg_ref, cw_ref, gain2_ref, wrt_ref, wa_ref, wr_ref, wc_ref, wo_ref,
                  o_ref, h_ref, aff_ref, *, n_lat_tiles, n_tiles):
    j = pl.program_id(1)
    first, last = _tile_ends(j, n_lat_tiles, n_tiles)
    cw = CONV_WIDTH
    d = x_ref.shape[2]
    w = RWKV_WIDTH
    for i in range(x_ref.shape[0]):
        r, v = [rv_ref[i, :, c * w:(c + 1) * w].astype(F32) for c in range(2)]
        k, g = [kg_ref[i, :, c * w:(c + 1) * w].astype(F32) for c in range(2)]
        o_rw = _rwkv_readout(yf_ref[i].astype(F32) + yb_ref[i].astype(F32), r, k, v, g, rk_ref[...], gw_ref[...], gb_ref[...], seg_ref[...])
        cv = cv_ref[i].astype(F32)
        hp = hp_ref[i].astype(F32)
        hn = hn_ref[i].astype(F32)
        z = cv[:, cw:2 * cw] * cv[:, 2 * cw:]
        zp = hp[:, cw:2 * cw] * hp[:, 2 * cw:]
        zn = hn[:, cw:2 * cw] * hn[:, 2 * cw:]
        z_prev, z_next = _neighbours(z, zp, zn, first, last)
        o_cv = cv[:, :cw] * (cw_ref[0:1] * z_prev + cw_ref[1:2] * z + cw_ref[2:3] * z_next)
        gt = gt_ref[i].astype(F32)
        m = (_sigmoid(gt[:, :d]) * _dg(oa_ref[i], wa_ref[0])
             + _sigmoid(gt[:, d:2 * d]) * _dg(o_rw.astype(BF16), wr_ref[0])
             + _sigmoid(gt[:, 2 * d:]) * _dg(o_cv.astype(BF16), wc_ref[0]))
        y = _dg(m.astype(BF16), wo_ref[0])
        x_new = x_ref[i] + mt_ref[i, 0, 2:3] * y
        o_ref[i] = x_new
        h, aff = _route(x_new, gain2_ref[...], mt_ref[i, 0, 3:4], mt_ref[i, 0, 4:5], wrt_ref[...])
        h_ref[i] = h.astype(BF16)
        aff_ref[i] = aff


def _merge(xc, o_att, y_f, y_b, rv, kg, cv, gt, mt, r_k, gn_w, gn_b, seg, conv_w, gain2, w_router_t, layer, stacked,
           n_lat_tiles):
    b, s, d = xc.shape
    n_tiles = s // ROW_TILE
    ne = w_router_t.shape[0]
    nb = TOKEN_BATCH
    hp, hn = _halo_specs(cv.shape[2], n_tiles, cv.dtype)
    tok = lambda a: pl.BlockSpec((nb, ROW_TILE, a.shape[2]), lambda i, j: (i, j, 0))
    full = lambda a: pl.BlockSpec(a.shape, lambda i, j: (0,) * a.ndim)
    consts = (r_k, gn_w, gn_b, seg, conv_w, gain2, w_router_t)
    return pl.pallas_call(
        functools.partial(_merge_kernel, n_lat_tiles=n_lat_tiles, n_tiles=n_tiles),
        grid=(b // nb, n_tiles),
        in_specs=[tok(a) for a in (xc, o_att, y_f, y_b, rv, kg, cv)] + [hp, hn, tok(gt),
                  pl.BlockSpec((nb, 1, 6, d), lambda i, j: (i, j // n_lat_tiles, 0, 0))]
                 + [full(a) for a in consts] + [_layer_spec(a, layer) for a in stacked],
        out_specs=[tok(xc), tok(xc), pl.BlockSpec((nb, ne, ROW_TILE), lambda i, j: (i, 0, j))],
        out_shape=[jax.ShapeDtypeStruct((b, s, d), F32), jax.ShapeDtypeStruct((b, s, d), BF16),
                   jax.ShapeDtypeStruct((b, ne, s), F32)],
        compiler_params=_params("parallel", "parallel"),
        name="merge",
    )(xc, o_att, y_f, y_b, rv, kg, cv, cv, cv, gt, mt, *consts, *stacked)


def _select_kernel(aff_ref, pos_ref, gate_ref, *, cap):
    a = aff_ref[0]
    ne, n = a.shape
    bits = pltpu.bitcast(a, jnp.int32)

    def count(mask):
        return jnp.sum(jnp.where(mask, 1.0, 0.0), axis=1, keepdims=True)

    def body(_, carry):
        lo, hi = carry
        mid = lo + ((hi - lo + 1) >> 1)
        ok = count(bits >= mid) >= cap
        return jnp.where(ok, mid, lo), jnp.where(ok, hi, mid - 1)

    lo0 = jnp.zeros((ne, 1), jnp.int32)
    hi0 = jnp.full((ne, 1), 0x7F800000, jnp.int32)
    thr, _ = lax.fori_loop(0, 32, body, (lo0, hi0))
    gt = bits > thr
    eq = bits == thr
    need = cap - count(gt)

    def tokens_before(mask):
        m = jnp.where(mask, 1.0, 0.0).astype(BF16)
        blk = min(n, SELECT_BLOCK)
        cols = []
        for j in range(n // blk):
            s_idx = lax.broadcasted_iota(jnp.int32, (n, blk), 0)
            t_idx = lax.broadcasted_iota(jnp.int32, (n, blk), 1) + j * blk
            cols.append(_dg(m, jnp.where(s_idx < t_idx, 1.0, 0.0).astype(BF16)))
        return jnp.concatenate(cols, axis=1)

    sel = jnp.logical_or(gt, jnp.logical_and(eq, tokens_before(eq) < need))
    pos_ref[0] = jnp.where(sel, tokens_before(sel).astype(jnp.int32), -1)
    gate_ref[0] = jnp.where(sel, a, 0.0)


def _select(aff_t, tok0, n, cap):
    b, ne, _ = aff_t.shape
    blk = tok0 // n
    return pl.pallas_call(
        functools.partial(_select_kernel, cap=cap),
        grid=(b,),
        in_specs=[pl.BlockSpec((1, ne, n), lambda i: (i, 0, blk))],
        out_specs=[pl.BlockSpec((1, ne, n), lambda i: (i, 0, 0)), pl.BlockSpec((1, ne, n), lambda i: (i, 0, 0))],
        out_shape=[jax.ShapeDtypeStruct((b, ne, n), jnp.int32), jax.ShapeDtypeStruct((b, ne, n), F32)],
        compiler_params=_params("parallel"),
        name="moe_select",
    )(aff_t)


def _expert_kernel(h_ref, pos_ref, gate_ref, wg_ref, wu_ref, wd_ref, o_ref, *, cap):
    e = pl.program_id(1)

    @pl.when(e == 0)
    def _():
        o_ref[...] = jnp.zeros_like(o_ref)

    nb, n, _ = h_ref.shape
    slot = lax.broadcasted_iota(jnp.int32, (cap, n), 0)
    onehots, xes, gates = [], [], []
    for i in range(nb):
        hit = pos_ref[i, 0] == slot
        onehot = jnp.where(hit, 1.0, 0.0).astype(BF16)
        onehots.append(onehot)
        xes.append(_dg(onehot, h_ref[i]).astype(BF16))
        gates.append(jnp.sum(jnp.where(hit, gate_ref[i, 0], 0.0), axis=1, keepdims=True))
    xe = jnp.concatenate(xes, axis=0)
    hg = _dg(xe, wg_ref[0])
    hu = _dg(xe, wu_ref[0])
    hid = (hg * _sigmoid(hg) * hu).astype(BF16)
    ye = (_dg(hid, wd_ref[0]) * jnp.concatenate(gates, axis=0)).astype(BF16)
    for i in range(nb):
        o_ref[i] += _dg(onehots[i], ye[i * cap:(i + 1) * cap], TN)


def _experts(h, pos, gate, wg, wu, wd, layer, tok0, n, cap):
    b, s, d = h.shape
    ne = pos.shape[1]
    f = wg.shape[2]
    blk = tok0 // n
    nb = max(1, min(b // 2, EXPERT_ROWS // cap))
    assert b % nb == 0
    sel = pl.BlockSpec((nb, 1, 1, n), lambda i, e: (i, e, 0, 0))
    return pl.pallas_call(
        functools.partial(_expert_kernel, cap=cap),
        grid=(b // nb, ne),
        in_specs=[pl.BlockSpec((nb, n, d), lambda i, e: (i, blk, 0)), sel, sel,
                  pl.BlockSpec((1, d, f), lambda i, e: (layer * ne + e, 0, 0)),
                  pl.BlockSpec((1, d, f), lambda i, e: (layer * ne + e, 0, 0)),
                  pl.BlockSpec((1, f, d), lambda i, e: (layer * ne + e, 0, 0))],
        out_specs=pl.BlockSpec((nb, n, d), lambda i, e: (i, 0, 0)),
        out_shape=jax.ShapeDtypeStruct((b, n, d), F32),
        compiler_params=_params("parallel", "arbitrary"),
        name="moe_experts",
    )(h, pos.reshape(b, ne, 1, n), gate.reshape(b, ne, 1, n), wg, wu, wd)


def _residual_kernel(x_ref, m_ref, mt_ref, o_ref):
    o_ref[0] = x_ref[0] + mt_ref[0, 0, 5:6] * m_ref[0]


def _moe_residual(xc, moe, mt, tok0, stream):
    b, s, d = xc.shape
    n = moe.shape[1]
    t0 = tok0 // ROW_TILE
    return pl.pallas_call(
        _residual_kernel,
        grid=(b, n // ROW_TILE),
        in_specs=[pl.BlockSpec((1, ROW_TILE, d), lambda i, j: (i, t0 + j, 0)),
                  pl.BlockSpec((1, ROW_TILE, d), lambda i, j: (i, j, 0)),
                  pl.BlockSpec((1, 1, 6, d), lambda i, j: (i, stream, 0, 0))],
        out_specs=pl.BlockSpec((1, ROW_TILE, d), lambda i, j: (i, t0 + j, 0)),
        out_shape=jax.ShapeDtypeStruct((b, s, d), F32),
        input_output_aliases={0: 0},
        compiler_params=_params("parallel", "parallel"),
        name="moe_residual",
    )(xc, moe, mt)


def _final_norm_kernel(x_ref, g_ref, o_ref):
    o_ref[0] = _rms(x_ref[0]) * g_ref[...]


def _final_norm(xc, gain, n_lat):
    b, s, d = xc.shape
    return pl.pallas_call(
        _final_norm_kernel,
        grid=(b, n_lat // ROW_TILE),
        in_specs=[pl.BlockSpec((1, ROW_TILE, d), lambda i, j: (i, j, 0)),
                  pl.BlockSpec((1, d), lambda i, j: (0, 0))],
        out_specs=pl.BlockSpec((1, ROW_TILE, d), lambda i, j: (i, j, 0)),
        out_shape=jax.ShapeDtypeStruct((b, n_lat, d), F32),
        compiler_params=_params("parallel", "parallel"),
        name="final_norm",
    )(xc, gain)


def _rope_tables(n_lat, n_ctx):
    rows = n_lat // GRID_W
    row = jnp.repeat(jnp.arange(rows, dtype=F32), GRID_W)
    col = jnp.tile(jnp.arange(GRID_W, dtype=F32), rows)
    axis_dim = HEAD_DIM // 2
    inv_freq = ROPE_THETA ** (-jnp.arange(0, axis_dim, 2, dtype=F32) / axis_dim)
    ang_r = row[:, None] * inv_freq[None, :]
    ang_c = col[:, None] * inv_freq[None, :]
    cos_h = jnp.concatenate([jnp.cos(ang_r), jnp.cos(ang_r), jnp.cos(ang_c), jnp.cos(ang_c)], axis=1)
    sin_h = jnp.concatenate([-jnp.sin(ang_r), jnp.sin(ang_r), -jnp.sin(ang_c), jnp.sin(ang_c)], axis=1)
    cos_h = jnp.concatenate([cos_h, jnp.ones((n_ctx, HEAD_DIM), F32)], axis=0)
    sin_h = jnp.concatenate([sin_h, jnp.zeros((n_ctx, HEAD_DIM), F32)], axis=0)
    return jnp.tile(cos_h, (1, ATT_HEADS)), jnp.tile(sin_h, (1, ATT_HEADS))


def _head_sum_matrix(width):
    idx = np.arange(width) // HEAD_DIM
    return jnp.asarray(idx[:, None] == idx[None, :], dtype=BF16)


def _two_dir_lowrank(w2):
    _, rank, w = w2.shape
    z = jnp.zeros((rank, w), w2.dtype)
    return jnp.concatenate([jnp.concatenate([w2[0], z], axis=1), jnp.concatenate([z, w2[1]], axis=1)], axis=0)


def kernel(x, c, ctx, c_ctx, ada_w, ada_b, norm1, w_in, q_gain, k_gain, shift_mu, decay_w0, decay_w2, iclr_a0, iclr_a2, gate_g2, rwkv_kk, rwkv_ka, rwkv_rk, rwkv_gn_w, rwkv_gn_b, conv_w, w_br_att, w_br_rwkv, w_br_conv, w_out, norm2, w_router, exp_gate, exp_up, exp_down, final_norm):
    b, n_lat, d = x.shape
    n_ctx = ctx.shape[1]
    depth = ada_w.shape[0]
    assert n_lat % ROW_TILE == 0 and n_ctx % ROW_TILE == 0 and n_lat % n_ctx == 0
    n_lat_tiles = n_lat // ROW_TILE

    pad = (-(b + 1)) % 8
    cc = jnp.concatenate([c, c_ctx[None, :], jnp.zeros((pad, d), F32)], axis=0)
    mods = _ada_table(cc, ada_w, ada_b)

    cos_t, sin_t = _rope_tables(n_lat, n_ctx)
    seg = _head_sum_matrix(ATT_WIDTH)
    att_w = ATT_WIDTH + 2 * ATT_KV_WIDTH
    cv_w = 3 * CONV_WIDTH
    offs = np.cumsum([0, att_w, RWKV_SEG, cv_w, 3 * d])

    w_groups = [w_in[:, :, offs[i]:offs[i + 1]].astype(BF16) for i in range(4)]
    w_tail = [t.astype(BF16) for t in (w_br_att, w_br_rwkv, w_br_conv, w_out)]
    ne, f = exp_gate.shape[1], exp_gate.shape[3]
    wg = exp_gate.astype(BF16).reshape(depth * ne, d, f)
    wu = exp_up.astype(BF16).reshape(depth * ne, d, f)
    wd = exp_down.astype(BF16).reshape(depth * ne, f, d)

    xc = jnp.concatenate([x, ctx], axis=1)
    for l in range(depth):
        mod_lat = mods[l, :b].reshape(b, 1, 6, d)
        mod_ctx = jnp.broadcast_to(mods[l, b].reshape(1, 1, 6, d), (b, 1, 6, d))
        mt = jnp.concatenate([mod_lat, mod_ctx], axis=1)

        gain1 = norm1[l].reshape(1, d)
        qvk, rw, cv, gt = _in_proj(
            xc, gain1, mt, cos_t, sin_t, jnp.tile(q_gain[l], ATT_HEADS).reshape(1, -1),
            jnp.tile(k_gain[l], ATT_KV_HEADS).reshape(1, -1), seg, l, w_groups, n_lat_tiles)
        o_att = _attention(qvk, n_lat)

        rv, kg, nkk, lw, kd, bb = _rwkv_prep(
            rw, shift_mu[l].reshape(1, -1), _two_dir_lowrank(decay_w2[l]), decay_w0[l].reshape(1, -1),
            _two_dir_lowrank(iclr_a2[l]), iclr_a0[l].reshape(1, -1), gate_g2[l],
            rwkv_kk[l].reshape(1, -1), rwkv_ka[l].reshape(1, -1), seg, n_lat_tiles)
        y_f, y_b = _rwkv_scan(rv, nkk, lw, kd, bb, n_lat)

        xc, h2, aff_t = _merge(
            xc, o_att, y_f, y_b, rv, kg, cv, gt, mt, rwkv_rk[l].reshape(1, -1), rwkv_gn_w[l].reshape(1, -1),
            rwkv_gn_b[l].reshape(1, -1), seg, conv_w[l], norm2[l].reshape(1, d), w_router[l].T, l, w_tail,
            n_lat_tiles)
        streams = [(0, n_lat, 0)] + ([(n_lat, n_ctx, 1)] if l < depth - 1 else [])
        for tok0, n, stream in streams:
            cap = CAPACITY_FACTOR * n // N_EXPERTS
            pos, gate = _select(aff_t, tok0, n, cap)
            moe = _experts(h2, pos, gate, wg, wu, wd, l, tok0, n, cap)
            xc = _moe_residual(xc, moe, mt, tok0, stream)
    return _final_norm(xc, final_norm.reshape(1, d), n_lat)
```

```python
import functools

import numpy as np
import jax
import jax.numpy as jnp
from jax import lax
from jax.experimental import pallas as pl
from jax.experimental.pallas import tpu as pltpu

F32 = jnp.float32
BF16 = jnp.bfloat16

GRID_W = 64
NORM_EPS = 1e-6
LOG2_E = float(np.log2(np.e))
ATT_HEADS = 8
ATT_KV_HEADS = 2
HEAD_DIM = 64
ATT_GROUP = ATT_HEADS // ATT_KV_HEADS
ATT_WIDTH = ATT_HEADS * HEAD_DIM
ATT_KV_WIDTH = ATT_KV_HEADS * HEAD_DIM
ROPE_THETA = 10000.0
RWKV_HEADS = 8
RWKV_WIDTH = RWKV_HEADS * HEAD_DIM
DECAY_RANK = 64
ICLR_RANK = 64
GATE_RANK = 128
RWKV_GN_EPS = 64e-5
RWKV_SEG = 3 * RWKV_WIDTH + 2 * DECAY_RANK + 2 * ICLR_RANK + GATE_RANK
CONV_WIDTH = 512
N_EXPERTS = 16
CAPACITY_FACTOR = 2

ROW_TILE = 256
CHUNK = 64
QUAD = 4 * HEAD_DIM
SCAN_BATCH = 4
SCAN_CHUNKS = 2
TOKEN_BATCH = 2
HALO = 8
SELECT_BLOCK = 256
SELECT_BATCH = 4
EXPERT_ROWS = 256
VMEM_LIMIT = 56 * 1024 * 1024

NT = (((1,), (1,)), ((), ()))
TN = (((0,), (0,)), ((), ()))
NN = (((1,), (0,)), ((), ()))


def _params(*sem):
    return pltpu.CompilerParams(dimension_semantics=sem, vmem_limit_bytes=VMEM_LIMIT)


def _layer_spec(a, layer):
    return pl.BlockSpec((1,) + a.shape[1:], lambda *_: (layer,) + (0,) * (a.ndim - 1))


def _dg(a, b, dn=NN):
    return lax.dot_general(a, b, dn, preferred_element_type=F32)


def _split2(x):
    hi = x.astype(BF16)
    lo = (x - hi.astype(F32)).astype(BF16)
    return hi, lo


def _mm1(a, b, dn=NN):
    return _dg(a.astype(BF16), b.astype(BF16), dn)


def _mm3(a, b, dn=NN):
    ah, al = _split2(a)
    bh, bl = _split2(b)
    return _dg(ah, bh, dn) + (_dg(ah, bl, dn) + _dg(al, bh, dn))


def _mm_exact_lhs(a_bf16, b, dn=NN):
    hi, lo = _split2(b)
    return _dg(a_bf16, hi, dn) + _dg(a_bf16, lo, dn)


def _head_sum(x, seg):
    return _dg(x.astype(BF16), seg)


def _sigmoid(x):
    return 0.5 * jnp.tanh(0.5 * x) + 0.5


def _rms(x):
    return x * lax.rsqrt(jnp.mean(x * x, axis=-1, keepdims=True) + NORM_EPS)


def _ada_kernel(c_ref, w_ref, b_ref, o_ref):
    c = c_ref[...]
    s = c * _sigmoid(c)
    o_ref[0] = _mm3(s, w_ref[0]) + b_ref[0]


def _ada_table(cc, ada_w, ada_b):
    n_layers, d, six_d = ada_w.shape
    rows = cc.shape[0]
    tn = six_d // 4
    return pl.pallas_call(
        _ada_kernel,
        grid=(n_layers, six_d // tn),
        in_specs=[pl.BlockSpec((rows, d), lambda l, j: (0, 0)),
                  pl.BlockSpec((1, d, tn), lambda l, j: (l, 0, j)),
                  pl.BlockSpec((1, 1, tn), lambda l, j: (l, 0, j))],
        out_specs=pl.BlockSpec((1, rows, tn), lambda l, j: (l, 0, j)),
        out_shape=jax.ShapeDtypeStruct((n_layers, rows, six_d), F32),
        compiler_params=_params("parallel", "parallel"),
        name="ada_table",
    )(cc, ada_w, ada_b.reshape(n_layers, 1, six_d))


def _swap16(x):
    w = x.shape[-1]
    lane = lax.broadcasted_iota(jnp.int32, x.shape, x.ndim - 1)
    return jnp.where((lane & 16) == 0, pltpu.roll(x, w - 16, x.ndim - 1), pltpu.roll(x, 16, x.ndim - 1))


def _att_heads(a, cos, sin, q_gain, k_gain, seg):
    def norm_rope(u, gain, cos_u, sin_u, seg_u):
        ms = _head_sum(u * u, seg_u) * (1.0 / HEAD_DIM)
        un = u * lax.rsqrt(ms + NORM_EPS) * gain
        return un * cos_u + _swap16(un) * sin_u

    kw = ATT_KV_WIDTH
    q = norm_rope(a[:, :ATT_WIDTH], q_gain, cos, sin, seg)
    k = norm_rope(a[:, ATT_WIDTH:ATT_WIDTH + kw], k_gain, cos[:, :kw], sin[:, :kw], seg[:kw, :kw])
    q = (q * (HEAD_DIM ** -0.5 * LOG2_E)).astype(BF16)
    ones = jnp.ones((a.shape[0], HEAD_DIM), BF16)
    v = a[:, ATT_WIDTH + kw:].astype(BF16)
    v1 = jnp.concatenate([t for h in range(ATT_KV_HEADS) for t in (v[:, h * HEAD_DIM:(h + 1) * HEAD_DIM], ones)],
                         axis=1)
    return q, k.astype(BF16), v1


def _in_proj_kernel(x_ref, gain_ref, mt_ref, cos_ref, sin_ref, qg_ref, kg_ref, seg_ref, wa_ref, wr_ref, wc_ref, wg_ref,
                    qvk_ref, rw_ref, cv_ref, gt_ref):
    for i in range(x_ref.shape[0]):
        h = _rms(x_ref[i]) * gain_ref[...]
        h = (h * (1.0 + mt_ref[i, 0, 1:2]) + mt_ref[i, 0, 0:1]).astype(BF16)
        q, k, v1 = _att_heads(_dg(h, wa_ref[0]), cos_ref[...], sin_ref[...], qg_ref[...], kg_ref[...], seg_ref[...])
        qvk_ref[i] = jnp.concatenate([q, v1, k], axis=1)
        rw_ref[i] = _dg(h, wr_ref[0]).astype(BF16)
        cv_ref[i] = _dg(h, wc_ref[0]).astype(BF16)
        gt_ref[i] = _dg(h, wg_ref[0]).astype(BF16)


def _in_proj(xc, gain, mt, cos_t, sin_t, q_gain_t, k_gain_t, seg, layer, weights, n_lat_tiles):
    b, s, d = xc.shape
    nb = TOKEN_BATCH
    tok = lambda n: pl.BlockSpec((nb, ROW_TILE, n), lambda i, j: (i, j, 0))
    full = lambda a: pl.BlockSpec(a.shape, lambda i, j: (0,) * a.ndim)
    table = pl.BlockSpec((ROW_TILE, ATT_WIDTH), lambda i, j: (j, 0))
    widths = [ATT_WIDTH + 3 * ATT_KV_WIDTH] + [w.shape[2] for w in weights[1:]]
    return pl.pallas_call(
        _in_proj_kernel,
        grid=(b // nb, s // ROW_TILE),
        in_specs=[tok(d), full(gain), pl.BlockSpec((nb, 1, 6, d), lambda i, j: (i, j // n_lat_tiles, 0, 0)),
                  table, table, full(q_gain_t), full(k_gain_t), full(seg)]
                 + [_layer_spec(w, layer) for w in weights],
        out_specs=[tok(n) for n in widths],
        out_shape=[jax.ShapeDtypeStruct((b, s, n), BF16) for n in widths],
        compiler_params=_params("parallel", "parallel"),
        name="in_proj",
    )(xc, gain, mt, cos_t, sin_t, q_gain_t, k_gain_t, seg, *weights)


def _attn_kernel(q_ref, k_ref, v_ref, o_ref, *, n_lat, n_lat_tiles):
    j = pl.program_id(1)
    s_all = k_ref.shape[1]

    def run(k0, k1):
        def scores(h):
            kvh = h // ATT_GROUP
            k = k_ref[0, k0:k1, kvh * HEAD_DIM:(kvh + 1) * HEAD_DIM]
            return _dg(q_ref[0, :, h * HEAD_DIM:(h + 1) * HEAD_DIM], k, NT)

        s_next = scores(0)
        for h in range(ATT_HEADS):
            s = s_next
            if h + 1 < ATT_HEADS:
                s_next = scores(h + 1)
            kvh = h // ATT_GROUP
            v1 = v_ref[0, k0:k1, 2 * kvh * HEAD_DIM:2 * (kvh + 1) * HEAD_DIM]
            p = jnp.exp2((s - jnp.max(s, axis=-1, keepdims=True)).astype(BF16))
            ov = _dg(p, v1)
            o = ov[:, :HEAD_DIM] / ov[:, HEAD_DIM:]
            o_ref[0, :, h * HEAD_DIM:(h + 1) * HEAD_DIM] = o.astype(BF16)

    @pl.when(j < n_lat_tiles)
    def _():
        run(0, s_all)

    @pl.when(j >= n_lat_tiles)
    def _():
        run(n_lat, s_all)


def _attention(qvk, n_lat):
    b, s, _ = qvk.shape
    v_w = 2 * ATT_KV_WIDTH
    return pl.pallas_call(
        functools.partial(_attn_kernel, n_lat=n_lat, n_lat_tiles=n_lat // ROW_TILE),
        grid=(b, s // ROW_TILE),
        in_specs=[pl.BlockSpec((1, ROW_TILE, ATT_WIDTH), lambda i, j: (i, j, 0)),
                  pl.BlockSpec((1, s, ATT_KV_WIDTH), lambda i, j: (i, 0, (ATT_WIDTH + v_w) // ATT_KV_WIDTH)),
                  pl.BlockSpec((1, s, v_w), lambda i, j: (i, 0, ATT_WIDTH // v_w))],
        out_specs=pl.BlockSpec((1, ROW_TILE, ATT_WIDTH), lambda i, j: (i, j, 0)),
        out_shape=jax.ShapeDtypeStruct((b, s, ATT_WIDTH), BF16),
        compiler_params=_params("parallel", "parallel"),
        name="attention",
    )(qvk, qvk, qvk)


def _neighbours(x, halo_prev, halo_next, first, last):
    rows = x.shape[0]
    ridx = lax.broadcasted_iota(jnp.int32, x.shape, 0)
    row_p = jnp.where(first, 0.0, halo_prev[halo_prev.shape[0] - 1:])
    row_n = jnp.where(last, 0.0, halo_next[0:1])
    prev = jnp.where(ridx == 0, row_p, pltpu.roll(x, 1, 0))
    nxt = jnp.where(ridx == rows - 1, row_n, pltpu.roll(x, rows - 1, 0))
    return prev, nxt


def _tile_ends(j, n_lat_tiles, n_tiles):
    first = jnp.logical_or(j == 0, j == n_lat_tiles)
    last = jnp.logical_or(j == n_lat_tiles - 1, j == n_tiles - 1)
    return first, last


def _halo_specs(width, n_tiles, dtype):
    halo = HALO * (4 // jnp.dtype(dtype).itemsize)
    per = ROW_TILE // halo
    nb = TOKEN_BATCH
    prev = pl.BlockSpec((nb, halo, width), lambda i, j: (i, jnp.maximum(j * per - 1, 0), 0))
    nxt = pl.BlockSpec((nb, halo, width), lambda i, j: (i, jnp.minimum((j + 1) * per, n_tiles * per - 1), 0))
    return prev, nxt


def _rwkv_prep_kernel(x_ref, hp_ref, hn_ref, mu_ref, w2_ref, w0_ref, a2_ref, a0_ref, g2_ref, kk_ref, ka_ref,
                      seg_ref, rv_ref, kg_ref, nkk_ref, lw_ref, kd_ref, bb_ref, *, n_lat_tiles, n_tiles):
    j = pl.program_id(1)
    first, last = _tile_ends(j, n_lat_tiles, n_tiles)
    w = RWKV_WIDTH
    halo = hp_ref.shape[1]
    rows = x_ref.shape[1]
    t_idx = lax.broadcasted_iota(jnp.int32, (rows, rows + 2 * halo), 0) + halo
    u_idx = lax.broadcasted_iota(jnp.int32, (rows, rows + 2 * halo), 1)
    is_prev = jnp.logical_and(u_idx == t_idx - 1, jnp.logical_or(u_idx >= halo, jnp.logical_not(first)))
    is_next = jnp.logical_and(u_idx == t_idx + 1, jnp.logical_or(u_idx < halo + rows, jnp.logical_not(last)))
    band = jnp.where(jnp.logical_or(is_prev, is_next), 0.5, 0.0).astype(BF16)
    for i in range(x_ref.shape[0]):
        x = x_ref[i].astype(F32)
        mean_nb = _dg(band, jnp.concatenate([hp_ref[i], x_ref[i], hn_ref[i]], axis=0))
        xs = x + mu_ref[...] * (mean_nb - x)
        r = xs[:, 0:w]
        k = xs[:, w:2 * w]
        v = xs[:, 2 * w:3 * w]
        o = 3 * w
        w_lo = xs[:, o:o + 2 * DECAY_RANK]
        a_lo = xs[:, o + 2 * DECAY_RANK:o + 2 * DECAY_RANK + 2 * ICLR_RANK]
        g_lo = xs[:, o + 2 * DECAY_RANK + 2 * ICLR_RANK:]
        w_pre = w0_ref[...] + _mm1(jnp.tanh(w_lo), w2_ref[...])
        logw = -_sigmoid(w_pre) * float(np.exp(-0.5))
        a = _sigmoid(a0_ref[...] + _mm1(a_lo, a2_ref[...]))
        g = _mm1(_sigmoid(g_lo), g2_ref[...])
        kk = k * kk_ref[...]
        kk = kk * lax.rsqrt(_head_sum(kk * kk, seg_ref[...]) + 1e-12)
        for c, (t_scan, t_tail) in enumerate(((r, k), (v, g))):
            rv_ref[i, :, c * w:(c + 1) * w] = t_scan.astype(BF16)
            kg_ref[i, :, c * w:(c + 1) * w] = t_tail.astype(BF16)
        nkk_ref[i] = (-kk).astype(BF16)
        for d in range(2):
            a_d = a[:, d * w:(d + 1) * w]
            lw_ref[i, d] = logw[:, d * w:(d + 1) * w]
            kd_ref[i, d] = (k * (1.0 + (a_d - 1.0) * ka_ref[...])).astype(BF16)
            bb_ref[i, d] = (kk * a_d).astype(BF16)


def _rwkv_prep(rw, mu, w2cat, w0, a2cat, a0, g2, k_k, k_a, seg, n_lat_tiles):
    b, s, wseg = rw.shape
    n_tiles = s // ROW_TILE
    w = RWKV_WIDTH
    nb = TOKEN_BATCH
    hp, hn = _halo_specs(wseg, n_tiles, rw.dtype)
    full = lambda shape: pl.BlockSpec(shape, lambda i, j: (0,) * len(shape))
    tok = pl.BlockSpec((nb, ROW_TILE, w), lambda i, j: (i, j, 0))
    tok2 = pl.BlockSpec((nb, 2, ROW_TILE, w), lambda i, j: (i, 0, j, 0))
    pair = pl.BlockSpec((nb, ROW_TILE, 2 * w), lambda i, j: (i, j, 0))
    one = jax.ShapeDtypeStruct((b, s, w), BF16)
    two = jax.ShapeDtypeStruct((b, 2, s, w), BF16)
    two_f32 = jax.ShapeDtypeStruct((b, 2, s, w), F32)
    return pl.pallas_call(
        functools.partial(_rwkv_prep_kernel, n_lat_tiles=n_lat_tiles, n_tiles=n_tiles),
        grid=(b // nb, n_tiles),
        in_specs=[pl.BlockSpec((nb, ROW_TILE, wseg), lambda i, j: (i, j, 0)), hp, hn,
                  full((1, wseg)), full(w2cat.shape), full((1, 2 * w)), full(a2cat.shape), full((1, 2 * w)),
                  full(g2.shape), full((1, w)), full((1, w)), full(seg.shape)],
        out_specs=[pair, pair, tok, tok2, tok2, tok2],
        out_shape=[jax.ShapeDtypeStruct((b, s, 2 * w), BF16)] * 2 + [one, two_f32, two, two],
        compiler_params=_params("parallel", "parallel"),
        name="rwkv_prep",
    )(rw, rw, rw, mu, w2cat, w0, a2cat, a0, g2, k_k, k_a, seg)


def _block_diag(y, head_masks):
    return jnp.concatenate([jnp.where(m, y, 0.0) for m in head_masks], axis=0)


def _scan_chain(sgn, r, v, nkk, lw, kd, bb, s0):
    row = lax.broadcasted_iota(jnp.int32, (CHUNK, QUAD), 0)
    lane = lax.broadcasted_iota(jnp.int32, (CHUNK, QUAD), 1)
    rel = ((lane % CHUNK) - row) * sgn
    strict = rel < 0
    incl = rel <= 0
    eye = jnp.where(rel == 0, 1.0, 0.0)
    head_masks = [(lane // HEAD_DIM) == h for h in range(4)]
    bd_mask = (lax.broadcasted_iota(jnp.int32, (QUAD, QUAD), 0) // HEAD_DIM
               == lax.broadcasted_iota(jnp.int32, (QUAD, QUAD), 1) // HEAD_DIM)
    t_row = lax.broadcasted_iota(jnp.int32, (CHUNK, CHUNK), 0)
    t_col = lax.broadcasted_iota(jnp.int32, (CHUNK, CHUNK), 1)
    tri = jnp.where((t_col - t_row) * sgn <= 0, 1.0, 0.0).astype(BF16)
    last_row = CHUNK - 1 if sgn > 0 else 0
    bd = lambda t: _block_diag(t, head_masks).astype(BF16)

    cum = _mm_exact_lhs(tri, lw)
    yield
    cum_end = cum[last_row:last_row + 1]
    w_inv = jnp.exp(-cum)
    w_rem = jnp.exp(cum_end - cum)
    ar = jnp.concatenate([nkk * jnp.exp(cum - lw), r * jnp.exp(cum)], axis=0).astype(BF16)
    xb = _dg(ar, bd(bb * w_inv), NT)
    xk = _dg(ar, bd(kd * w_inv), NT)
    ars = _dg(ar, s0.astype(BF16), NT)
    yield
    n_ab = jnp.where(strict, xb[:CHUNK], 0.0)
    l_ak = jnp.where(strict, xk[:CHUNK], 0.0)
    g_rb = jnp.where(incl, xb[CHUNK:], 0.0)
    g_rk = jnp.where(incl, xk[CHUNK:], 0.0)

    m = eye + n_ab
    p = _dg(n_ab.astype(BF16), bd(n_ab))
    lg = _dg(jnp.concatenate([l_ak, g_rk], axis=0).astype(BF16), bd(v))
    yield
    for _ in range(int(np.log2(CHUNK)) - 2):
        both = _dg(jnp.concatenate([p, m], axis=0).astype(BF16), bd(p))
        yield
        p = both[:CHUNK]
        m = m + both[CHUNK:]
    m = m + _dg(m.astype(BF16), bd(p))
    yield
    x0 = ars[:CHUNK] + lg[:CHUNK]
    u = _dg(m.astype(BF16), bd(x0))
    yield
    y = ars[CHUNK:] + _dg(g_rb.astype(BF16), bd(u)) + lg[CHUNK:]
    z = _dg(jnp.concatenate([u, v], axis=0).astype(BF16),
            jnp.concatenate([bb * w_rem, kd * w_rem], axis=0).astype(BF16), TN)
    yield
    return y, s0 * jnp.exp(cum_end) + jnp.where(bd_mask, z, 0.0)


def _run_interleaved(chains, stagger=0):
    results = [None] * len(chains)
    live = list(enumerate(chains))
    rnd = 0
    while live:
        still = []
        for idx, g in live:
            if rnd >= idx * stagger:
                try:
                    next(g)
                except StopIteration as stop:
                    results[idx] = stop.value
                    continue
            still.append((idx, g))
        live = still
        rnd += 1
    return results


def _rwkv_scan_kernel(pf_ref, nf_ref, lwf_ref, kdf_ref, bbf_ref, pb_ref, nb_ref, lwb_ref, kdb_ref, bbb_ref,
                      yf_ref, yb_ref, s_ref):
    @pl.when(pl.program_id(1) == 0)
    def _():
        s_ref[...] = jnp.zeros_like(s_ref)

    w = RWKV_WIDTH
    f32 = lambda t: t.astype(F32)
    dirs = ((1, pf_ref, nf_ref, lwf_ref, kdf_ref, bbf_ref, yf_ref),
            (-1, pb_ref, nb_ref, lwb_ref, kdb_ref, bbb_ref, yb_ref))
    keys = [(i, d, q) for i in range(SCAN_BATCH) for d in range(2) for q in range(w // QUAD)]
    state = {key: s_ref[key] for key in keys}
    for step in range(SCAN_CHUNKS):
        work = []
        for i, d, q in keys:
            sgn, p_ref, n_ref, lw_ref, kd_ref, bb_ref, y_ref = dirs[d]
            c = step if sgn > 0 else SCAN_CHUNKS - 1 - step
            rows = slice(c * CHUNK, (c + 1) * CHUNK)
            ql = slice(q * QUAD, (q + 1) * QUAD)
            vl = slice(w + q * QUAD, w + (q + 1) * QUAD)
            args = (f32(p_ref[i, rows, ql]), f32(p_ref[i, rows, vl]), f32(n_ref[i, rows, ql]),
                    lw_ref[i, 0, rows, ql], f32(kd_ref[i, 0, rows, ql]), f32(bb_ref[i, 0, rows, ql]), state[(i, d, q)])
            work.append((sgn, args, y_ref, (i, rows, ql)))
        results = _run_interleaved([_scan_chain(sgn, *args) for sgn, args, _, _ in work])
        for key, (_, _, y_ref, y_idx), (y, s_new) in zip(keys, work, results):
            y_ref[y_idx] = y.astype(y_ref.dtype)
            state[key] = s_new
    for key in keys:
        s_ref[key] = state[key]


def _rwkv_scan(rv, nkk, lw, kd, bb, n_lat):
    b, s, w = nkk.shape
    rows = SCAN_CHUNKS * CHUNK
    assert s % rows == 0 and n_lat % rows == 0 and b % SCAN_BATCH == 0
    nc = s // rows
    nc_lat = n_lat // rows
    nc_ctx = nc - nc_lat
    nb = SCAN_BATCH

    fwd = lambda c: jnp.where(c < nc_ctx, nc_lat + c, c - nc_ctx)
    bwd = lambda c: jnp.where(c < nc_ctx, nc - 1 - c, nc_lat - 1 - (c - nc_ctx))
    tok = lambda order, a: pl.BlockSpec((nb, rows, a.shape[2]), lambda i, c: (i, order(c), 0))
    tok2 = lambda order, d: pl.BlockSpec((nb, 1, rows, w), lambda i, c: (i, d, order(c), 0))
    out = jax.ShapeDtypeStruct((b, s, w), BF16)
    return pl.pallas_call(
        _rwkv_scan_kernel,
        grid=(b // nb, nc),
        in_specs=[tok(fwd, rv), tok(fwd, nkk), tok2(fwd, 0), tok2(fwd, 0), tok2(fwd, 0),
                  tok(bwd, rv), tok(bwd, nkk), tok2(bwd, 1), tok2(bwd, 1), tok2(bwd, 1)],
        out_specs=[tok(fwd, nkk), tok(bwd, nkk)],
        out_shape=[out, out],
        scratch_shapes=[pltpu.VMEM((nb, 2, w // QUAD, QUAD, QUAD), F32)],
        compiler_params=_params("parallel", "arbitrary"),
        name="rwkv_scan",
    )(rv, nkk, lw, kd, bb, rv, nkk, lw, kd, bb)


def _rwkv_readout(y, r, k, v, g, r_k, gn_w, gn_b, seg):
    mean = _head_sum(y, seg) * (1.0 / HEAD_DIM)
    yc = y - mean
    var = _head_sum(yc * yc, seg) * (1.0 / HEAD_DIM)
    yn = yc * lax.rsqrt(var + RWKV_GN_EPS)
    bonus = _head_sum(r * k * r_k, seg) * v
    return (yn * gn_w + gn_b + bonus) * g


def _route(x, gain, shift, scale, w_router_t):
    h = _rms(x) * gain
    h = h * (1.0 + scale) + shift
    logits = _mm3(w_router_t, h, NT)
    e = jnp.exp(logits - jnp.max(logits, axis=0, keepdims=True))
    return h, e / jnp.sum(e, axis=0, keepdims=True)


def _merge_kernel(x_ref, oa_ref, yf_ref, yb_ref, rv_ref, kg_ref, cv_ref, hp_ref, hn_ref, gt_ref, mt_ref,
                  rk_ref, gw_ref, gb_ref, seg_ref, cw_ref, gain2_ref, wrt_ref, wa_ref, wr_ref, wc_ref, wo_ref,
                  o_ref, h_ref, aff_ref, *, n_lat_tiles, n_tiles):
    j = pl.program_id(1)
    first, last = _tile_ends(j, n_lat_tiles, n_tiles)
    cw = CONV_WIDTH
    d = x_ref.shape[2]
    w = RWKV_WIDTH
    def sample(i):
        r, v = [rv_ref[i, :, c * w:(c + 1) * w].astype(F32) for c in range(2)]
        k, g = [kg_ref[i, :, c * w:(c + 1) * w].astype(F32) for c in range(2)]
        o_rw = _rwkv_readout(yf_ref[i].astype(F32) + yb_ref[i].astype(F32), r, k, v, g, rk_ref[...], gw_ref[...],
                             gb_ref[...], seg_ref[...])
        yield
        cv = cv_ref[i].astype(F32)
        hp = hp_ref[i].astype(F32)
        hn = hn_ref[i].astype(F32)
        z = cv[:, cw:2 * cw] * cv[:, 2 * cw:]
        zp = hp[:, cw:2 * cw] * hp[:, 2 * cw:]
        zn = hn[:, cw:2 * cw] * hn[:, 2 * cw:]
        z_prev, z_next = _neighbours(z, zp, zn, first, last)
        o_cv = cv[:, :cw] * (cw_ref[0:1] * z_prev + cw_ref[1:2] * z + cw_ref[2:3] * z_next)
        yield
        t = jnp.tanh(gt_ref[i].astype(F32))
        m = ((1.0 + t[:, :d]) * _dg(oa_ref[i], wa_ref[0])
             + (1.0 + t[:, d:2 * d]) * _dg(o_rw.astype(BF16), wr_ref[0])
             + (1.0 + t[:, 2 * d:]) * _dg(o_cv.astype(BF16), wc_ref[0]))
        yield
        y = _dg(m.astype(BF16), wo_ref[0])
        x_new = x_ref[i] + mt_ref[i, 0, 2:3] * y
        o_ref[i] = x_new
        yield
        h, aff = _route(x_new, gain2_ref[...], mt_ref[i, 0, 3:4], mt_ref[i, 0, 4:5], wrt_ref[...])
        h_ref[i] = h.astype(BF16)
        aff_ref[i] = aff

    _run_interleaved([sample(i) for i in range(x_ref.shape[0])], stagger=2)


def _merge(xc, o_att, y_f, y_b, rv, kg, cv, gt, mt, r_k, gn_w, gn_b, seg, conv_w, gain2, w_router_t, layer, stacked,
           n_lat_tiles):
    b, s, d = xc.shape
    n_tiles = s // ROW_TILE
    ne = w_router_t.shape[0]
    nb = TOKEN_BATCH
    hp, hn = _halo_specs(cv.shape[2], n_tiles, cv.dtype)
    tok = lambda a: pl.BlockSpec((nb, ROW_TILE, a.shape[2]), lambda i, j: (i, j, 0))
    full = lambda a: pl.BlockSpec(a.shape, lambda i, j: (0,) * a.ndim)
    consts = (r_k, gn_w, gn_b, seg, conv_w, gain2, w_router_t)
    return pl.pallas_call(
        functools.partial(_merge_kernel, n_lat_tiles=n_lat_tiles, n_tiles=n_tiles),
        grid=(b // nb, n_tiles),
        in_specs=[tok(a) for a in (xc, o_att, y_f, y_b, rv, kg, cv)] + [hp, hn, tok(gt),
                  pl.BlockSpec((nb, 1, 6, d), lambda i, j: (i, j // n_lat_tiles, 0, 0))]
                 + [full(a) for a in consts] + [_layer_spec(a, layer) for a in stacked],
        out_specs=[tok(xc), tok(xc), pl.BlockSpec((nb, ne, ROW_TILE), lambda i, j: (i, 0, j))],
        out_shape=[jax.ShapeDtypeStruct((b, s, d), F32), jax.ShapeDtypeStruct((b, s, d), BF16),
                   jax.ShapeDtypeStruct((b, ne, s), F32)],
        compiler_params=_params("parallel", "parallel"),
        name="merge",
    )(xc, o_att, y_f, y_b, rv, kg, cv, cv, cv, gt, mt, *consts, *stacked)


def _select_kernel(aff_ref, pos_ref, gate_ref, *, cap):
    nb, n_exp, n = aff_ref.shape
    a = aff_ref[...].reshape(nb * n_exp, n)
    ne = nb * n_exp
    bits = pltpu.bitcast(a, jnp.int32)

    def count(mask):
        return jnp.sum(jnp.where(mask, 1.0, 0.0), axis=1, keepdims=True)

    def body(_, carry):
        lo, hi = carry
        mid = lo + ((hi - lo + 1) >> 1)
        ok = count(bits >= mid) >= cap
        return jnp.where(ok, mid, lo), jnp.where(ok, hi, mid - 1)

    lo0 = jnp.zeros((ne, 1), jnp.int32)
    hi0 = jnp.full((ne, 1), 0x7F800000, jnp.int32)
    thr, _ = lax.fori_loop(0, 32, body, (lo0, hi0))
    gt = bits > thr
    eq = bits == thr
    need = cap - count(gt)

    def tokens_before(mask):
        m = jnp.where(mask, 1.0, 0.0).astype(BF16)
        blk = min(n, SELECT_BLOCK)
        cols = []
        for j in range(n // blk):
            s_idx = lax.broadcasted_iota(jnp.int32, (n, blk), 0)
            t_idx = lax.broadcasted_iota(jnp.int32, (n, blk), 1) + j * blk
            cols.append(_dg(m, jnp.where(s_idx < t_idx, 1.0, 0.0).astype(BF16)))
        return jnp.concatenate(cols, axis=1)

    sel = jnp.logical_or(gt, jnp.logical_and(eq, tokens_before(eq) < need))
    pos_ref[...] = jnp.where(sel, tokens_before(sel).astype(jnp.int32), -1).reshape(nb, n_exp, n)
    gate_ref[...] = jnp.where(sel, a, 0.0).reshape(nb, n_exp, n)


def _select(aff_t, tok0, n, cap):
    b, ne, _ = aff_t.shape
    blk = tok0 // n
    nb = SELECT_BATCH if b % SELECT_BATCH == 0 else 1
    return pl.pallas_call(
        functools.partial(_select_kernel, cap=cap),
        grid=(b // nb,),
        in_specs=[pl.BlockSpec((nb, ne, n), lambda i: (i, 0, blk))],
        out_specs=[pl.BlockSpec((nb, ne, n), lambda i: (i, 0, 0)), pl.BlockSpec((nb, ne, n), lambda i: (i, 0, 0))],
        out_shape=[jax.ShapeDtypeStruct((b, ne, n), jnp.int32), jax.ShapeDtypeStruct((b, ne, n), F32)],
        compiler_params=_params("parallel"),
        name="moe_select",
    )(aff_t)


def _expert_kernel(h_ref, pos_ref, gate_ref, wg_ref, wu_ref, wd_ref, o_ref, *, cap):
    e = pl.program_id(1)

    @pl.when(e == 0)
    def _():
        o_ref[...] = jnp.zeros_like(o_ref)

    nb, n, _ = h_ref.shape
    slot = lax.broadcasted_iota(jnp.int32, (cap, n), 0)
    onehots, xes, gates = [], [], []
    for i in range(nb):
        hit = pos_ref[i, 0] == slot
        onehot = jnp.where(hit, 1.0, 0.0).astype(BF16)
        onehots.append(onehot)
        xes.append(_dg(onehot, h_ref[i]).astype(BF16))
        gates.append(jnp.sum(jnp.where(hit, gate_ref[i, 0], 0.0), axis=1, keepdims=True))
    xe = jnp.concatenate(xes, axis=0)
    hg = _dg(xe, wg_ref[0])
    hu = _dg(xe, wu_ref[0])
    hid = (hg * _sigmoid(hg) * hu).astype(BF16)
    ye = (_dg(hid, wd_ref[0]) * jnp.concatenate(gates, axis=0)).astype(BF16)
    for i in range(nb):
        o_ref[i] += _dg(onehots[i], ye[i * cap:(i + 1) * cap], TN)


def _experts(h, pos, gate, wg, wu, wd, layer, tok0, n, cap):
    b, s, d = h.shape
    ne = pos.shape[1]
    f = wg.shape[2]
    blk = tok0 // n
    nb = max(1, min(b // 2, EXPERT_ROWS // cap))
    assert b % nb == 0
    sel = pl.BlockSpec((nb, 1, 1, n), lambda i, e: (i, e, 0, 0))
    return pl.pallas_call(
        functools.partial(_expert_kernel, cap=cap),
        grid=(b // nb, ne),
        in_specs=[pl.BlockSpec((nb, n, d), lambda i, e: (i, blk, 0)), sel, sel,
                  pl.BlockSpec((1, d, f), lambda i, e: (layer * ne + e, 0, 0)),
                  pl.BlockSpec((1, d, f), lambda i, e: (layer * ne + e, 0, 0)),
                  pl.BlockSpec((1, f, d), lambda i, e: (layer * ne + e, 0, 0))],
        out_specs=pl.BlockSpec((nb, n, d), lambda i, e: (i, 0, 0)),
        out_shape=jax.ShapeDtypeStruct((b, n, d), F32),
        compiler_params=_params("parallel", "arbitrary"),
        name="moe_experts",
    )(h, pos.reshape(b, ne, 1, n), gate.reshape(b, ne, 1, n), wg, wu, wd)


def _residual_kernel(x_ref, m_ref, mt_ref, o_ref):
    o_ref[0] = x_ref[0] + mt_ref[0, 0, 5:6] * m_ref[0]


def _moe_residual(xc, moe, mt, tok0, stream):
    b, s, d = xc.shape
    n = moe.shape[1]
    t0 = tok0 // ROW_TILE
    return pl.pallas_call(
        _residual_kernel,
        grid=(b, n // ROW_TILE),
        in_specs=[pl.BlockSpec((1, ROW_TILE, d), lambda i, j: (i, t0 + j, 0)),
                  pl.BlockSpec((1, ROW_TILE, d), lambda i, j: (i, j, 0)),
                  pl.BlockSpec((1, 1, 6, d), lambda i, j: (i, stream, 0, 0))],
        out_specs=pl.BlockSpec((1, ROW_TILE, d), lambda i, j: (i, t0 + j, 0)),
        out_shape=jax.ShapeDtypeStruct((b, s, d), F32),
        input_output_aliases={0: 0},
        compiler_params=_params("parallel", "parallel"),
        name="moe_residual",
    )(xc, moe, mt)


def _final_norm_kernel(x_ref, g_ref, o_ref):
    o_ref[0] = _rms(x_ref[0]) * g_ref[...]


def _final_norm(xc, gain, n_lat):
    b, s, d = xc.shape
    return pl.pallas_call(
        _final_norm_kernel,
        grid=(b, n_lat // ROW_TILE),
        in_specs=[pl.BlockSpec((1, ROW_TILE, d), lambda i, j: (i, j, 0)),
                  pl.BlockSpec((1, d), lambda i, j: (0, 0))],
        out_specs=pl.BlockSpec((1, ROW_TILE, d), lambda i, j: (i, j, 0)),
        out_shape=jax.ShapeDtypeStruct((b, n_lat, d), F32),
        compiler_params=_params("parallel", "parallel"),
        name="final_norm",
    )(xc, gain)


def _rope_tables(n_lat, n_ctx):
    rows = n_lat // GRID_W
    row = jnp.repeat(jnp.arange(rows, dtype=F32), GRID_W)
    col = jnp.tile(jnp.arange(GRID_W, dtype=F32), rows)
    axis_dim = HEAD_DIM // 2
    inv_freq = ROPE_THETA ** (-jnp.arange(0, axis_dim, 2, dtype=F32) / axis_dim)
    ang_r = row[:, None] * inv_freq[None, :]
    ang_c = col[:, None] * inv_freq[None, :]
    cos_h = jnp.concatenate([jnp.cos(ang_r), jnp.cos(ang_r), jnp.cos(ang_c), jnp.cos(ang_c)], axis=1)
    sin_h = jnp.concatenate([-jnp.sin(ang_r), jnp.sin(ang_r), -jnp.sin(ang_c), jnp.sin(ang_c)], axis=1)
    cos_h = jnp.concatenate([cos_h, jnp.ones((n_ctx, HEAD_DIM), F32)], axis=0)
    sin_h = jnp.concatenate([sin_h, jnp.zeros((n_ctx, HEAD_DIM), F32)], axis=0)
    return jnp.tile(cos_h, (1, ATT_HEADS)), jnp.tile(sin_h, (1, ATT_HEADS))


def _head_sum_matrix(width):
    idx = np.arange(width) // HEAD_DIM
    return jnp.asarray(idx[:, None] == idx[None, :], dtype=BF16)


def _two_dir_lowrank(w2):
    _, rank, w = w2.shape
    z = jnp.zeros((rank, w), w2.dtype)
    return jnp.concatenate([jnp.concatenate([w2[0], z], axis=1), jnp.concatenate([z, w2[1]], axis=1)], axis=0)


def kernel(x, c, ctx, c_ctx, ada_w, ada_b, norm1, w_in, q_gain, k_gain, shift_mu, decay_w0, decay_w2, iclr_a0, iclr_a2, gate_g2, rwkv_kk, rwkv_ka, rwkv_rk, rwkv_gn_w, rwkv_gn_b, conv_w, w_br_att, w_br_rwkv, w_br_conv, w_out, norm2, w_router, exp_gate, exp_up, exp_down, final_norm):
    b, n_lat, d = x.shape
    n_ctx = ctx.shape[1]
    depth = ada_w.shape[0]
    assert n_lat % ROW_TILE == 0 and n_ctx % ROW_TILE == 0 and n_lat % n_ctx == 0
    n_lat_tiles = n_lat // ROW_TILE

    pad = (-(b + 1)) % 8
    cc = jnp.concatenate([c, c_ctx[None, :], jnp.zeros((pad, d), F32)], axis=0)
    mods = _ada_table(cc, ada_w, ada_b)

    cos_t, sin_t = _rope_tables(n_lat, n_ctx)
    seg = _head_sum_matrix(ATT_WIDTH)
    att_w = ATT_WIDTH + 2 * ATT_KV_WIDTH
    cv_w = 3 * CONV_WIDTH
    offs = np.cumsum([0, att_w, RWKV_SEG, cv_w, 3 * d])

    group_scale = (1.0, 1.0, 1.0, 0.5)
    w_groups = [(w_in[:, :, offs[i]:offs[i + 1]] * group_scale[i]).astype(BF16) for i in range(4)]
    w_tail = [t.astype(BF16) for t in (w_br_att, w_br_rwkv, w_br_conv, 0.5 * w_out)]
    ne, f = exp_gate.shape[1], exp_gate.shape[3]
    wg = exp_gate.astype(BF16).reshape(depth * ne, d, f)
    wu = exp_up.astype(BF16).reshape(depth * ne, d, f)
    wd = exp_down.astype(BF16).reshape(depth * ne, f, d)

    xc = jnp.concatenate([x, ctx], axis=1)
    for l in range(depth):
        mod_lat = mods[l, :b].reshape(b, 1, 6, d)
        mod_ctx = jnp.broadcast_to(mods[l, b].reshape(1, 1, 6, d), (b, 1, 6, d))
        mt = jnp.concatenate([mod_lat, mod_ctx], axis=1)

        gain1 = norm1[l].reshape(1, d)
        qvk, rw, cv, gt = _in_proj(
            xc, gain1, mt, cos_t, sin_t, jnp.tile(q_gain[l], ATT_HEADS).reshape(1, -1),
            jnp.tile(k_gain[l], ATT_KV_HEADS).reshape(1, -1), seg, l, w_groups, n_lat_tiles)
        o_att = _attention(qvk, n_lat)

        rv, kg, nkk, lw, kd, bb = _rwkv_prep(
            rw, shift_mu[l].reshape(1, -1), _two_dir_lowrank(decay_w2[l]), decay_w0[l].reshape(1, -1),
            _two_dir_lowrank(iclr_a2[l]), iclr_a0[l].reshape(1, -1), gate_g2[l],
            rwkv_kk[l].reshape(1, -1), rwkv_ka[l].reshape(1, -1), seg, n_lat_tiles)
        y_f, y_b = _rwkv_scan(rv, nkk, lw, kd, bb, n_lat)

        xc, h2, aff_t = _merge(
            xc, o_att, y_f, y_b, rv, kg, cv, gt, mt, rwkv_rk[l].reshape(1, -1), rwkv_gn_w[l].reshape(1, -1),
            rwkv_gn_b[l].reshape(1, -1), seg, conv_w[l], norm2[l].reshape(1, d), w_router[l].T, l, w_tail,
            n_lat_tiles)
        streams = [(0, n_lat, 0)] + ([(n_lat, n_ctx, 1)] if l < depth - 1 else [])
        for tok0, n, stream in streams:
            cap = CAPACITY_FACTOR * n // N_EXPERTS
            pos, gate = _select(aff_t, tok0, n, cap)
            moe = _experts(h2, pos, gate, wg, wu, wd, l, tok0, n, cap)
            xc = _moe_residual(xc, moe, mt, tok0, stream)
    return _final_norm(xc, final_norm.reshape(1, d), n_lat)
```

```python
import functools

import numpy as np
import jax
import jax.numpy as jnp
from jax import lax
from jax.experimental import pallas as pl
from jax.experimental.pallas import tpu as pltpu

F32 = jnp.float32
BF16 = jnp.bfloat16

GRID_W = 64
NORM_EPS = 1e-6
LOG2_E = float(np.log2(np.e))
ATT_HEADS = 8
ATT_KV_HEADS = 2
HEAD_DIM = 64
ATT_GROUP = ATT_HEADS // ATT_KV_HEADS
ATT_WIDTH = ATT_HEADS * HEAD_DIM
ATT_KV_WIDTH = ATT_KV_HEADS * HEAD_DIM
ROPE_THETA = 10000.0
RWKV_HEADS = 8
RWKV_WIDTH = RWKV_HEADS * HEAD_DIM
DECAY_RANK = 64
ICLR_RANK = 64
GATE_RANK = 128
RWKV_GN_EPS = 64e-5
RWKV_SEG = 3 * RWKV_WIDTH + 2 * DECAY_RANK + 2 * ICLR_RANK + GATE_RANK
CONV_WIDTH = 512
N_EXPERTS = 16
CAPACITY_FACTOR = 2

ROW_TILE = 256
ATT_TILE = 512
CHUNK = 64
QUAD = 4 * HEAD_DIM
SCAN_BATCH = 4
SCAN_CHUNKS = 2
TOKEN_BATCH = 2
HALO = 8
SELECT_BLOCK = 256
SELECT_BATCH = 4
EXPERT_ROWS = 256
VMEM_LIMIT = 56 * 1024 * 1024

NT = (((1,), (1,)), ((), ()))
TN = (((0,), (0,)), ((), ()))
NN = (((1,), (0,)), ((), ()))


def _params(*sem):
    return pltpu.CompilerParams(dimension_semantics=sem, vmem_limit_bytes=VMEM_LIMIT)


def _layer_spec(a, layer):
    return pl.BlockSpec((1,) + a.shape[1:], lambda *_: (layer,) + (0,) * (a.ndim - 1))


def _dg(a, b, dn=NN):
    return lax.dot_general(a, b, dn, preferred_element_type=F32)


def _split2(x):
    hi = x.astype(BF16)
    lo = (x - hi.astype(F32)).astype(BF16)
    return hi, lo


def _mm1(a, b, dn=NN):
    return _dg(a.astype(BF16), b.astype(BF16), dn)


def _mm3(a, b, dn=NN):
    ah, al = _split2(a)
    bh, bl = _split2(b)
    return _dg(ah, bh, dn) + (_dg(ah, bl, dn) + _dg(al, bh, dn))


def _mm_exact_lhs(a_bf16, b, dn=NN):
    hi, lo = _split2(b)
    return _dg(a_bf16, hi, dn) + _dg(a_bf16, lo, dn)


def _head_sum(x, seg):
    return _dg(x.astype(BF16), seg)


def _sigmoid(x):
    return 0.5 * jnp.tanh(0.5 * x) + 0.5


def _rms(x):
    return x * lax.rsqrt(jnp.mean(x * x, axis=-1, keepdims=True) + NORM_EPS)


def _ada_kernel(c_ref, w_ref, b_ref, o_ref):
    c = c_ref[...]
    s = c * _sigmoid(c)
    o_ref[0] = _mm3(s, w_ref[0]) + b_ref[0]


def _ada_table(cc, ada_w, ada_b):
    n_layers, d, six_d = ada_w.shape
    rows = cc.shape[0]
    tn = six_d // 4
    return pl.pallas_call(
        _ada_kernel,
        grid=(n_layers, six_d // tn),
        in_specs=[pl.BlockSpec((rows, d), lambda l, j: (0, 0)),
                  pl.BlockSpec((1, d, tn), lambda l, j: (l, 0, j)),
                  pl.BlockSpec((1, 1, tn), lambda l, j: (l, 0, j))],
        out_specs=pl.BlockSpec((1, rows, tn), lambda l, j: (l, 0, j)),
        out_shape=jax.ShapeDtypeStruct((n_layers, rows, six_d), F32),
        compiler_params=_params("parallel", "parallel"),
        name="ada_table",
    )(cc, ada_w, ada_b.reshape(n_layers, 1, six_d))


def _swap16(x):
    w = x.shape[-1]
    lane = lax.broadcasted_iota(jnp.int32, x.shape, x.ndim - 1)
    return jnp.where((lane & 16) == 0, pltpu.roll(x, w - 16, x.ndim - 1), pltpu.roll(x, 16, x.ndim - 1))


def _att_heads(a, cos, sin, q_gain, k_gain, seg):
    def norm_rope(u, gain, cos_u, sin_u, seg_u):
        ms = _head_sum(u * u, seg_u) * (1.0 / HEAD_DIM)
        un = u * lax.rsqrt(ms + NORM_EPS) * gain
        return un * cos_u + _swap16(un) * sin_u

    kw = ATT_KV_WIDTH
    q = norm_rope(a[:, :ATT_WIDTH], q_gain, cos, sin, seg)
    k = norm_rope(a[:, ATT_WIDTH:ATT_WIDTH + kw], k_gain, cos[:, :kw], sin[:, :kw], seg[:kw, :kw])
    q = (q * (HEAD_DIM ** -0.5 * LOG2_E)).astype(BF16)
    ones = jnp.ones((a.shape[0], HEAD_DIM), BF16)
    v = a[:, ATT_WIDTH + kw:].astype(BF16)
    v1 = jnp.concatenate([t for h in range(ATT_KV_HEADS) for t in (v[:, h * HEAD_DIM:(h + 1) * HEAD_DIM], ones)],
                         axis=1)
    return q, k.astype(BF16), v1


def _in_proj_kernel(x_ref, gain_ref, mt_ref, cos_ref, sin_ref, qg_ref, kg_ref, seg_ref, wa_ref, wr_ref, wc_ref, wg_ref,
                    qvk_ref, rw_ref, cv_ref, gt_ref):
    for i in range(x_ref.shape[0]):
        h = _rms(x_ref[i]) * gain_ref[...]
        h = (h * (1.0 + mt_ref[i, 0, 1:2]) + mt_ref[i, 0, 0:1]).astype(BF16)
        q, k, v1 = _att_heads(_dg(h, wa_ref[0]), cos_ref[...], sin_ref[...], qg_ref[...], kg_ref[...], seg_ref[...])
        qvk_ref[i] = jnp.concatenate([q, v1, k], axis=1)
        rw_ref[i] = _dg(h, wr_ref[0]).astype(BF16)
        cv_ref[i] = _dg(h, wc_ref[0]).astype(BF16)
        gt_ref[i] = _dg(h, wg_ref[0]).astype(BF16)


def _in_proj(xc, gain, mt, cos_t, sin_t, q_gain_t, k_gain_t, seg, layer, weights, n_lat_tiles):
    b, s, d = xc.shape
    nb = TOKEN_BATCH
    tok = lambda n: pl.BlockSpec((nb, ROW_TILE, n), lambda i, j: (i, j, 0))
    full = lambda a: pl.BlockSpec(a.shape, lambda i, j: (0,) * a.ndim)
    table = pl.BlockSpec((ROW_TILE, ATT_WIDTH), lambda i, j: (j, 0))
    widths = [ATT_WIDTH + 3 * ATT_KV_WIDTH] + [w.shape[2] for w in weights[1:]]
    return pl.pallas_call(
        _in_proj_kernel,
        grid=(b // nb, s // ROW_TILE),
        in_specs=[tok(d), full(gain), pl.BlockSpec((nb, 1, 6, d), lambda i, j: (i, j // n_lat_tiles, 0, 0)),
                  table, table, full(q_gain_t), full(k_gain_t), full(seg)]
                 + [_layer_spec(w, layer) for w in weights],
        out_specs=[tok(n) for n in widths],
        out_shape=[jax.ShapeDtypeStruct((b, s, n), BF16) for n in widths],
        compiler_params=_params("parallel", "parallel"),
        name="in_proj",
    )(xc, gain, mt, cos_t, sin_t, q_gain_t, k_gain_t, seg, *weights)


def _attn_kernel(q_ref, k_ref, v_ref, *rest):
    o_ref = rest[-1]

    def scores(h):
        kvh = h // ATT_GROUP
        return _dg(q_ref[0, :, h * HEAD_DIM:(h + 1) * HEAD_DIM], k_ref[0, :, kvh * HEAD_DIM:(kvh + 1) * HEAD_DIM], NT)

    s_next = scores(0)
    for h in range(ATT_HEADS):
        s = s_next
        if h + 1 < ATT_HEADS:
            s_next = scores(h + 1)
        kvh = h // ATT_GROUP
        v1 = v_ref[0, :, 2 * kvh * HEAD_DIM:2 * (kvh + 1) * HEAD_DIM]
        p = jnp.exp2((s - jnp.max(s, axis=-1, keepdims=True)).astype(BF16))
        ov = _dg(p, v1)
        o = ov[:, :HEAD_DIM] / ov[:, HEAD_DIM:]
        o_ref[0, :, h * HEAD_DIM:(h + 1) * HEAD_DIM] = o.astype(BF16)


def _attention(qvk, n_lat):
    b, s, _ = qvk.shape
    n_ctx = s - n_lat
    v_w = 2 * ATT_KV_WIDTH
    k_col, v_col = (ATT_WIDTH + v_w) // ATT_KV_WIDTH, ATT_WIDTH // v_w
    tile = ATT_TILE if n_lat % ATT_TILE == 0 else ROW_TILE
    out = jax.ShapeDtypeStruct((b, s, ATT_WIDTH), BF16)
    o_lat = pl.pallas_call(
        _attn_kernel,
        grid=(b, n_lat // tile),
        in_specs=[pl.BlockSpec((1, tile, ATT_WIDTH), lambda i, j: (i, j, 0)),
                  pl.BlockSpec((1, s, ATT_KV_WIDTH), lambda i, j: (i, 0, k_col)),
                  pl.BlockSpec((1, s, v_w), lambda i, j: (i, 0, v_col))],
        out_specs=pl.BlockSpec((1, tile, ATT_WIDTH), lambda i, j: (i, j, 0)),
        out_shape=out,
        compiler_params=_params("parallel", "parallel"),
        name="attention",
    )(qvk, qvk, qvk)
    ctx_blk = n_lat // n_ctx
    return pl.pallas_call(
        _attn_kernel,
        grid=(b,),
        in_specs=[pl.BlockSpec((1, n_ctx, ATT_WIDTH), lambda i: (i, ctx_blk, 0)),
                  pl.BlockSpec((1, n_ctx, ATT_KV_WIDTH), lambda i: (i, ctx_blk, k_col)),
                  pl.BlockSpec((1, n_ctx, v_w), lambda i: (i, ctx_blk, v_col)),
                  pl.BlockSpec(memory_space=pl.ANY)],
        out_specs=pl.BlockSpec((1, n_ctx, ATT_WIDTH), lambda i: (i, ctx_blk, 0)),
        out_shape=out,
        input_output_aliases={3: 0},
        compiler_params=_params("parallel"),
        name="attention_ctx",
    )(qvk, qvk, qvk, o_lat)


def _neighbours(x, halo_prev, halo_next, first, last):
    rows = x.shape[0]
    ridx = lax.broadcasted_iota(jnp.int32, x.shape, 0)
    row_p = jnp.where(first, 0.0, halo_prev[halo_prev.shape[0] - 1:])
    row_n = jnp.where(last, 0.0, halo_next[0:1])
    prev = jnp.where(ridx == 0, row_p, pltpu.roll(x, 1, 0))
    nxt = jnp.where(ridx == rows - 1, row_n, pltpu.roll(x, rows - 1, 0))
    return prev, nxt


def _tile_ends(j, n_lat_tiles, n_tiles):
    first = jnp.logical_or(j == 0, j == n_lat_tiles)
    last = jnp.logical_or(j == n_lat_tiles - 1, j == n_tiles - 1)
    return first, last


def _halo_specs(width, n_tiles, dtype):
    halo = HALO * (4 // jnp.dtype(dtype).itemsize)
    per = ROW_TILE // halo
    nb = TOKEN_BATCH
    prev = pl.BlockSpec((nb, halo, width), lambda i, j: (i, jnp.maximum(j * per - 1, 0), 0))
    nxt = pl.BlockSpec((nb, halo, width), lambda i, j: (i, jnp.minimum((j + 1) * per, n_tiles * per - 1), 0))
    return prev, nxt


def _rwkv_prep_kernel(x_ref, hp_ref, hn_ref, mu_ref, w2_ref, w0_ref, a2_ref, a0_ref, g2_ref, kk_ref, ka_ref,
                      seg_ref, rv_ref, kg_ref, nkk_ref, lw_ref, kd_ref, bb_ref, *, n_lat_tiles, n_tiles):
    j = pl.program_id(1)
    first, last = _tile_ends(j, n_lat_tiles, n_tiles)
    w = RWKV_WIDTH
    halo = hp_ref.shape[1]
    rows = x_ref.shape[1]
    t_idx = lax.broadcasted_iota(jnp.int32, (rows, rows + 2 * halo), 0) + halo
    u_idx = lax.broadcasted_iota(jnp.int32, (rows, rows + 2 * halo), 1)
    is_prev = jnp.logical_and(u_idx == t_idx - 1, jnp.logical_or(u_idx >= halo, jnp.logical_not(first)))
    is_next = jnp.logical_and(u_idx == t_idx + 1, jnp.logical_or(u_idx < halo + rows, jnp.logical_not(last)))
    band = jnp.where(jnp.logical_or(is_prev, is_next), 0.5, 0.0).astype(BF16)
    for i in range(x_ref.shape[0]):
        x = x_ref[i].astype(F32)
        mean_nb = _dg(band, jnp.concatenate([hp_ref[i], x_ref[i], hn_ref[i]], axis=0))
        xs = x + mu_ref[...] * (mean_nb - x)
        r = xs[:, 0:w]
        k = xs[:, w:2 * w]
        v = xs[:, 2 * w:3 * w]
        o = 3 * w
        w_lo = xs[:, o:o + 2 * DECAY_RANK]
        a_lo = xs[:, o + 2 * DECAY_RANK:o + 2 * DECAY_RANK + 2 * ICLR_RANK]
        g_lo = xs[:, o + 2 * DECAY_RANK + 2 * ICLR_RANK:]
        w_pre = w0_ref[...] + _mm1(jnp.tanh(w_lo), w2_ref[...])
        logw = -_sigmoid(w_pre) * float(np.exp(-0.5))
        a = _sigmoid(a0_ref[...] + _mm1(a_lo, a2_ref[...]))
        g = _mm1(_sigmoid(g_lo), g2_ref[...])
        kk = k * kk_ref[...]
        kk = kk * lax.rsqrt(_head_sum(kk * kk, seg_ref[...]) + 1e-12)
        for c, (t_scan, t_tail) in enumerate(((r, k), (v, g))):
            rv_ref[i, :, c * w:(c + 1) * w] = t_scan.astype(BF16)
            kg_ref[i, :, c * w:(c + 1) * w] = t_tail.astype(BF16)
        nkk_ref[i] = (-kk).astype(BF16)
        for d in range(2):
            a_d = a[:, d * w:(d + 1) * w]
            lw_ref[i, d] = logw[:, d * w:(d + 1) * w]
            kd_ref[i, d] = (k * (1.0 + (a_d - 1.0) * ka_ref[...])).astype(BF16)
            bb_ref[i, d] = (kk * a_d).astype(BF16)


def _rwkv_prep(rw, mu, w2cat, w0, a2cat, a0, g2, k_k, k_a, seg, n_lat_tiles):
    b, s, wseg = rw.shape
    n_tiles = s // ROW_TILE
    w = RWKV_WIDTH
    nb = TOKEN_BATCH
    hp, hn = _halo_specs(wseg, n_tiles, rw.dtype)
    full = lambda shape: pl.BlockSpec(shape, lambda i, j: (0,) * len(shape))
    tok = pl.BlockSpec((nb, ROW_TILE, w), lambda i, j: (i, j, 0))
    tok2 = pl.BlockSpec((nb, 2, ROW_TILE, w), lambda i, j: (i, 0, j, 0))
    pair = pl.BlockSpec((nb, ROW_TILE, 2 * w), lambda i, j: (i, j, 0))
    one = jax.ShapeDtypeStruct((b, s, w), BF16)
    two = jax.ShapeDtypeStruct((b, 2, s, w), BF16)
    two_f32 = jax.ShapeDtypeStruct((b, 2, s, w), F32)
    return pl.pallas_call(
        functools.partial(_rwkv_prep_kernel, n_lat_tiles=n_lat_tiles, n_tiles=n_tiles),
        grid=(b // nb, n_tiles),
        in_specs=[pl.BlockSpec((nb, ROW_TILE, wseg), lambda i, j: (i, j, 0)), hp, hn,
                  full((1, wseg)), full(w2cat.shape), full((1, 2 * w)), full(a2cat.shape), full((1, 2 * w)),
                  full(g2.shape), full((1, w)), full((1, w)), full(seg.shape)],
        out_specs=[pair, pair, tok, tok2, tok2, tok2],
        out_shape=[jax.ShapeDtypeStruct((b, s, 2 * w), BF16)] * 2 + [one, two_f32, two, two],
        compiler_params=_params("parallel", "parallel"),
        name="rwkv_prep",
    )(rw, rw, rw, mu, w2cat, w0, a2cat, a0, g2, k_k, k_a, seg)


def _block_diag(y, head_masks):
    return jnp.concatenate([jnp.where(m, y, 0.0) for m in head_masks], axis=0)


def _scan_chain(sgn, r, v, nkk, lw, kd, bb, s0):
    row = lax.broadcasted_iota(jnp.int32, (CHUNK, QUAD), 0)
    lane = lax.broadcasted_iota(jnp.int32, (CHUNK, QUAD), 1)
    rel = ((lane % CHUNK) - row) * sgn
    strict = rel < 0
    incl = rel <= 0
    eye = jnp.where(rel == 0, 1.0, 0.0)
    head_masks = [(lane // HEAD_DIM) == h for h in range(4)]
    bd_mask = (lax.broadcasted_iota(jnp.int32, (QUAD, QUAD), 0) // HEAD_DIM
               == lax.broadcasted_iota(jnp.int32, (QUAD, QUAD), 1) // HEAD_DIM)
    t_row = lax.broadcasted_iota(jnp.int32, (CHUNK, CHUNK), 0)
    t_col = lax.broadcasted_iota(jnp.int32, (CHUNK, CHUNK), 1)
    tri = jnp.where((t_col - t_row) * sgn <= 0, 1.0, 0.0).astype(BF16)
    last_row = CHUNK - 1 if sgn > 0 else 0
    bd = lambda t: _block_diag(t, head_masks).astype(BF16)

    cum = _mm_exact_lhs(tri, lw)
    yield
    cum_end = cum[last_row:last_row + 1]
    w_inv = jnp.exp(-cum)
    w_rem = jnp.exp(cum_end - cum)
    ar = jnp.concatenate([nkk * jnp.exp(cum - lw), r * jnp.exp(cum)], axis=0).astype(BF16)
    xb = _dg(ar, bd(bb * w_inv), NT)
    xk = _dg(ar, bd(kd * w_inv), NT)
    ars = _dg(ar, s0.astype(BF16), NT)
    yield
    n_ab = jnp.where(strict, xb[:CHUNK], 0.0)
    l_ak = jnp.where(strict, xk[:CHUNK], 0.0)
    g_rb = jnp.where(incl, xb[CHUNK:], 0.0)
    g_rk = jnp.where(incl, xk[CHUNK:], 0.0)

    m = eye + n_ab
    p = _dg(n_ab.astype(BF16), bd(n_ab))
    lg = _dg(jnp.concatenate([l_ak, g_rk], axis=0).astype(BF16), bd(v))
    yield
    for _ in range(int(np.log2(CHUNK)) - 2):
        both = _dg(jnp.concatenate([p, m], axis=0).astype(BF16), bd(p))
        yield
        p = both[:CHUNK]
        m = m + both[CHUNK:]
    m = m + _dg(m.astype(BF16), bd(p))
    yield
    x0 = ars[:CHUNK] + lg[:CHUNK]
    u = _dg(m.astype(BF16), bd(x0))
    yield
    y = ars[CHUNK:] + _dg(g_rb.astype(BF16), bd(u)) + lg[CHUNK:]
    z = _dg(jnp.concatenate([u, v], axis=0).astype(BF16),
            jnp.concatenate([bb * w_rem, kd * w_rem], axis=0).astype(BF16), TN)
    yield
    return y, s0 * jnp.exp(cum_end) + jnp.where(bd_mask, z, 0.0)


def _run_interleaved(chains, stagger=0):
    results = [None] * len(chains)
    live = list(enumerate(chains))
    rnd = 0
    while live:
        still = []
        for idx, g in live:
            if rnd >= idx * stagger:
                try:
                    next(g)
                except StopIteration as stop:
                    results[idx] = stop.value
                    continue
            still.append((idx, g))
        live = still
        rnd += 1
    return results


def _rwkv_scan_kernel(pf_ref, nf_ref, lwf_ref, kdf_ref, bbf_ref, pb_ref, nb_ref, lwb_ref, kdb_ref, bbb_ref,
                      yf_ref, yb_ref, s_ref):
    @pl.when(pl.program_id(1) == 0)
    def _():
        s_ref[...] = jnp.zeros_like(s_ref)

    w = RWKV_WIDTH
    f32 = lambda t: t.astype(F32)
    dirs = ((1, pf_ref, nf_ref, lwf_ref, kdf_ref, bbf_ref, yf_ref),
            (-1, pb_ref, nb_ref, lwb_ref, kdb_ref, bbb_ref, yb_ref))
    keys = [(i, d, q) for i in range(SCAN_BATCH) for d in range(2) for q in range(w // QUAD)]
    state = {key: s_ref[key] for key in keys}
    for step in range(SCAN_CHUNKS):
        work = []
        for i, d, q in keys:
            sgn, p_ref, n_ref, lw_ref, kd_ref, bb_ref, y_ref = dirs[d]
            c = step if sgn > 0 else SCAN_CHUNKS - 1 - step
            rows = slice(c * CHUNK, (c + 1) * CHUNK)
            ql = slice(q * QUAD, (q + 1) * QUAD)
            vl = slice(w + q * QUAD, w + (q + 1) * QUAD)
            args = (f32(p_ref[i, rows, ql]), f32(p_ref[i, rows, vl]), f32(n_ref[i, rows, ql]),
                    lw_ref[i, 0, rows, ql], f32(kd_ref[i, 0, rows, ql]), f32(bb_ref[i, 0, rows, ql]), state[(i, d, q)])
            work.append((sgn, args, y_ref, (i, rows, ql)))
        results = _run_interleaved([_scan_chain(sgn, *args) for sgn, args, _, _ in work])
        for key, (_, _, y_ref, y_idx), (y, s_new) in zip(keys, work, results):
            y_ref[y_idx] = y.astype(y_ref.dtype)
            state[key] = s_new
    for key in keys:
        s_ref[key] = state[key]


def _rwkv_scan(rv, nkk, lw, kd, bb, n_lat):
    b, s, w = nkk.shape
    rows = SCAN_CHUNKS * CHUNK
    assert s % rows == 0 and n_lat % rows == 0 and b % SCAN_BATCH == 0
    nc = s // rows
    nc_lat = n_lat // rows
    nc_ctx = nc - nc_lat
    nb = SCAN_BATCH

    fwd = lambda c: jnp.where(c < nc_ctx, nc_lat + c, c - nc_ctx)
    bwd = lambda c: jnp.where(c < nc_ctx, nc - 1 - c, nc_lat - 1 - (c - nc_ctx))
    tok = lambda order, a: pl.BlockSpec((nb, rows, a.shape[2]), lambda i, c: (i, order(c), 0))
    tok2 = lambda order, d: pl.BlockSpec((nb, 1, rows, w), lambda i, c: (i, d, order(c), 0))
    out = jax.ShapeDtypeStruct((b, s, w), BF16)
    return pl.pallas_call(
        _rwkv_scan_kernel,
        grid=(b // nb, nc),
        in_specs=[tok(fwd, rv), tok(fwd, nkk), tok2(fwd, 0), tok2(fwd, 0), tok2(fwd, 0),
                  tok(bwd, rv), tok(bwd, nkk), tok2(bwd, 1), tok2(bwd, 1), tok2(bwd, 1)],
        out_specs=[tok(fwd, nkk), tok(bwd, nkk)],
        out_shape=[out, out],
        scratch_shapes=[pltpu.VMEM((nb, 2, w // QUAD, QUAD, QUAD), F32)],
        compiler_params=_params("parallel", "arbitrary"),
        name="rwkv_scan",
    )(rv, nkk, lw, kd, bb, rv, nkk, lw, kd, bb)


def _rwkv_readout(y, r, k, v, g, r_k, gn_w, gn_b, seg):
    mean = _head_sum(y, seg) * (1.0 / HEAD_DIM)
    yc = y - mean
    var = _head_sum(yc * yc, seg) * (1.0 / HEAD_DIM)
    yn = yc * lax.rsqrt(var + RWKV_GN_EPS)
    bonus = _head_sum(r * k * r_k, seg) * v
    return (yn * gn_w + gn_b + bonus) * g


def _route(x, gain, shift, scale, w_router_t):
    h = _rms(x) * gain
    h = h * (1.0 + scale) + shift
    logits = _mm3(w_router_t, h, NT)
    e = jnp.exp(logits - jnp.max(logits, axis=0, keepdims=True))
    return h, e / jnp.sum(e, axis=0, keepdims=True)


def _merge_kernel(x_ref, oa_ref, yf_ref, yb_ref, rv_ref, kg_ref, cv_ref, hp_ref, hn_ref, gt_ref, mt_ref,
                  rk_ref, gw_ref, gb_ref, seg_ref, cw_ref, gain2_ref, wrt_ref, wa_ref, wr_ref, wc_ref, wo_ref,
                  o_ref, h_ref, aff_ref, *, n_lat_tiles, n_tiles):
    j = pl.program_id(1)
    first, last = _tile_ends(j, n_lat_tiles, n_tiles)
    cw = CONV_WIDTH
    d = x_ref.shape[2]
    w = RWKV_WIDTH
    def sample(i):
        r, v = [rv_ref[i, :, c * w:(c + 1) * w].astype(F32) for c in range(2)]
        k, g = [kg_ref[i, :, c * w:(c + 1) * w].astype(F32) for c in range(2)]
        o_rw = _rwkv_readout(yf_ref[i].astype(F32) + yb_ref[i].astype(F32), r, k, v, g, rk_ref[...], gw_ref[...],
                             gb_ref[...], seg_ref[...])
        yield
        cv = cv_ref[i].astype(F32)
        hp = hp_ref[i].astype(F32)
        hn = hn_ref[i].astype(F32)
        z = cv[:, cw:2 * cw] * cv[:, 2 * cw:]
        zp = hp[:, cw:2 * cw] * hp[:, 2 * cw:]
        zn = hn[:, cw:2 * cw] * hn[:, 2 * cw:]
        z_prev, z_next = _neighbours(z, zp, zn, first, last)
        o_cv = cv[:, :cw] * (cw_ref[0:1] * z_prev + cw_ref[1:2] * z + cw_ref[2:3] * z_next)
        yield
        t = jnp.tanh(gt_ref[i].astype(F32))
        m = ((1.0 + t[:, :d]) * _dg(oa_ref[i], wa_ref[0])
             + (1.0 + t[:, d:2 * d]) * _dg(o_rw.astype(BF16), wr_ref[0])
             + (1.0 + t[:, 2 * d:]) * _dg(o_cv.astype(BF16), wc_ref[0]))
        yield
        y = _dg(m.astype(BF16), wo_ref[0])
        x_new = x_ref[i] + mt_ref[i, 0, 2:3] * y
        o_ref[i] = x_new
        yield
        h, aff = _route(x_new, gain2_ref[...], mt_ref[i, 0, 3:4], mt_ref[i, 0, 4:5], wrt_ref[...])
        h_ref[i] = h.astype(BF16)
        aff_ref[i] = aff

    _run_interleaved([sample(i) for i in range(x_ref.shape[0])], stagger=2)


def _merge(xc, o_att, y_f, y_b, rv, kg, cv, gt, mt, r_k, gn_w, gn_b, seg, conv_w, gain2, w_router_t, layer, stacked,
           n_lat_tiles):
    b, s, d = xc.shape
    n_tiles = s // ROW_TILE
    ne = w_router_t.shape[0]
    nb = TOKEN_BATCH
    hp, hn = _halo_specs(cv.shape[2], n_tiles, cv.dtype)
    tok = lambda a: pl.BlockSpec((nb, ROW_TILE, a.shape[2]), lambda i, j: (i, j, 0))
    full = lambda a: pl.BlockSpec(a.shape, lambda i, j: (0,) * a.ndim)
    consts = (r_k, gn_w, gn_b, seg, conv_w, gain2, w_router_t)
    return pl.pallas_call(
        functools.partial(_merge_kernel, n_lat_tiles=n_lat_tiles, n_tiles=n_tiles),
        grid=(b // nb, n_tiles),
        in_specs=[tok(a) for a in (xc, o_att, y_f, y_b, rv, kg, cv)] + [hp, hn, tok(gt),
                  pl.BlockSpec((nb, 1, 6, d), lambda i, j: (i, j // n_lat_tiles, 0, 0))]
                 + [full(a) for a in consts] + [_layer_spec(a, layer) for a in stacked],
        out_specs=[tok(xc), tok(xc), pl.BlockSpec((nb, ne, ROW_TILE), lambda i, j: (i, 0, j))],
        out_shape=[jax.ShapeDtypeStruct((b, s, d), F32), jax.ShapeDtypeStruct((b, s, d), BF16),
                   jax.ShapeDtypeStruct((b, ne, s), F32)],
        compiler_params=_params("parallel", "parallel"),
        name="merge",
    )(xc, o_att, y_f, y_b, rv, kg, cv, cv, cv, gt, mt, *consts, *stacked)


def _select_kernel(aff_ref, pos_ref, gate_ref, *, cap):
    nb, n_exp, n = aff_ref.shape
    a = aff_ref[...].reshape(nb * n_exp, n)
    ne = nb * n_exp
    bits = pltpu.bitcast(a, jnp.int32)

    def count(mask):
        return jnp.sum(jnp.where(mask, 1.0, 0.0), axis=1, keepdims=True)

    def body(_, carry):
        lo, hi = carry
        mid = lo + ((hi - lo + 1) >> 1)
        ok = count(bits >= mid) >= cap
        return jnp.where(ok, mid, lo), jnp.where(ok, hi, mid - 1)

    lo0 = jnp.zeros((ne, 1), jnp.int32)
    hi0 = jnp.full((ne, 1), 0x7F800000, jnp.int32)
    thr, _ = lax.fori_loop(0, 32, body, (lo0, hi0))
    gt = bits > thr
    eq = bits == thr
    need = cap - count(gt)

    def tokens_before(mask):
        m = jnp.where(mask, 1.0, 0.0).astype(BF16)
        blk = min(n, SELECT_BLOCK)
        cols = []
        for j in range(n // blk):
            s_idx = lax.broadcasted_iota(jnp.int32, (n, blk), 0)
            t_idx = lax.broadcasted_iota(jnp.int32, (n, blk), 1) + j * blk
            cols.append(_dg(m, jnp.where(s_idx < t_idx, 1.0, 0.0).astype(BF16)))
        return jnp.concatenate(cols, axis=1)

    sel = jnp.logical_or(gt, jnp.logical_and(eq, tokens_before(eq) < need))
    pos_ref[...] = jnp.where(sel, tokens_before(sel).astype(jnp.int32), -1).reshape(nb, n_exp, n)
    gate_ref[...] = jnp.where(sel, a, 0.0).reshape(nb, n_exp, n)


def _select(aff_t, tok0, n, cap):
    b, ne, _ = aff_t.shape
    blk = tok0 // n
    nb = SELECT_BATCH if b % SELECT_BATCH == 0 else 1
    return pl.pallas_call(
        functools.partial(_select_kernel, cap=cap),
        grid=(b // nb,),
        in_specs=[pl.BlockSpec((nb, ne, n), lambda i: (i, 0, blk))],
        out_specs=[pl.BlockSpec((nb, ne, n), lambda i: (i, 0, 0)), pl.BlockSpec((nb, ne, n), lambda i: (i, 0, 0))],
        out_shape=[jax.ShapeDtypeStruct((b, ne, n), jnp.int32), jax.ShapeDtypeStruct((b, ne, n), F32)],
        compiler_params=_params("parallel"),
        name="moe_select",
    )(aff_t)


def _expert_kernel(h_ref, pos_ref, gate_ref, wg_ref, wu_ref, wd_ref, o_ref, *, cap):
    e = pl.program_id(1)

    @pl.when(e == 0)
    def _():
        o_ref[...] = jnp.zeros_like(o_ref)

    nb, n, _ = h_ref.shape
    slot = lax.broadcasted_iota(jnp.int32, (cap, n), 0)
    onehots, xes, gates = [], [], []
    for i in range(nb):
        hit = pos_ref[i, 0] == slot
        onehot = jnp.where(hit, 1.0, 0.0).astype(BF16)
        onehots.append(onehot)
        xes.append(_dg(onehot, h_ref[i]).astype(BF16))
        gates.append(jnp.sum(jnp.where(hit, gate_ref[i, 0], 0.0), axis=1, keepdims=True))
    xe = jnp.concatenate(xes, axis=0)
    hg = _dg(xe, wg_ref[0])
    hu = _dg(xe, wu_ref[0])
    hid = (hg * _sigmoid(hg) * hu).astype(BF16)
    ye = (_dg(hid, wd_ref[0]) * jnp.concatenate(gates, axis=0)).astype(BF16)
    for i in range(nb):
        o_ref[i] += _dg(onehots[i], ye[i * cap:(i + 1) * cap], TN)


def _experts(h, pos, gate, wg, wu, wd, layer, tok0, n, cap):
    b, s, d = h.shape
    ne = pos.shape[1]
    f = wg.shape[2]
    blk = tok0 // n
    nb = max(1, min(b // 2, EXPERT_ROWS // cap))
    assert b % nb == 0
    sel = pl.BlockSpec((nb, 1, 1, n), lambda i, e: (i, e, 0, 0))
    return pl.pallas_call(
        functools.partial(_expert_kernel, cap=cap),
        grid=(b // nb, ne),
        in_specs=[pl.BlockSpec((nb, n, d), lambda i, e: (i, blk, 0)), sel, sel,
                  pl.BlockSpec((1, d, f), lambda i, e: (layer * ne + e, 0, 0)),
                  pl.BlockSpec((1, d, f), lambda i, e: (layer * ne + e, 0, 0)),
                  pl.BlockSpec((1, f, d), lambda i, e: (layer * ne + e, 0, 0))],
        out_specs=pl.BlockSpec((nb, n, d), lambda i, e: (i, 0, 0)),
        out_shape=jax.ShapeDtypeStruct((b, n, d), F32),
        compiler_params=_params("parallel", "arbitrary"),
        name="moe_experts",
    )(h, pos.reshape(b, ne, 1, n), gate.reshape(b, ne, 1, n), wg, wu, wd)


def _residual_kernel(x_ref, m_ref, mt_ref, o_ref):
    o_ref[0] = x_ref[0] + mt_ref[0, 0, 5:6] * m_ref[0]


def _moe_residual(xc, moe, mt, tok0, stream):
    b, s, d = xc.shape
    n = moe.shape[1]
    t0 = tok0 // ROW_TILE
    return pl.pallas_call(
        _residual_kernel,
        grid=(b, n // ROW_TILE),
        in_specs=[pl.BlockSpec((1, ROW_TILE, d), lambda i, j: (i, t0 + j, 0)),
                  pl.BlockSpec((1, ROW_TILE, d), lambda i, j: (i, j, 0)),
                  pl.BlockSpec((1, 1, 6, d), lambda i, j: (i, stream, 0, 0))],
        out_specs=pl.BlockSpec((1, ROW_TILE, d), lambda i, j: (i, t0 + j, 0)),
        out_shape=jax.ShapeDtypeStruct((b, s, d), F32),
        input_output_aliases={0: 0},
        compiler_params=_params("parallel", "parallel"),
        name="moe_residual",
    )(xc, moe, mt)


def _final_kernel(x_ref, m_ref, mt_ref, g_ref, o_ref):
    o_ref[0] = _rms(x_ref[0] + mt_ref[0, 0, 5:6] * m_ref[0]) * g_ref[...]


def _final_residual_norm(xc, moe, mt, gain):
    b, s, d = xc.shape
    n_lat = moe.shape[1]
    tok = pl.BlockSpec((1, ROW_TILE, d), lambda i, j: (i, j, 0))
    return pl.pallas_call(
        _final_kernel,
        grid=(b, n_lat // ROW_TILE),
        in_specs=[tok, tok, pl.BlockSpec((1, 1, 6, d), lambda i, j: (i, 0, 0, 0)),
                  pl.BlockSpec((1, d), lambda i, j: (0, 0))],
        out_specs=tok,
        out_shape=jax.ShapeDtypeStruct((b, n_lat, d), F32),
        compiler_params=_params("parallel", "parallel"),
        name="final_norm",
    )(xc, moe, mt, gain)


def _rope_tables(n_lat, n_ctx):
    rows = n_lat // GRID_W
    row = jnp.repeat(jnp.arange(rows, dtype=F32), GRID_W)
    col = jnp.tile(jnp.arange(GRID_W, dtype=F32), rows)
    axis_dim = HEAD_DIM // 2
    inv_freq = ROPE_THETA ** (-jnp.arange(0, axis_dim, 2, dtype=F32) / axis_dim)
    ang_r = row[:, None] * inv_freq[None, :]
    ang_c = col[:, None] * inv_freq[None, :]
    cos_h = jnp.concatenate([jnp.cos(ang_r), jnp.cos(ang_r), jnp.cos(ang_c), jnp.cos(ang_c)], axis=1)
    sin_h = jnp.concatenate([-jnp.sin(ang_r), jnp.sin(ang_r), -jnp.sin(ang_c), jnp.sin(ang_c)], axis=1)
    cos_h = jnp.concatenate([cos_h, jnp.ones((n_ctx, HEAD_DIM), F32)], axis=0)
    sin_h = jnp.concatenate([sin_h, jnp.zeros((n_ctx, HEAD_DIM), F32)], axis=0)
    return jnp.tile(cos_h, (1, ATT_HEADS)), jnp.tile(sin_h, (1, ATT_HEADS))


def _head_sum_matrix(width):
    idx = np.arange(width) // HEAD_DIM
    return jnp.asarray(idx[:, None] == idx[None, :], dtype=BF16)


def _two_dir_lowrank(w2):
    _, rank, w = w2.shape
    z = jnp.zeros((rank, w), w2.dtype)
    return jnp.concatenate([jnp.concatenate([w2[0], z], axis=1), jnp.concatenate([z, w2[1]], axis=1)], axis=0)


def kernel(x, c, ctx, c_ctx, ada_w, ada_b, norm1, w_in, q_gain, k_gain, shift_mu, decay_w0, decay_w2, iclr_a0, iclr_a2, gate_g2, rwkv_kk, rwkv_ka, rwkv_rk, rwkv_gn_w, rwkv_gn_b, conv_w, w_br_att, w_br_rwkv, w_br_conv, w_out, norm2, w_router, exp_gate, exp_up, exp_down, final_norm):
    b, n_lat, d = x.shape
    n_ctx = ctx.shape[1]
    depth = ada_w.shape[0]
    assert n_lat % ROW_TILE == 0 and n_ctx % ROW_TILE == 0 and n_lat % n_ctx == 0
    n_lat_tiles = n_lat // ROW_TILE

    pad = (-(b + 1)) % 8
    cc = jnp.concatenate([c, c_ctx[None, :], jnp.zeros((pad, d), F32)], axis=0)
    mods = _ada_table(cc, ada_w, ada_b)

    cos_t, sin_t = _rope_tables(n_lat, n_ctx)
    seg = _head_sum_matrix(ATT_WIDTH)
    att_w = ATT_WIDTH + 2 * ATT_KV_WIDTH
    cv_w = 3 * CONV_WIDTH
    offs = np.cumsum([0, att_w, RWKV_SEG, cv_w, 3 * d])

    group_scale = (1.0, 1.0, 1.0, 0.5)
    w_groups = [(w_in[:, :, offs[i]:offs[i + 1]] * group_scale[i]).astype(BF16) for i in range(4)]
    w_tail = [t.astype(BF16) for t in (w_br_att, w_br_rwkv, w_br_conv, 0.5 * w_out)]
    ne, f = exp_gate.shape[1], exp_gate.shape[3]
    wg = exp_gate.astype(BF16).reshape(depth * ne, d, f)
    wu = exp_up.astype(BF16).reshape(depth * ne, d, f)
    wd = exp_down.astype(BF16).reshape(depth * ne, f, d)

    xc = jnp.concatenate([x, ctx], axis=1)
    for l in range(depth):
        mod_lat = mods[l, :b].reshape(b, 1, 6, d)
        mod_ctx = jnp.broadcast_to(mods[l, b].reshape(1, 1, 6, d), (b, 1, 6, d))
        mt = jnp.concatenate([mod_lat, mod_ctx], axis=1)

        gain1 = norm1[l].reshape(1, d)
        qvk, rw, cv, gt = _in_proj(
            xc, gain1, mt, cos_t, sin_t, jnp.tile(q_gain[l], ATT_HEADS).reshape(1, -1),
            jnp.tile(k_gain[l], ATT_KV_HEADS).reshape(1, -1), seg, l, w_groups, n_lat_tiles)
        o_att = _attention(qvk, n_lat)

        rv, kg, nkk, lw, kd, bb = _rwkv_prep(
            rw, shift_mu[l].reshape(1, -1), _two_dir_lowrank(decay_w2[l]), decay_w0[l].reshape(1, -1),
            _two_dir_lowrank(iclr_a2[l]), iclr_a0[l].reshape(1, -1), gate_g2[l],
            rwkv_kk[l].reshape(1, -1), rwkv_ka[l].reshape(1, -1), seg, n_lat_tiles)
        y_f, y_b = _rwkv_scan(rv, nkk, lw, kd, bb, n_lat)

        xc, h2, aff_t = _merge(
            xc, o_att, y_f, y_b, rv, kg, cv, gt, mt, rwkv_rk[l].reshape(1, -1), rwkv_gn_w[l].reshape(1, -1),
            rwkv_gn_b[l].reshape(1, -1), seg, conv_w[l], norm2[l].reshape(1, d), w_router[l].T, l, w_tail,
            n_lat_tiles)
        streams = [(0, n_lat, 0)] + ([(n_lat, n_ctx, 1)] if l < depth - 1 else [])
        for tok0, n, stream in streams:
            cap = CAPACITY_FACTOR * n // N_EXPERTS
            pos, gate = _select(aff_t, tok0, n, cap)
            moe = _experts(h2, pos, gate, wg, wu, wd, l, tok0, n, cap)
            if l == depth - 1:
                return _final_residual_norm(xc, moe, mt, final_norm.reshape(1, d))
            xc = _moe_residual(xc, moe, mt, tok0, stream)
```

```python
import functools

import numpy as np
import jax
import jax.numpy as jnp
from jax import lax
from jax.experimental import pallas as pl
from jax.experimental.pallas import tpu as pltpu

F32 = jnp.float32
BF16 = jnp.bfloat16

GRID_W = 64
NORM_EPS = 1e-6
LOG2_E = float(np.log2(np.e))
ATT_HEADS = 8
ATT_KV_HEADS = 2
HEAD_DIM = 64
ATT_GROUP = ATT_HEADS // ATT_KV_HEADS
ATT_WIDTH = ATT_HEADS * HEAD_DIM
ATT_KV_WIDTH = ATT_KV_HEADS * HEAD_DIM
ROPE_THETA = 10000.0
RWKV_HEADS = 8
RWKV_WIDTH = RWKV_HEADS * HEAD_DIM
DECAY_RANK = 64
ICLR_RANK = 64
GATE_RANK = 128
RWKV_GN_EPS = 64e-5
RWKV_SEG = 3 * RWKV_WIDTH + 2 * DECAY_RANK + 2 * ICLR_RANK + GATE_RANK
CONV_WIDTH = 512
N_EXPERTS = 16
CAPACITY_FACTOR = 2

ROW_TILE = 256
ATT_TILE = 512
CHUNK = 64
QUAD = 4 * HEAD_DIM
SCAN_BATCH = 4
SCAN_CHUNKS = 2
TOKEN_BATCH = 2
HALO = 8
SELECT_BLOCK = 256
SELECT_BATCH = 4
EXPERT_ROWS = 256
VMEM_LIMIT = 56 * 1024 * 1024

NT = (((1,), (1,)), ((), ()))
TN = (((0,), (0,)), ((), ()))
NN = (((1,), (0,)), ((), ()))


def _params(*sem):
    return pltpu.CompilerParams(dimension_semantics=sem, vmem_limit_bytes=VMEM_LIMIT)


def _layer_spec(a, layer):
    return pl.BlockSpec((1,) + a.shape[1:], lambda *_: (layer,) + (0,) * (a.ndim - 1))


def _dg(a, b, dn=NN):
    return lax.dot_general(a, b, dn, preferred_element_type=F32)


def _split2(x):
    hi = x.astype(BF16)
    lo = (x - hi.astype(F32)).astype(BF16)
    return hi, lo


def _mm1(a, b, dn=NN):
    return _dg(a.astype(BF16), b.astype(BF16), dn)


def _mm3(a, b, dn=NN):
    ah, al = _split2(a)
    bh, bl = _split2(b)
    return _dg(ah, bh, dn) + (_dg(ah, bl, dn) + _dg(al, bh, dn))


def _mm_exact_lhs(a_bf16, b, dn=NN):
    hi, lo = _split2(b)
    return _dg(a_bf16, hi, dn) + _dg(a_bf16, lo, dn)


def _head_sum(x, seg):
    return _dg(x.astype(BF16), seg)


def _sigmoid(x):
    return 0.5 * jnp.tanh(0.5 * x) + 0.5


def _rms(x):
    return x * lax.rsqrt(jnp.mean(x * x, axis=-1, keepdims=True) + NORM_EPS)


def _ada_kernel(c_ref, w_ref, b_ref, o_ref):
    c = c_ref[...]
    s = c * _sigmoid(c)
    o_ref[0] = _mm3(s, w_ref[0]) + b_ref[0]


def _ada_table(cc, ada_w, ada_b):
    n_layers, d, six_d = ada_w.shape
    rows = cc.shape[0]
    tn = six_d // 4
    return pl.pallas_call(
        _ada_kernel,
        grid=(n_layers, six_d // tn),
        in_specs=[pl.BlockSpec((rows, d), lambda l, j: (0, 0)),
                  pl.BlockSpec((1, d, tn), lambda l, j: (l, 0, j)),
                  pl.BlockSpec((1, 1, tn), lambda l, j: (l, 0, j))],
        out_specs=pl.BlockSpec((1, rows, tn), lambda l, j: (l, 0, j)),
        out_shape=jax.ShapeDtypeStruct((n_layers, rows, six_d), F32),
        compiler_params=_params("parallel", "parallel"),
        name="ada_table",
    )(cc, ada_w, ada_b.reshape(n_layers, 1, six_d))


def _swap16(x):
    w = x.shape[-1]
    lane = lax.broadcasted_iota(jnp.int32, x.shape, x.ndim - 1)
    return jnp.where((lane & 16) == 0, pltpu.roll(x, w - 16, x.ndim - 1), pltpu.roll(x, 16, x.ndim - 1))


def _att_heads(a, cos, sin, q_gain, k_gain, seg):
    def norm_rope(u, gain, cos_u, sin_u, seg_u):
        ms = _head_sum(u * u, seg_u) * (1.0 / HEAD_DIM)
        un = u * lax.rsqrt(ms + NORM_EPS) * gain
        return un * cos_u + _swap16(un) * sin_u

    kw = ATT_KV_WIDTH
    q = norm_rope(a[:, :ATT_WIDTH], q_gain, cos, sin, seg)
    k = norm_rope(a[:, ATT_WIDTH:ATT_WIDTH + kw], k_gain, cos[:, :kw], sin[:, :kw], seg[:kw, :kw])
    q = (q * (HEAD_DIM ** -0.5 * LOG2_E)).astype(BF16)
    ones = jnp.ones((a.shape[0], HEAD_DIM), BF16)
    v = a[:, ATT_WIDTH + kw:].astype(BF16)
    v1 = jnp.concatenate([t for h in range(ATT_KV_HEADS) for t in (v[:, h * HEAD_DIM:(h + 1) * HEAD_DIM], ones)],
                         axis=1)
    return q, k.astype(BF16), v1


def _in_proj_kernel(*refs, n_lat_tiles, has_pending):
    if has_pending:
        x_ref, ml_ref, mc_ref, mtp_ref = refs[:4]
        refs = refs[4:]
        xo_ref, refs = refs[-5], refs[:-5] + refs[-4:]
    else:
        x_ref, refs = refs[0], refs[1:]
    gain_ref, mt_ref, cos_ref, sin_ref, qg_ref, kg_ref, seg_ref, wa_ref, wr_ref, wc_ref, wg_ref = refs[:11]
    qvk_ref, rw_ref, cv_ref, gt_ref = refs[11:]
    is_latent = pl.program_id(1) < n_lat_tiles
    for i in range(x_ref.shape[0]):
        x = x_ref[i]
        if has_pending:
            x = x + mtp_ref[i, 0, 5:6] * jnp.where(is_latent, ml_ref[i], mc_ref[i])
            xo_ref[i] = x
        h = _rms(x) * gain_ref[...]
        h = (h * (1.0 + mt_ref[i, 0, 1:2]) + mt_ref[i, 0, 0:1]).astype(BF16)
        q, k, v1 = _att_heads(_dg(h, wa_ref[0]), cos_ref[...], sin_ref[...], qg_ref[...], kg_ref[...], seg_ref[...])
        qvk_ref[i] = jnp.concatenate([q, v1, k], axis=1)
        rw_ref[i] = _dg(h, wr_ref[0]).astype(BF16)
        cv_ref[i] = _dg(h, wc_ref[0]).astype(BF16)
        gt_ref[i] = _dg(h, wg_ref[0]).astype(BF16)


def _in_proj(xc, pending, gain, mt, cos_t, sin_t, q_gain_t, k_gain_t, seg, layer, weights, n_lat_tiles):
    b, s, d = xc.shape
    nb = TOKEN_BATCH if pending is None else 1
    tok = lambda n: pl.BlockSpec((nb, ROW_TILE, n), lambda i, j: (i, j, 0))
    full = lambda a: pl.BlockSpec(a.shape, lambda i, j: (0,) * a.ndim)
    table = pl.BlockSpec((ROW_TILE, ATT_WIDTH), lambda i, j: (j, 0))
    mods = lambda: pl.BlockSpec((nb, 1, 6, d), lambda i, j: (i, j // n_lat_tiles, 0, 0))
    widths = [ATT_WIDTH + 3 * ATT_KV_WIDTH] + [w.shape[2] for w in weights[1:]]
    in_specs, operands = [tok(d)], [xc]
    out_specs = [tok(n) for n in widths]
    out_shape = [jax.ShapeDtypeStruct((b, s, n), BF16) for n in widths]
    if pending is not None:
        moe_lat, moe_ctx, mt_prev = pending
        in_specs += [pl.BlockSpec((nb, ROW_TILE, d), lambda i, j: (i, jnp.minimum(j, n_lat_tiles - 1), 0)),
                     pl.BlockSpec((nb, ROW_TILE, d), lambda i, j: (i, jnp.maximum(j - n_lat_tiles, 0), 0)), mods()]
        operands += [moe_lat, moe_ctx, mt_prev]
        out_specs = [tok(d)] + out_specs
        out_shape = [jax.ShapeDtypeStruct((b, s, d), F32)] + out_shape
    in_specs += [full(gain), mods(), table, table, full(q_gain_t), full(k_gain_t), full(seg)]
    in_specs += [_layer_spec(w, layer) for w in weights]
    operands += [gain, mt, cos_t, sin_t, q_gain_t, k_gain_t, seg, *weights]
    outs = pl.pallas_call(
        functools.partial(_in_proj_kernel, n_lat_tiles=n_lat_tiles, has_pending=pending is not None),
        grid=(b // nb, s // ROW_TILE),
        in_specs=in_specs,
        out_specs=out_specs,
        out_shape=out_shape,
        compiler_params=_params("parallel", "parallel"),
        name="in_proj",
    )(*operands)
    return outs if pending is not None else [xc] + list(outs)


def _attn_kernel(q_ref, k_ref, v_ref, *rest):
    o_ref = rest[-1]

    def scores(h):
        kvh = h // ATT_GROUP
        return _dg(q_ref[0, :, h * HEAD_DIM:(h + 1) * HEAD_DIM], k_ref[0, :, kvh * HEAD_DIM:(kvh + 1) * HEAD_DIM], NT)

    s_next = scores(0)
    for h in range(ATT_HEADS):
        s = s_next
        if h + 1 < ATT_HEADS:
            s_next = scores(h + 1)
        kvh = h // ATT_GROUP
        v1 = v_ref[0, :, 2 * kvh * HEAD_DIM:2 * (kvh + 1) * HEAD_DIM]
        p = jnp.exp2((s - jnp.max(s, axis=-1, keepdims=True)).astype(BF16))
        ov = _dg(p, v1)
        o = ov[:, :HEAD_DIM] / ov[:, HEAD_DIM:]
        o_ref[0, :, h * HEAD_DIM:(h + 1) * HEAD_DIM] = o.astype(BF16)


def _attention(qvk, n_lat):
    b, s, _ = qvk.shape
    n_ctx = s - n_lat
    v_w = 2 * ATT_KV_WIDTH
    k_col, v_col = (ATT_WIDTH + v_w) // ATT_KV_WIDTH, ATT_WIDTH // v_w
    tile = ATT_TILE if n_lat % ATT_TILE == 0 else ROW_TILE
    out = jax.ShapeDtypeStruct((b, s, ATT_WIDTH), BF16)
    o_lat = pl.pallas_call(
        _attn_kernel,
        grid=(b, n_lat // tile),
        in_specs=[pl.BlockSpec((1, tile, ATT_WIDTH), lambda i, j: (i, j, 0)),
                  pl.BlockSpec((1, s, ATT_KV_WIDTH), lambda i, j: (i, 0, k_col)),
                  pl.BlockSpec((1, s, v_w), lambda i, j: (i, 0, v_col))],
        out_specs=pl.BlockSpec((1, tile, ATT_WIDTH), lambda i, j: (i, j, 0)),
        out_shape=out,
        compiler_params=_params("parallel", "parallel"),
        name="attention",
    )(qvk, qvk, qvk)
    ctx_blk = n_lat // n_ctx
    return pl.pallas_call(
        _attn_kernel,
        grid=(b,),
        in_specs=[pl.BlockSpec((1, n_ctx, ATT_WIDTH), lambda i: (i, ctx_blk, 0)),
                  pl.BlockSpec((1, n_ctx, ATT_KV_WIDTH), lambda i: (i, ctx_blk, k_col)),
                  pl.BlockSpec((1, n_ctx, v_w), lambda i: (i, ctx_blk, v_col)),
                  pl.BlockSpec(memory_space=pl.ANY)],
        out_specs=pl.BlockSpec((1, n_ctx, ATT_WIDTH), lambda i: (i, ctx_blk, 0)),
        out_shape=out,
        input_output_aliases={3: 0},
        compiler_params=_params("parallel"),
        name="attention_ctx",
    )(qvk, qvk, qvk, o_lat)


def _neighbours(x, halo_prev, halo_next, first, last):
    rows = x.shape[0]
    ridx = lax.broadcasted_iota(jnp.int32, x.shape, 0)
    row_p = jnp.where(first, 0.0, halo_prev[halo_prev.shape[0] - 1:])
    row_n = jnp.where(last, 0.0, halo_next[0:1])
    prev = jnp.where(ridx == 0, row_p, pltpu.roll(x, 1, 0))
    nxt = jnp.where(ridx == rows - 1, row_n, pltpu.roll(x, rows - 1, 0))
    return prev, nxt


def _tile_ends(j, n_lat_tiles, n_tiles):
    first = jnp.logical_or(j == 0, j == n_lat_tiles)
    last = jnp.logical_or(j == n_lat_tiles - 1, j == n_tiles - 1)
    return first, last


def _halo_specs(width, n_tiles, dtype):
    halo = HALO * (4 // jnp.dtype(dtype).itemsize)
    per = ROW_TILE // halo
    nb = TOKEN_BATCH
    prev = pl.BlockSpec((nb, halo, width), lambda i, j: (i, jnp.maximum(j * per - 1, 0), 0))
    nxt = pl.BlockSpec((nb, halo, width), lambda i, j: (i, jnp.minimum((j + 1) * per, n_tiles * per - 1), 0))
    return prev, nxt


def _rwkv_prep_kernel(x_ref, hp_ref, hn_ref, mu_ref, w2_ref, w0_ref, a2_ref, a0_ref, g2_ref, kk_ref, ka_ref,
                      seg_ref, rv_ref, kg_ref, nkk_ref, lw_ref, kd_ref, bb_ref, *, n_lat_tiles, n_tiles):
    j = pl.program_id(1)
    first, last = _tile_ends(j, n_lat_tiles, n_tiles)
    w = RWKV_WIDTH
    halo = hp_ref.shape[1]
    rows = x_ref.shape[1]
    t_idx = lax.broadcasted_iota(jnp.int32, (rows, rows + 2 * halo), 0) + halo
    u_idx = lax.broadcasted_iota(jnp.int32, (rows, rows + 2 * halo), 1)
    is_prev = jnp.logical_and(u_idx == t_idx - 1, jnp.logical_or(u_idx >= halo, jnp.logical_not(first)))
    is_next = jnp.logical_and(u_idx == t_idx + 1, jnp.logical_or(u_idx < halo + rows, jnp.logical_not(last)))
    band = jnp.where(jnp.logical_or(is_prev, is_next), 0.5, 0.0).astype(BF16)
    for i in range(x_ref.shape[0]):
        x = x_ref[i].astype(F32)
        mean_nb = _dg(band, jnp.concatenate([hp_ref[i], x_ref[i], hn_ref[i]], axis=0))
        xs = x + mu_ref[...] * (mean_nb - x)
        r = xs[:, 0:w]
        k = xs[:, w:2 * w]
        v = xs[:, 2 * w:3 * w]
        o = 3 * w
        w_lo = xs[:, o:o + 2 * DECAY_RANK]
        a_lo = xs[:, o + 2 * DECAY_RANK:o + 2 * DECAY_RANK + 2 * ICLR_RANK]
        g_lo = xs[:, o + 2 * DECAY_RANK + 2 * ICLR_RANK:]
        w_pre = w0_ref[...] + _mm1(jnp.tanh(w_lo), w2_ref[...])
        logw = -_sigmoid(w_pre) * float(np.exp(-0.5))
        a = _sigmoid(a0_ref[...] + _mm1(a_lo, a2_ref[...]))
        g = _mm1(_sigmoid(g_lo), g2_ref[...])
        kk = k * kk_ref[...]
        kk = kk * lax.rsqrt(_head_sum(kk * kk, seg_ref[...]) + 1e-12)
        for c, (t_scan, t_tail) in enumerate(((r, k), (v, g))):
            rv_ref[i, :, c * w:(c + 1) * w] = t_scan.astype(BF16)
            kg_ref[i, :, c * w:(c + 1) * w] = t_tail.astype(BF16)
        nkk_ref[i] = (-kk).astype(BF16)
        for d in range(2):
            a_d = a[:, d * w:(d + 1) * w]
            lw_ref[i, d] = logw[:, d * w:(d + 1) * w]
            kd_ref[i, d] = (k * (1.0 + (a_d - 1.0) * ka_ref[...])).astype(BF16)
            bb_ref[i, d] = (kk * a_d).astype(BF16)


def _rwkv_prep(rw, mu, w2cat, w0, a2cat, a0, g2, k_k, k_a, seg, n_lat_tiles):
    b, s, wseg = rw.shape
    n_tiles = s // ROW_TILE
    w = RWKV_WIDTH
    nb = TOKEN_BATCH
    hp, hn = _halo_specs(wseg, n_tiles, rw.dtype)
    full = lambda shape: pl.BlockSpec(shape, lambda i, j: (0,) * len(shape))
    tok = pl.BlockSpec((nb, ROW_TILE, w), lambda i, j: (i, j, 0))
    tok2 = pl.BlockSpec((nb, 2, ROW_TILE, w), lambda i, j: (i, 0, j, 0))
    pair = pl.BlockSpec((nb, ROW_TILE, 2 * w), lambda i, j: (i, j, 0))
    one = jax.ShapeDtypeStruct((b, s, w), BF16)
    two = jax.ShapeDtypeStruct((b, 2, s, w), BF16)
    two_f32 = jax.ShapeDtypeStruct((b, 2, s, w), F32)
    return pl.pallas_call(
        functools.partial(_rwkv_prep_kernel, n_lat_tiles=n_lat_tiles, n_tiles=n_tiles),
        grid=(b // nb, n_tiles),
        in_specs=[pl.BlockSpec((nb, ROW_TILE, wseg), lambda i, j: (i, j, 0)), hp, hn,
                  full((1, wseg)), full(w2cat.shape), full((1, 2 * w)), full(a2cat.shape), full((1, 2 * w)),
                  full(g2.shape), full((1, w)), full((1, w)), full(seg.shape)],
        out_specs=[pair, pair, tok, tok2, tok2, tok2],
        out_shape=[jax.ShapeDtypeStruct((b, s, 2 * w), BF16)] * 2 + [one, two_f32, two, two],
        compiler_params=_params("parallel", "parallel"),
        name="rwkv_prep",
    )(rw, rw, rw, mu, w2cat, w0, a2cat, a0, g2, k_k, k_a, seg)


def _block_diag(y, head_masks):
    return jnp.concatenate([jnp.where(m, y, 0.0) for m in head_masks], axis=0)


def _scan_chain(sgn, r, v, nkk, lw, kd, bb, s0):
    row = lax.broadcasted_iota(jnp.int32, (CHUNK, QUAD), 0)
    lane = lax.broadcasted_iota(jnp.int32, (CHUNK, QUAD), 1)
    rel = ((lane % CHUNK) - row) * sgn
    strict = rel < 0
    incl = rel <= 0
    eye = jnp.where(rel == 0, 1.0, 0.0)
    head_masks = [(lane // HEAD_DIM) == h for h in range(4)]
    bd_mask = (lax.broadcasted_iota(jnp.int32, (QUAD, QUAD), 0) // HEAD_DIM
               == lax.broadcasted_iota(jnp.int32, (QUAD, QUAD), 1) // HEAD_DIM)
    t_row = lax.broadcasted_iota(jnp.int32, (CHUNK, CHUNK), 0)
    t_col = lax.broadcasted_iota(jnp.int32, (CHUNK, CHUNK), 1)
    tri = jnp.where((t_col - t_row) * sgn <= 0, 1.0, 0.0).astype(BF16)
    last_row = CHUNK - 1 if sgn > 0 else 0
    bd = lambda t: _block_diag(t, head_masks).astype(BF16)

    cum = _mm_exact_lhs(tri, lw)
    yield
    cum_end = cum[last_row:last_row + 1]
    w_inv = jnp.exp(-cum)
    w_rem = jnp.exp(cum_end - cum)
    ar = jnp.concatenate([nkk * jnp.exp(cum - lw), r * jnp.exp(cum)], axis=0).astype(BF16)
    xb = _dg(ar, bd(bb * w_inv), NT)
    xk = _dg(ar, bd(kd * w_inv), NT)
    ars = _dg(ar, s0.astype(BF16), NT)
    yield
    n_ab = jnp.where(strict, xb[:CHUNK], 0.0)
    l_ak = jnp.where(strict, xk[:CHUNK], 0.0)
    g_rb = jnp.where(incl, xb[CHUNK:], 0.0)
    g_rk = jnp.where(incl, xk[CHUNK:], 0.0)

    m = eye + n_ab
    p = _dg(n_ab.astype(BF16), bd(n_ab))
    lg = _dg(jnp.concatenate([l_ak, g_rk], axis=0).astype(BF16), bd(v))
    yield
    for _ in range(int(np.log2(CHUNK)) - 2):
        both = _dg(jnp.concatenate([p, m], axis=0).astype(BF16), bd(p))
        yield
        p = both[:CHUNK]
        m = m + both[CHUNK:]
    m = m + _dg(m.astype(BF16), bd(p))
    yield
    x0 = ars[:CHUNK] + lg[:CHUNK]
    u = _dg(m.astype(BF16), bd(x0))
    yield
    y = ars[CHUNK:] + _dg(g_rb.astype(BF16), bd(u)) + lg[CHUNK:]
    z = _dg(jnp.concatenate([u, v], axis=0).astype(BF16),
            jnp.concatenate([bb * w_rem, kd * w_rem], axis=0).astype(BF16), TN)
    yield
    return y, s0 * jnp.exp(cum_end) + jnp.where(bd_mask, z, 0.0)


def _run_interleaved(chains, stagger=0):
    results = [None] * len(chains)
    live = list(enumerate(chains))
    rnd = 0
    while live:
        still = []
        for idx, g in live:
            if rnd >= idx * stagger:
                try:
                    next(g)
                except StopIteration as stop:
                    results[idx] = stop.value
                    continue
            still.append((idx, g))
        live = still
        rnd += 1
    return results


def _rwkv_scan_kernel(pf_ref, nf_ref, lwf_ref, kdf_ref, bbf_ref, pb_ref, nb_ref, lwb_ref, kdb_ref, bbb_ref,
                      yf_ref, yb_ref, s_ref):
    @pl.when(pl.program_id(1) == 0)
    def _():
        s_ref[...] = jnp.zeros_like(s_ref)

    w = RWKV_WIDTH
    f32 = lambda t: t.astype(F32)
    dirs = ((1, pf_ref, nf_ref, lwf_ref, kdf_ref, bbf_ref, yf_ref),
            (-1, pb_ref, nb_ref, lwb_ref, kdb_ref, bbb_ref, yb_ref))
    keys = [(i, d, q) for i in range(SCAN_BATCH) for d in range(2) for q in range(w // QUAD)]
    state = {key: s_ref[key] for key in keys}
    for step in range(SCAN_CHUNKS):
        work = []
        for i, d, q in keys:
            sgn, p_ref, n_ref, lw_ref, kd_ref, bb_ref, y_ref = dirs[d]
            c = step if sgn > 0 else SCAN_CHUNKS - 1 - step
            rows = slice(c * CHUNK, (c + 1) * CHUNK)
            ql = slice(q * QUAD, (q + 1) * QUAD)
            vl = slice(w + q * QUAD, w + (q + 1) * QUAD)
            args = (f32(p_ref[i, rows, ql]), f32(p_ref[i, rows, vl]), f32(n_ref[i, rows, ql]),
                    lw_ref[i, 0, rows, ql], f32(kd_ref[i, 0, rows, ql]), f32(bb_ref[i, 0, rows, ql]), state[(i, d, q)])
            work.append((sgn, args, y_ref, (i, rows, ql)))
        results = _run_interleaved([_scan_chain(sgn, *args) for sgn, args, _, _ in work])
        for key, (_, _, y_ref, y_idx), (y, s_new) in zip(keys, work, results):
            y_ref[y_idx] = y.astype(y_ref.dtype)
            state[key] = s_new
    for key in keys:
        s_ref[key] = state[key]


def _rwkv_scan(rv, nkk, lw, kd, bb, n_lat):
    b, s, w = nkk.shape
    rows = SCAN_CHUNKS * CHUNK
    assert s % rows == 0 and n_lat % rows == 0 and b % SCAN_BATCH == 0
    nc = s // rows
    nc_lat = n_lat // rows
    nc_ctx = nc - nc_lat
    nb = SCAN_BATCH

    fwd = lambda c: jnp.where(c < nc_ctx, nc_lat + c, c - nc_ctx)
    bwd = lambda c: jnp.where(c < nc_ctx, nc - 1 - c, nc_lat - 1 - (c - nc_ctx))
    tok = lambda order, a: pl.BlockSpec((nb, rows, a.shape[2]), lambda i, c: (i, order(c), 0))
    tok2 = lambda order, d: pl.BlockSpec((nb, 1, rows, w), lambda i, c: (i, d, order(c), 0))
    out = jax.ShapeDtypeStruct((b, s, w), BF16)
    return pl.pallas_call(
        _rwkv_scan_kernel,
        grid=(b // nb, nc),
        in_specs=[tok(fwd, rv), tok(fwd, nkk), tok2(fwd, 0), tok2(fwd, 0), tok2(fwd, 0),
                  tok(bwd, rv), tok(bwd, nkk), tok2(bwd, 1), tok2(bwd, 1), tok2(bwd, 1)],
        out_specs=[tok(fwd, nkk), tok(bwd, nkk)],
        out_shape=[out, out],
        scratch_shapes=[pltpu.VMEM((nb, 2, w // QUAD, QUAD, QUAD), F32)],
        compiler_params=_params("parallel", "arbitrary"),
        name="rwkv_scan",
    )(rv, nkk, lw, kd, bb, rv, nkk, lw, kd, bb)


def _rwkv_readout(y, r, k, v, g, r_k, gn_w, gn_b, seg):
    mean = _head_sum(y, seg) * (1.0 / HEAD_DIM)
    yc = y - mean
    var = _head_sum(yc * yc, seg) * (1.0 / HEAD_DIM)
    yn = yc * lax.rsqrt(var + RWKV_GN_EPS)
    bonus = _head_sum(r * k * r_k, seg) * v
    return (yn * gn_w + gn_b + bonus) * g


def _route(x, gain, shift, scale, w_router_t):
    h = _rms(x) * gain
    h = h * (1.0 + scale) + shift
    logits = _mm3(w_router_t, h, NT)
    e = jnp.exp(logits - jnp.max(logits, axis=0, keepdims=True))
    return h, e / jnp.sum(e, axis=0, keepdims=True)


def _merge_kernel(x_ref, oa_ref, yf_ref, yb_ref, rv_ref, kg_ref, cv_ref, hp_ref, hn_ref, gt_ref, mt_ref,
                  rk_ref, gw_ref, gb_ref, seg_ref, cw_ref, gain2_ref, wrt_ref, wa_ref, wr_ref, wc_ref, wo_ref,
                  o_ref, h_ref, aff_ref, *, n_lat_tiles, n_tiles):
    j = pl.program_id(1)
    first, last = _tile_ends(j, n_lat_tiles, n_tiles)
    cw = CONV_WIDTH
    d = x_ref.shape[2]
    w = RWKV_WIDTH
    def sample(i):
        r, v = [rv_ref[i, :, c * w:(c + 1) * w].astype(F32) for c in range(2)]
        k, g = [kg_ref[i, :, c * w:(c + 1) * w].astype(F32) for c in range(2)]
        o_rw = _rwkv_readout(yf_ref[i].astype(F32) + yb_ref[i].astype(F32), r, k, v, g, rk_ref[...], gw_ref[...],
                             gb_ref[...], seg_ref[...])
        yield
        cv = cv_ref[i].astype(F32)
        hp = hp_ref[i].astype(F32)
        hn = hn_ref[i].astype(F32)
        z = cv[:, cw:2 * cw] * cv[:, 2 * cw:]
        zp = hp[:, cw:2 * cw] * hp[:, 2 * cw:]
        zn = hn[:, cw:2 * cw] * hn[:, 2 * cw:]
        z_prev, z_next = _neighbours(z, zp, zn, first, last)
        o_cv = cv[:, :cw] * (cw_ref[0:1] * z_prev + cw_ref[1:2] * z + cw_ref[2:3] * z_next)
        yield
        t = jnp.tanh(gt_ref[i].astype(F32))
        m = ((1.0 + t[:, :d]) * _dg(oa_ref[i], wa_ref[0])
             + (1.0 + t[:, d:2 * d]) * _dg(o_rw.astype(BF16), wr_ref[0])
             + (1.0 + t[:, 2 * d:]) * _dg(o_cv.astype(BF16), wc_ref[0]))
        yield
        y = _dg(m.astype(BF16), wo_ref[0])
        x_new = x_ref[i] + mt_ref[i, 0, 2:3] * y
        o_ref[i] = x_new
        yield
        h, aff = _route(x_new, gain2_ref[...], mt_ref[i, 0, 3:4], mt_ref[i, 0, 4:5], wrt_ref[...])
        h_ref[i] = h.astype(BF16)
        aff_ref[i] = aff

    _run_interleaved([sample(i) for i in range(x_ref.shape[0])], stagger=2)


def _merge(xc, o_att, y_f, y_b, rv, kg, cv, gt, mt, r_k, gn_w, gn_b, seg, conv_w, gain2, w_router_t, layer, stacked,
           n_lat_tiles):
    b, s, d = xc.shape
    n_tiles = s // ROW_TILE
    ne = w_router_t.shape[0]
    nb = TOKEN_BATCH
    hp, hn = _halo_specs(cv.shape[2], n_tiles, cv.dtype)
    tok = lambda a: pl.BlockSpec((nb, ROW_TILE, a.shape[2]), lambda i, j: (i, j, 0))
    full = lambda a: pl.BlockSpec(a.shape, lambda i, j: (0,) * a.ndim)
    consts = (r_k, gn_w, gn_b, seg, conv_w, gain2, w_router_t)
    return pl.pallas_call(
        functools.partial(_merge_kernel, n_lat_tiles=n_lat_tiles, n_tiles=n_tiles),
        grid=(b // nb, n_tiles),
        in_specs=[tok(a) for a in (xc, o_att, y_f, y_b, rv, kg, cv)] + [hp, hn, tok(gt),
                  pl.BlockSpec((nb, 1, 6, d), lambda i, j: (i, j // n_lat_tiles, 0, 0))]
                 + [full(a) for a in consts] + [_layer_spec(a, layer) for a in stacked],
        out_specs=[tok(xc), tok(xc), pl.BlockSpec((nb, ne, ROW_TILE), lambda i, j: (i, 0, j))],
        out_shape=[jax.ShapeDtypeStruct((b, s, d), F32), jax.ShapeDtypeStruct((b, s, d), BF16),
                   jax.ShapeDtypeStruct((b, ne, s), F32)],
        compiler_params=_params("parallel", "parallel"),
        name="merge",
    )(xc, o_att, y_f, y_b, rv, kg, cv, cv, cv, gt, mt, *consts, *stacked)


def _select_kernel(aff_ref, pos_ref, gate_ref, *, cap):
    nb, n_exp, n = aff_ref.shape
    a = aff_ref[...].reshape(nb * n_exp, n)
    ne = nb * n_exp
    bits = pltpu.bitcast(a, jnp.int32)

    def count(mask):
        return jnp.sum(jnp.where(mask, 1.0, 0.0), axis=1, keepdims=True)

    def body(_, carry):
        lo, hi = carry
        mid = lo + ((hi - lo + 1) >> 1)
        ok = count(bits >= mid) >= cap
        return jnp.where(ok, mid, lo), jnp.where(ok, hi, mid - 1)

    lo0 = jnp.zeros((ne, 1), jnp.int32)
    hi0 = jnp.full((ne, 1), 0x7F800000, jnp.int32)
    thr, _ = lax.fori_loop(0, 32, body, (lo0, hi0))
    gt = bits > thr
    eq = bits == thr
    need = cap - count(gt)

    def tokens_before(mask):
        m = jnp.where(mask, 1.0, 0.0).astype(BF16)
        blk = min(n, SELECT_BLOCK)
        cols = []
        for j in range(n // blk):
            s_idx = lax.broadcasted_iota(jnp.int32, (n, blk), 0)
            t_idx = lax.broadcasted_iota(jnp.int32, (n, blk), 1) + j * blk
            cols.append(_dg(m, jnp.where(s_idx < t_idx, 1.0, 0.0).astype(BF16)))
        return jnp.concatenate(cols, axis=1)

    sel = jnp.logical_or(gt, jnp.logical_and(eq, tokens_before(eq) < need))
    pos_ref[...] = jnp.where(sel, tokens_before(sel).astype(jnp.int32), -1).reshape(nb, n_exp, n)
    gate_ref[...] = jnp.where(sel, a, 0.0).reshape(nb, n_exp, n)


def _select(aff_t, tok0, n, cap):
    b, ne, _ = aff_t.shape
    blk = tok0 // n
    nb = SELECT_BATCH if b % SELECT_BATCH == 0 else 1
    return pl.pallas_call(
        functools.partial(_select_kernel, cap=cap),
        grid=(b // nb,),
        in_specs=[pl.BlockSpec((nb, ne, n), lambda i: (i, 0, blk))],
        out_specs=[pl.BlockSpec((nb, ne, n), lambda i: (i, 0, 0)), pl.BlockSpec((nb, ne, n), lambda i: (i, 0, 0))],
        out_shape=[jax.ShapeDtypeStruct((b, ne, n), jnp.int32), jax.ShapeDtypeStruct((b, ne, n), F32)],
        compiler_params=_params("parallel"),
        name="moe_select",
    )(aff_t)


def _expert_kernel(h_ref, pos_ref, gate_ref, wg_ref, wu_ref, wd_ref, o_ref, *, cap):
    e = pl.program_id(1)

    @pl.when(e == 0)
    def _():
        o_ref[...] = jnp.zeros_like(o_ref)

    nb, n, _ = h_ref.shape
    slot = lax.broadcasted_iota(jnp.int32, (cap, n), 0)
    onehots, xes, gates = [], [], []
    for i in range(nb):
        hit = pos_ref[i, 0] == slot
        onehot = jnp.where(hit, 1.0, 0.0).astype(BF16)
        onehots.append(onehot)
        xes.append(_dg(onehot, h_ref[i]).astype(BF16))
        gates.append(jnp.sum(jnp.where(hit, gate_ref[i, 0], 0.0), axis=1, keepdims=True))
    xe = jnp.concatenate(xes, axis=0)
    hg = _dg(xe, wg_ref[0])
    hu = _dg(xe, wu_ref[0])
    hid = (hg * _sigmoid(hg) * hu).astype(BF16)
    ye = (_dg(hid, wd_ref[0]) * jnp.concatenate(gates, axis=0)).astype(BF16)
    for i in range(nb):
        o_ref[i] += _dg(onehots[i], ye[i * cap:(i + 1) * cap], TN)


def _experts(h, pos, gate, wg, wu, wd, layer, tok0, n, cap):
    b, s, d = h.shape
    ne = pos.shape[1]
    f = wg.shape[2]
    blk = tok0 // n
    nb = max(1, min(b // 2, EXPERT_ROWS // cap))
    assert b % nb == 0
    sel = pl.BlockSpec((nb, 1, 1, n), lambda i, e: (i, e, 0, 0))
    return pl.pallas_call(
        functools.partial(_expert_kernel, cap=cap),
        grid=(b // nb, ne),
        in_specs=[pl.BlockSpec((nb, n, d), lambda i, e: (i, blk, 0)), sel, sel,
                  pl.BlockSpec((1, d, f), lambda i, e: (layer * ne + e, 0, 0)),
                  pl.BlockSpec((1, d, f), lambda i, e: (layer * ne + e, 0, 0)),
                  pl.BlockSpec((1, f, d), lambda i, e: (layer * ne + e, 0, 0))],
        out_specs=pl.BlockSpec((nb, n, d), lambda i, e: (i, 0, 0)),
        out_shape=jax.ShapeDtypeStruct((b, n, d), F32),
        compiler_params=_params("parallel", "arbitrary"),
        name="moe_experts",
    )(h, pos.reshape(b, ne, 1, n), gate.reshape(b, ne, 1, n), wg, wu, wd)


def _final_kernel(x_ref, m_ref, mt_ref, g_ref, o_ref):
    o_ref[0] = _rms(x_ref[0] + mt_ref[0, 0, 5:6] * m_ref[0]) * g_ref[...]


def _final_residual_norm(xc, moe, mt, gain):
    b, s, d = xc.shape
    n_lat = moe.shape[1]
    tok = pl.BlockSpec((1, ROW_TILE, d), lambda i, j: (i, j, 0))
    return pl.pallas_call(
        _final_kernel,
        grid=(b, n_lat // ROW_TILE),
        in_specs=[tok, tok, pl.BlockSpec((1, 1, 6, d), lambda i, j: (i, 0, 0, 0)),
                  pl.BlockSpec((1, d), lambda i, j: (0, 0))],
        out_specs=tok,
        out_shape=jax.ShapeDtypeStruct((b, n_lat, d), F32),
        compiler_params=_params("parallel", "parallel"),
        name="final_norm",
    )(xc, moe, mt, gain)


def _rope_tables(n_lat, n_ctx):
    rows = n_lat // GRID_W
    row = jnp.repeat(jnp.arange(rows, dtype=F32), GRID_W)
    col = jnp.tile(jnp.arange(GRID_W, dtype=F32), rows)
    axis_dim = HEAD_DIM // 2
    inv_freq = ROPE_THETA ** (-jnp.arange(0, axis_dim, 2, dtype=F32) / axis_dim)
    ang_r = row[:, None] * inv_freq[None, :]
    ang_c = col[:, None] * inv_freq[None, :]
    cos_h = jnp.concatenate([jnp.cos(ang_r), jnp.cos(ang_r), jnp.cos(ang_c), jnp.cos(ang_c)], axis=1)
    sin_h = jnp.concatenate([-jnp.sin(ang_r), jnp.sin(ang_r), -jnp.sin(ang_c), jnp.sin(ang_c)], axis=1)
    cos_h = jnp.concatenate([cos_h, jnp.ones((n_ctx, HEAD_DIM), F32)], axis=0)
    sin_h = jnp.concatenate([sin_h, jnp.zeros((n_ctx, HEAD_DIM), F32)], axis=0)
    return jnp.tile(cos_h, (1, ATT_HEADS)), jnp.tile(sin_h, (1, ATT_HEADS))


def _head_sum_matrix(width):
    idx = np.arange(width) // HEAD_DIM
    return jnp.asarray(idx[:, None] == idx[None, :], dtype=BF16)


def _two_dir_lowrank(w2):
    _, rank, w = w2.shape
    z = jnp.zeros((rank, w), w2.dtype)
    return jnp.concatenate([jnp.concatenate([w2[0], z], axis=1), jnp.concatenate([z, w2[1]], axis=1)], axis=0)


def kernel(x, c, ctx, c_ctx, ada_w, ada_b, norm1, w_in, q_gain, k_gain, shift_mu, decay_w0, decay_w2, iclr_a0, iclr_a2, gate_g2, rwkv_kk, rwkv_ka, rwkv_rk, rwkv_gn_w, rwkv_gn_b, conv_w, w_br_att, w_br_rwkv, w_br_conv, w_out, norm2, w_router, exp_gate, exp_up, exp_down, final_norm):
    b, n_lat, d = x.shape
    n_ctx = ctx.shape[1]
    depth = ada_w.shape[0]
    assert n_lat % ROW_TILE == 0 and n_ctx % ROW_TILE == 0 and n_lat % n_ctx == 0
    n_lat_tiles = n_lat // ROW_TILE

    pad = (-(b + 1)) % 8
    cc = jnp.concatenate([c, c_ctx[None, :], jnp.zeros((pad, d), F32)], axis=0)
    mods = _ada_table(cc, ada_w, ada_b)

    cos_t, sin_t = _rope_tables(n_lat, n_ctx)
    seg = _head_sum_matrix(ATT_WIDTH)
    att_w = ATT_WIDTH + 2 * ATT_KV_WIDTH
    cv_w = 3 * CONV_WIDTH
    offs = np.cumsum([0, att_w, RWKV_SEG, cv_w, 3 * d])

    group_scale = (1.0, 1.0, 1.0, 0.5)
    w_groups = [(w_in[:, :, offs[i]:offs[i + 1]] * group_scale[i]).astype(BF16) for i in range(4)]
    w_tail = [t.astype(BF16) for t in (w_br_att, w_br_rwkv, w_br_conv, 0.5 * w_out)]
    ne, f = exp_gate.shape[1], exp_gate.shape[3]
    wg = exp_gate.astype(BF16).reshape(depth * ne, d, f)
    wu = exp_up.astype(BF16).reshape(depth * ne, d, f)
    wd = exp_down.astype(BF16).reshape(depth * ne, f, d)

    xc = jnp.concatenate([x, ctx], axis=1)
    pending = None
    for l in range(depth):
        mod_lat = mods[l, :b].reshape(b, 1, 6, d)
        mod_ctx = jnp.broadcast_to(mods[l, b].reshape(1, 1, 6, d), (b, 1, 6, d))
        mt = jnp.concatenate([mod_lat, mod_ctx], axis=1)

        gain1 = norm1[l].reshape(1, d)
        xc, qvk, rw, cv, gt = _in_proj(
            xc, pending, gain1, mt, cos_t, sin_t, jnp.tile(q_gain[l], ATT_HEADS).reshape(1, -1),
            jnp.tile(k_gain[l], ATT_KV_HEADS).reshape(1, -1), seg, l, w_groups, n_lat_tiles)
        o_att = _attention(qvk, n_lat)

        rv, kg, nkk, lw, kd, bb = _rwkv_prep(
            rw, shift_mu[l].reshape(1, -1), _two_dir_lowrank(decay_w2[l]), decay_w0[l].reshape(1, -1),
            _two_dir_lowrank(iclr_a2[l]), iclr_a0[l].reshape(1, -1), gate_g2[l],
            rwkv_kk[l].reshape(1, -1), rwkv_ka[l].reshape(1, -1), seg, n_lat_tiles)
        y_f, y_b = _rwkv_scan(rv, nkk, lw, kd, bb, n_lat)

        xc, h2, aff_t = _merge(
            xc, o_att, y_f, y_b, rv, kg, cv, gt, mt, rwkv_rk[l].reshape(1, -1), rwkv_gn_w[l].reshape(1, -1),
            rwkv_gn_b[l].reshape(1, -1), seg, conv_w[l], norm2[l].reshape(1, d), w_router[l].T, l, w_tail,
            n_lat_tiles)
        streams = [(0, n_lat)] + ([(n_lat, n_ctx)] if l < depth - 1 else [])
        moes = []
        for tok0, n in streams:
            cap = CAPACITY_FACTOR * n // N_EXPERTS
            pos, gate = _select(aff_t, tok0, n, cap)
            moes.append(_experts(h2, pos, gate, wg, wu, wd, l, tok0, n, cap))
        if l == depth - 1:
            return _final_residual_norm(xc, moes[0], mt, final_norm.reshape(1, d))
        pending = (moes[0], moes[1], mt)
```

```python
import functools

import numpy as np
import jax
import jax.numpy as jnp
from jax import lax
from jax.experimental import pallas as pl
from jax.experimental.pallas import tpu as pltpu

F32 = jnp.float32
BF16 = jnp.bfloat16

GRID_W = 64
NORM_EPS = 1e-6
LOG2_E = float(np.log2(np.e))
ATT_HEADS = 8
ATT_KV_HEADS = 2
HEAD_DIM = 64
ATT_GROUP = ATT_HEADS // ATT_KV_HEADS
ATT_WIDTH = ATT_HEADS * HEAD_DIM
ATT_KV_WIDTH = ATT_KV_HEADS * HEAD_DIM
ROPE_THETA = 10000.0
RWKV_HEADS = 8
RWKV_WIDTH = RWKV_HEADS * HEAD_DIM
DECAY_RANK = 64
ICLR_RANK = 64
GATE_RANK = 128
RWKV_GN_EPS = 64e-5
RWKV_SEG = 3 * RWKV_WIDTH + 2 * DECAY_RANK + 2 * ICLR_RANK + GATE_RANK
CONV_WIDTH = 512
N_EXPERTS = 16
CAPACITY_FACTOR = 2

ROW_TILE = 256
ATT_TILE = 512
CHUNK = 64
QUAD = 4 * HEAD_DIM
SCAN_BATCH = 4
SCAN_CHUNKS = 4
TOKEN_BATCH = 2
HALO = 8
SELECT_BLOCK = 256
SELECT_BATCH = 4
EXPERT_ROWS = 256
VMEM_LIMIT = 56 * 1024 * 1024

NT = (((1,), (1,)), ((), ()))
TN = (((0,), (0,)), ((), ()))
NN = (((1,), (0,)), ((), ()))


def _params(*sem):
    return pltpu.CompilerParams(dimension_semantics=sem, vmem_limit_bytes=VMEM_LIMIT)


def _layer_spec(a, layer):
    return pl.BlockSpec((1,) + a.shape[1:], lambda *_: (layer,) + (0,) * (a.ndim - 1))


def _dg(a, b, dn=NN):
    return lax.dot_general(a, b, dn, preferred_element_type=F32)


def _split2(x):
    hi = x.astype(BF16)
    lo = (x - hi.astype(F32)).astype(BF16)
    return hi, lo


def _mm1(a, b, dn=NN):
    return _dg(a.astype(BF16), b.astype(BF16), dn)


def _mm3(a, b, dn=NN):
    ah, al = _split2(a)
    bh, bl = _split2(b)
    return _dg(ah, bh, dn) + (_dg(ah, bl, dn) + _dg(al, bh, dn))


def _mm_exact_lhs(a_bf16, b, dn=NN):
    hi, lo = _split2(b)
    return _dg(a_bf16, hi, dn) + _dg(a_bf16, lo, dn)


def _head_sum(x, seg):
    return _dg(x.astype(BF16), seg)


def _sigmoid(x):
    return 0.5 * jnp.tanh(0.5 * x) + 0.5


def _rms(x):
    return x * lax.rsqrt(jnp.mean(x * x, axis=-1, keepdims=True) + NORM_EPS)


def _ada_kernel(c_ref, w_ref, b_ref, o_ref):
    c = c_ref[...]
    s = c * _sigmoid(c)
    o_ref[0] = _mm3(s, w_ref[0]) + b_ref[0]


def _ada_table(cc, ada_w, ada_b):
    n_layers, d, six_d = ada_w.shape
    rows = cc.shape[0]
    tn = six_d // 4
    return pl.pallas_call(
        _ada_kernel,
        grid=(n_layers, six_d // tn),
        in_specs=[pl.BlockSpec((rows, d), lambda l, j: (0, 0)),
                  pl.BlockSpec((1, d, tn), lambda l, j: (l, 0, j)),
                  pl.BlockSpec((1, 1, tn), lambda l, j: (l, 0, j))],
        out_specs=pl.BlockSpec((1, rows, tn), lambda l, j: (l, 0, j)),
        out_shape=jax.ShapeDtypeStruct((n_layers, rows, six_d), F32),
        compiler_params=_params("parallel", "parallel"),
        name="ada_table",
    )(cc, ada_w, ada_b.reshape(n_layers, 1, six_d))


def _swap16(x):
    w = x.shape[-1]
    lane = lax.broadcasted_iota(jnp.int32, x.shape, x.ndim - 1)
    return jnp.where((lane & 16) == 0, pltpu.roll(x, w - 16, x.ndim - 1), pltpu.roll(x, 16, x.ndim - 1))


def _att_heads(a, cos, sin, q_gain, k_gain, seg):
    def norm_rope(u, gain, cos_u, sin_u, seg_u):
        ms = _head_sum(u * u, seg_u) * (1.0 / HEAD_DIM)
        un = u * lax.rsqrt(ms + NORM_EPS) * gain
        return un * cos_u + _swap16(un) * sin_u

    kw = ATT_KV_WIDTH
    q = norm_rope(a[:, :ATT_WIDTH], q_gain, cos, sin, seg)
    k = norm_rope(a[:, ATT_WIDTH:ATT_WIDTH + kw], k_gain, cos[:, :kw], sin[:, :kw], seg[:kw, :kw])
    q = (q * (HEAD_DIM ** -0.5 * LOG2_E)).astype(BF16)
    ones = jnp.ones((a.shape[0], HEAD_DIM), BF16)
    v = a[:, ATT_WIDTH + kw:].astype(BF16)
    v1 = jnp.concatenate([t for h in range(ATT_KV_HEADS) for t in (v[:, h * HEAD_DIM:(h + 1) * HEAD_DIM], ones)],
                         axis=1)
    return q, k.astype(BF16), v1


def _in_proj_kernel(*refs, n_lat_tiles, has_pending):
    if has_pending:
        x_ref, ml_ref, mc_ref, mtp_ref = refs[:4]
        refs = refs[4:]
        xo_ref, refs = refs[-5], refs[:-5] + refs[-4:]
    else:
        x_ref, refs = refs[0], refs[1:]
    gain_ref, mt_ref, cos_ref, sin_ref, qg_ref, kg_ref, seg_ref, wa_ref, wr_ref, wc_ref, wg_ref = refs[:11]
    qvk_ref, rw_ref, cv_ref, gt_ref = refs[11:]
    is_latent = pl.program_id(1) < n_lat_tiles
    for i in range(x_ref.shape[0]):
        x = x_ref[i]
        if has_pending:
            x = x + mtp_ref[i, 0, 5:6] * jnp.where(is_latent, ml_ref[i], mc_ref[i])
            xo_ref[i] = x
        h = _rms(x) * gain_ref[...]
        h = (h * (1.0 + mt_ref[i, 0, 1:2]) + mt_ref[i, 0, 0:1]).astype(BF16)
        q, k, v1 = _att_heads(_dg(h, wa_ref[0]), cos_ref[...], sin_ref[...], qg_ref[...], kg_ref[...], seg_ref[...])
        qvk_ref[i] = jnp.concatenate([q, v1, k], axis=1)
        rw_ref[i] = _dg(h, wr_ref[0]).astype(BF16)
        cv_ref[i] = _dg(h, wc_ref[0]).astype(BF16)
        gt_ref[i] = _dg(h, wg_ref[0]).astype(BF16)


def _in_proj(xc, pending, gain, mt, cos_t, sin_t, q_gain_t, k_gain_t, seg, layer, weights, n_lat_tiles):
    b, s, d = xc.shape
    nb = TOKEN_BATCH if pending is None else 1
    tok = lambda n: pl.BlockSpec((nb, ROW_TILE, n), lambda i, j: (i, j, 0))
    full = lambda a: pl.BlockSpec(a.shape, lambda i, j: (0,) * a.ndim)
    table = pl.BlockSpec((ROW_TILE, ATT_WIDTH), lambda i, j: (j, 0))
    mods = lambda: pl.BlockSpec((nb, 1, 6, d), lambda i, j: (i, j // n_lat_tiles, 0, 0))
    widths = [ATT_WIDTH + 3 * ATT_KV_WIDTH] + [w.shape[2] for w in weights[1:]]
    in_specs, operands = [tok(d)], [xc]
    out_specs = [tok(n) for n in widths]
    out_shape = [jax.ShapeDtypeStruct((b, s, n), BF16) for n in widths]
    if pending is not None:
        moe_lat, moe_ctx, mt_prev = pending
        in_specs += [pl.BlockSpec((nb, ROW_TILE, d), lambda i, j: (i, jnp.minimum(j, n_lat_tiles - 1), 0)),
                     pl.BlockSpec((nb, ROW_TILE, d), lambda i, j: (i, jnp.maximum(j - n_lat_tiles, 0), 0)), mods()]
        operands += [moe_lat, moe_ctx, mt_prev]
        out_specs = [tok(d)] + out_specs
        out_shape = [jax.ShapeDtypeStruct((b, s, d), F32)] + out_shape
    in_specs += [full(gain), mods(), table, table, full(q_gain_t), full(k_gain_t), full(seg)]
    in_specs += [_layer_spec(w, layer) for w in weights]
    operands += [gain, mt, cos_t, sin_t, q_gain_t, k_gain_t, seg, *weights]
    outs = pl.pallas_call(
        functools.partial(_in_proj_kernel, n_lat_tiles=n_lat_tiles, has_pending=pending is not None),
        grid=(b // nb, s // ROW_TILE),
        in_specs=in_specs,
        out_specs=out_specs,
        out_shape=out_shape,
        compiler_params=_params("parallel", "parallel"),
        name="in_proj",
    )(*operands)
    return outs if pending is not None else [xc] + list(outs)


def _attn_kernel(q_ref, k_ref, v_ref, *rest):
    o_ref = rest[-1]

    def scores(h):
        kvh = h // ATT_GROUP
        return _dg(q_ref[0, :, h * HEAD_DIM:(h + 1) * HEAD_DIM], k_ref[0, :, kvh * HEAD_DIM:(kvh + 1) * HEAD_DIM], NT)

    s_next = scores(0)
    for h in range(ATT_HEADS):
        s = s_next
        if h + 1 < ATT_HEADS:
            s_next = scores(h + 1)
        kvh = h // ATT_GROUP
        v1 = v_ref[0, :, 2 * kvh * HEAD_DIM:2 * (kvh + 1) * HEAD_DIM]
        p = jnp.exp2((s - jnp.max(s, axis=-1, keepdims=True)).astype(BF16))
        ov = _dg(p, v1)
        o = ov[:, :HEAD_DIM] / ov[:, HEAD_DIM:]
        o_ref[0, :, h * HEAD_DIM:(h + 1) * HEAD_DIM] = o.astype(BF16)


def _attention(qvk, n_lat):
    b, s, _ = qvk.shape
    n_ctx = s - n_lat
    v_w = 2 * ATT_KV_WIDTH
    k_col, v_col = (ATT_WIDTH + v_w) // ATT_KV_WIDTH, ATT_WIDTH // v_w
    tile = ATT_TILE if n_lat % ATT_TILE == 0 else ROW_TILE
    out = jax.ShapeDtypeStruct((b, s, ATT_WIDTH), BF16)
    o_lat = pl.pallas_call(
        _attn_kernel,
        grid=(b, n_lat // tile),
        in_specs=[pl.BlockSpec((1, tile, ATT_WIDTH), lambda i, j: (i, j, 0)),
                  pl.BlockSpec((1, s, ATT_KV_WIDTH), lambda i, j: (i, 0, k_col)),
                  pl.BlockSpec((1, s, v_w), lambda i, j: (i, 0, v_col))],
        out_specs=pl.BlockSpec((1, tile, ATT_WIDTH), lambda i, j: (i, j, 0)),
        out_shape=out,
        compiler_params=_params("parallel", "parallel"),
        name="attention",
    )(qvk, qvk, qvk)
    ctx_blk = n_lat // n_ctx
    return pl.pallas_call(
        _attn_kernel,
        grid=(b,),
        in_specs=[pl.BlockSpec((1, n_ctx, ATT_WIDTH), lambda i: (i, ctx_blk, 0)),
                  pl.BlockSpec((1, n_ctx, ATT_KV_WIDTH), lambda i: (i, ctx_blk, k_col)),
                  pl.BlockSpec((1, n_ctx, v_w), lambda i: (i, ctx_blk, v_col)),
                  pl.BlockSpec(memory_space=pl.ANY)],
        out_specs=pl.BlockSpec((1, n_ctx, ATT_WIDTH), lambda i: (i, ctx_blk, 0)),
        out_shape=out,
        input_output_aliases={3: 0},
        compiler_params=_params("parallel"),
        name="attention_ctx",
    )(qvk, qvk, qvk, o_lat)


def _neighbours(x, halo_prev, halo_next, first, last):
    rows = x.shape[0]
    ridx = lax.broadcasted_iota(jnp.int32, x.shape, 0)
    row_p = jnp.where(first, 0.0, halo_prev[halo_prev.shape[0] - 1:])
    row_n = jnp.where(last, 0.0, halo_next[0:1])
    prev = jnp.where(ridx == 0, row_p, pltpu.roll(x, 1, 0))
    nxt = jnp.where(ridx == rows - 1, row_n, pltpu.roll(x, rows - 1, 0))
    return prev, nxt


def _tile_ends(j, n_lat_tiles, n_tiles):
    first = jnp.logical_or(j == 0, j == n_lat_tiles)
    last = jnp.logical_or(j == n_lat_tiles - 1, j == n_tiles - 1)
    return first, last


def _halo_specs(width, n_tiles, dtype):
    halo = HALO * (4 // jnp.dtype(dtype).itemsize)
    per = ROW_TILE // halo
    nb = TOKEN_BATCH
    prev = pl.BlockSpec((nb, halo, width), lambda i, j: (i, jnp.maximum(j * per - 1, 0), 0))
    nxt = pl.BlockSpec((nb, halo, width), lambda i, j: (i, jnp.minimum((j + 1) * per, n_tiles * per - 1), 0))
    return prev, nxt


def _rwkv_prep_kernel(x_ref, hp_ref, hn_ref, mu_ref, w2_ref, w0_ref, a2_ref, a0_ref, g2_ref, kk_ref, ka_ref,
                      seg_ref, rv_ref, kg_ref, nkk_ref, lw_ref, kd_ref, bb_ref, *, n_lat_tiles, n_tiles):
    j = pl.program_id(1)
    first, last = _tile_ends(j, n_lat_tiles, n_tiles)
    w = RWKV_WIDTH
    halo = hp_ref.shape[1]
    rows = x_ref.shape[1]
    t_idx = lax.broadcasted_iota(jnp.int32, (rows, rows + 2 * halo), 0) + halo
    u_idx = lax.broadcasted_iota(jnp.int32, (rows, rows + 2 * halo), 1)
    is_prev = jnp.logical_and(u_idx == t_idx - 1, jnp.logical_or(u_idx >= halo, jnp.logical_not(first)))
    is_next = jnp.logical_and(u_idx == t_idx + 1, jnp.logical_or(u_idx < halo + rows, jnp.logical_not(last)))
    band = jnp.where(jnp.logical_or(is_prev, is_next), 0.5, 0.0).astype(BF16)
    for i in range(x_ref.shape[0]):
        x = x_ref[i].astype(F32)
        mean_nb = _dg(band, jnp.concatenate([hp_ref[i], x_ref[i], hn_ref[i]], axis=0))
        xs = x + mu_ref[...] * (mean_nb - x)
        r = xs[:, 0:w]
        k = xs[:, w:2 * w]
        v = xs[:, 2 * w:3 * w]
        o = 3 * w
        w_lo = xs[:, o:o + 2 * DECAY_RANK]
        a_lo = xs[:, o + 2 * DECAY_RANK:o + 2 * DECAY_RANK + 2 * ICLR_RANK]
        g_lo = xs[:, o + 2 * DECAY_RANK + 2 * ICLR_RANK:]
        w_pre = w0_ref[...] + _mm1(jnp.tanh(w_lo), w2_ref[...])
        logw = -_sigmoid(w_pre) * float(np.exp(-0.5))
        a = _sigmoid(a0_ref[...] + _mm1(a_lo, a2_ref[...]))
        g = _mm1(_sigmoid(g_lo), g2_ref[...])
        kk = k * kk_ref[...]
        kk = kk * lax.rsqrt(_head_sum(kk * kk, seg_ref[...]) + 1e-12)
        for c, (t_scan, t_tail) in enumerate(((r, k), (v, g))):
            rv_ref[i, :, c * w:(c + 1) * w] = t_scan.astype(BF16)
            kg_ref[i, :, c * w:(c + 1) * w] = t_tail.astype(BF16)
        nkk_ref[i] = (-kk).astype(BF16)
        for d in range(2):
            a_d = a[:, d * w:(d + 1) * w]
            lw_ref[i, d] = logw[:, d * w:(d + 1) * w]
            kd_ref[i, d] = (k * (1.0 + (a_d - 1.0) * ka_ref[...])).astype(BF16)
            bb_ref[i, d] = (kk * a_d).astype(BF16)


def _rwkv_prep(rw, mu, w2cat, w0, a2cat, a0, g2, k_k, k_a, seg, n_lat_tiles):
    b, s, wseg = rw.shape
    n_tiles = s // ROW_TILE
    w = RWKV_WIDTH
    nb = TOKEN_BATCH
    hp, hn = _halo_specs(wseg, n_tiles, rw.dtype)
    full = lambda shape: pl.BlockSpec(shape, lambda i, j: (0,) * len(shape))
    tok = pl.BlockSpec((nb, ROW_TILE, w), lambda i, j: (i, j, 0))
    tok2 = pl.BlockSpec((nb, 2, ROW_TILE, w), lambda i, j: (i, 0, j, 0))
    pair = pl.BlockSpec((nb, ROW_TILE, 2 * w), lambda i, j: (i, j, 0))
    one = jax.ShapeDtypeStruct((b, s, w), BF16)
    two = jax.ShapeDtypeStruct((b, 2, s, w), BF16)
    two_f32 = jax.ShapeDtypeStruct((b, 2, s, w), F32)
    return pl.pallas_call(
        functools.partial(_rwkv_prep_kernel, n_lat_tiles=n_lat_tiles, n_tiles=n_tiles),
        grid=(b // nb, n_tiles),
        in_specs=[pl.BlockSpec((nb, ROW_TILE, wseg), lambda i, j: (i, j, 0)), hp, hn,
                  full((1, wseg)), full(w2cat.shape), full((1, 2 * w)), full(a2cat.shape), full((1, 2 * w)),
                  full(g2.shape), full((1, w)), full((1, w)), full(seg.shape)],
        out_specs=[pair, pair, tok, tok2, tok2, tok2],
        out_shape=[jax.ShapeDtypeStruct((b, s, 2 * w), BF16)] * 2 + [one, two_f32, two, two],
        compiler_params=_params("parallel", "parallel"),
        name="rwkv_prep",
    )(rw, rw, rw, mu, w2cat, w0, a2cat, a0, g2, k_k, k_a, seg)


def _block_diag(y, head_masks):
    return jnp.concatenate([jnp.where(m, y, 0.0) for m in head_masks], axis=0)


def _scan_chain(sgn, r, v, nkk, lw, kd, bb, s0):
    row = lax.broadcasted_iota(jnp.int32, (CHUNK, QUAD), 0)
    lane = lax.broadcasted_iota(jnp.int32, (CHUNK, QUAD), 1)
    rel = ((lane % CHUNK) - row) * sgn
    strict = rel < 0
    incl = rel <= 0
    eye = jnp.where(rel == 0, 1.0, 0.0)
    head_masks = [(lane // HEAD_DIM) == h for h in range(4)]
    bd_mask = (lax.broadcasted_iota(jnp.int32, (QUAD, QUAD), 0) // HEAD_DIM
               == lax.broadcasted_iota(jnp.int32, (QUAD, QUAD), 1) // HEAD_DIM)
    t_row = lax.broadcasted_iota(jnp.int32, (CHUNK, CHUNK), 0)
    t_col = lax.broadcasted_iota(jnp.int32, (CHUNK, CHUNK), 1)
    tri = jnp.where((t_col - t_row) * sgn <= 0, 1.0, 0.0).astype(BF16)
    last_row = CHUNK - 1 if sgn > 0 else 0
    bd = lambda t: _block_diag(t, head_masks).astype(BF16)

    cum = _mm_exact_lhs(tri, lw)
    yield
    cum_end = cum[last_row:last_row + 1]
    w_inv = jnp.exp(-cum)
    w_rem = jnp.exp(cum_end - cum)
    ar = jnp.concatenate([nkk * jnp.exp(cum - lw), r * jnp.exp(cum)], axis=0).astype(BF16)
    xb = _dg(ar, bd(bb * w_inv), NT)
    xk = _dg(ar, bd(kd * w_inv), NT)
    ars = _dg(ar, s0.astype(BF16), NT)
    yield
    n_ab = jnp.where(strict, xb[:CHUNK], 0.0)
    l_ak = jnp.where(strict, xk[:CHUNK], 0.0)
    g_rb = jnp.where(incl, xb[CHUNK:], 0.0)
    g_rk = jnp.where(incl, xk[CHUNK:], 0.0)

    m = eye + n_ab
    p = _dg(n_ab.astype(BF16), bd(n_ab))
    lg = _dg(jnp.concatenate([l_ak, g_rk], axis=0).astype(BF16), bd(v))
    yield
    for _ in range(int(np.log2(CHUNK)) - 2):
        both = _dg(jnp.concatenate([p, m], axis=0).astype(BF16), bd(p))
        yield
        p = both[:CHUNK]
        m = m + both[CHUNK:]
    m = m + _dg(m.astype(BF16), bd(p))
    yield
    x0 = ars[:CHUNK] + lg[:CHUNK]
    u = _dg(m.astype(BF16), bd(x0))
    yield
    y = ars[CHUNK:] + _dg(g_rb.astype(BF16), bd(u)) + lg[CHUNK:]
    z = _dg(jnp.concatenate([u, v], axis=0).astype(BF16),
            jnp.concatenate([bb * w_rem, kd * w_rem], axis=0).astype(BF16), TN)
    yield
    return y, s0 * jnp.exp(cum_end) + jnp.where(bd_mask, z, 0.0)


def _run_interleaved(chains, stagger=0):
    results = [None] * len(chains)
    live = list(enumerate(chains))
    rnd = 0
    while live:
        still = []
        for idx, g in live:
            if rnd >= idx * stagger:
                try:
                    next(g)
                except StopIteration as stop:
                    results[idx] = stop.value
                    continue
            still.append((idx, g))
        live = still
        rnd += 1
    return results


def _rwkv_scan_kernel(pf_ref, nf_ref, lwf_ref, kdf_ref, bbf_ref, pb_ref, nb_ref, lwb_ref, kdb_ref, bbb_ref,
                      yf_ref, yb_ref, s_ref):
    @pl.when(pl.program_id(1) == 0)
    def _():
        s_ref[...] = jnp.zeros_like(s_ref)

    w = RWKV_WIDTH
    f32 = lambda t: t.astype(F32)
    dirs = ((1, pf_ref, nf_ref, lwf_ref, kdf_ref, bbf_ref, yf_ref),
            (-1, pb_ref, nb_ref, lwb_ref, kdb_ref, bbb_ref, yb_ref))
    keys = [(i, d, q) for i in range(SCAN_BATCH) for d in range(2) for q in range(w // QUAD)]
    state = {key: s_ref[key] for key in keys}
    for step in range(SCAN_CHUNKS):
        work = []
        for i, d, q in keys:
            sgn, p_ref, n_ref, lw_ref, kd_ref, bb_ref, y_ref = dirs[d]
            c = step if sgn > 0 else SCAN_CHUNKS - 1 - step
            rows = slice(c * CHUNK, (c + 1) * CHUNK)
            ql = slice(q * QUAD, (q + 1) * QUAD)
            vl = slice(w + q * QUAD, w + (q + 1) * QUAD)
            args = (f32(p_ref[i, rows, ql]), f32(p_ref[i, rows, vl]), f32(n_ref[i, rows, ql]),
                    lw_ref[i, 0, rows, ql], f32(kd_ref[i, 0, rows, ql]), f32(bb_ref[i, 0, rows, ql]), state[(i, d, q)])
            work.append((sgn, args, y_ref, (i, rows, ql)))
        results = _run_interleaved([_scan_chain(sgn, *args) for sgn, args, _, _ in work])
        for key, (_, _, y_ref, y_idx), (y, s_new) in zip(keys, work, results):
            y_ref[y_idx] = y.astype(y_ref.dtype)
            state[key] = s_new
    for key in keys:
        s_ref[key] = state[key]


def _rwkv_scan(rv, nkk, lw, kd, bb, n_lat):
    b, s, w = nkk.shape
    rows = SCAN_CHUNKS * CHUNK
    assert s % rows == 0 and n_lat % rows == 0 and b % SCAN_BATCH == 0
    nc = s // rows
    nc_lat = n_lat // rows
    nc_ctx = nc - nc_lat
    nb = SCAN_BATCH

    fwd = lambda c: jnp.where(c < nc_ctx, nc_lat + c, c - nc_ctx)
    bwd = lambda c: jnp.where(c < nc_ctx, nc - 1 - c, nc_lat - 1 - (c - nc_ctx))
    tok = lambda order, a: pl.BlockSpec((nb, rows, a.shape[2]), lambda i, c: (i, order(c), 0))
    tok2 = lambda order, d: pl.BlockSpec((nb, 1, rows, w), lambda i, c: (i, d, order(c), 0))
    out = jax.ShapeDtypeStruct((b, s, w), BF16)
    return pl.pallas_call(
        _rwkv_scan_kernel,
        grid=(b // nb, nc),
        in_specs=[tok(fwd, rv), tok(fwd, nkk), tok2(fwd, 0), tok2(fwd, 0), tok2(fwd, 0),
                  tok(bwd, rv), tok(bwd, nkk), tok2(bwd, 1), tok2(bwd, 1), tok2(bwd, 1)],
        out_specs=[tok(fwd, nkk), tok(bwd, nkk)],
        out_shape=[out, out],
        scratch_shapes=[pltpu.VMEM((nb, 2, w // QUAD, QUAD, QUAD), F32)],
        compiler_params=_params("parallel", "arbitrary"),
        name="rwkv_scan",
    )(rv, nkk, lw, kd, bb, rv, nkk, lw, kd, bb)


def _rwkv_readout(y, r, k, v, g, r_k, gn_w, gn_b, seg):
    mean = _head_sum(y, seg) * (1.0 / HEAD_DIM)
    yc = y - mean
    var = _head_sum(yc * yc, seg) * (1.0 / HEAD_DIM)
    yn = yc * lax.rsqrt(var + RWKV_GN_EPS)
    bonus = _head_sum(r * k * r_k, seg) * v
    return (yn * gn_w + gn_b + bonus) * g


def _route(x, gain, shift, scale, w_router_t):
    h = _rms(x) * gain
    h = h * (1.0 + scale) + shift
    logits = _mm3(w_router_t, h, NT)
    e = jnp.exp(logits - jnp.max(logits, axis=0, keepdims=True))
    return h, e / jnp.sum(e, axis=0, keepdims=True)


def _merge_kernel(x_ref, oa_ref, yf_ref, yb_ref, rv_ref, kg_ref, cv_ref, hp_ref, hn_ref, gt_ref, mt_ref,
                  rk_ref, gw_ref, gb_ref, seg_ref, cw_ref, gain2_ref, wrt_ref, wa_ref, wr_ref, wc_ref, wo_ref,
                  o_ref, h_ref, aff_ref, *, n_lat_tiles, n_tiles):
    j = pl.program_id(1)
    first, last = _tile_ends(j, n_lat_tiles, n_tiles)
    cw = CONV_WIDTH
    d = x_ref.shape[2]
    w = RWKV_WIDTH
    def sample(i):
        r, v = [rv_ref[i, :, c * w:(c + 1) * w].astype(F32) for c in range(2)]
        k, g = [kg_ref[i, :, c * w:(c + 1) * w].astype(F32) for c in range(2)]
        o_rw = _rwkv_readout(yf_ref[i].astype(F32) + yb_ref[i].astype(F32), r, k, v, g, rk_ref[...], gw_ref[...],
                             gb_ref[...], seg_ref[...])
        yield
        cv = cv_ref[i].astype(F32)
        hp = hp_ref[i].astype(F32)
        hn = hn_ref[i].astype(F32)
        z = cv[:, cw:2 * cw] * cv[:, 2 * cw:]
        zp = hp[:, cw:2 * cw] * hp[:, 2 * cw:]
        zn = hn[:, cw:2 * cw] * hn[:, 2 * cw:]
        z_prev, z_next = _neighbours(z, zp, zn, first, last)
        o_cv = cv[:, :cw] * (cw_ref[0:1] * z_prev + cw_ref[1:2] * z + cw_ref[2:3] * z_next)
        yield
        t = jnp.tanh(gt_ref[i].astype(F32))
        m = ((1.0 + t[:, :d]) * _dg(oa_ref[i], wa_ref[0])
             + (1.0 + t[:, d:2 * d]) * _dg(o_rw.astype(BF16), wr_ref[0])
             + (1.0 + t[:, 2 * d:]) * _dg(o_cv.astype(BF16), wc_ref[0]))
        yield
        y = _dg(m.astype(BF16), wo_ref[0])
        x_new = x_ref[i] + mt_ref[i, 0, 2:3] * y
        o_ref[i] = x_new
        yield
        h, aff = _route(x_new, gain2_ref[...], mt_ref[i, 0, 3:4], mt_ref[i, 0, 4:5], wrt_ref[...])
        h_ref[i] = h.astype(BF16)
        aff_ref[i] = aff

    _run_interleaved([sample(i) for i in range(x_ref.shape[0])], stagger=2)


def _merge(xc, o_att, y_f, y_b, rv, kg, cv, gt, mt, r_k, gn_w, gn_b, seg, conv_w, gain2, w_router_t, layer, stacked,
           n_lat_tiles):
    b, s, d = xc.shape
    n_tiles = s // ROW_TILE
    ne = w_router_t.shape[0]
    nb = TOKEN_BATCH
    hp, hn = _halo_specs(cv.shape[2], n_tiles, cv.dtype)
    tok = lambda a: pl.BlockSpec((nb, ROW_TILE, a.shape[2]), lambda i, j: (i, j, 0))
    full = lambda a: pl.BlockSpec(a.shape, lambda i, j: (0,) * a.ndim)
    consts = (r_k, gn_w, gn_b, seg, conv_w, gain2, w_router_t)
    return pl.pallas_call(
        functools.partial(_merge_kernel, n_lat_tiles=n_lat_tiles, n_tiles=n_tiles),
        grid=(b // nb, n_tiles),
        in_specs=[tok(a) for a in (xc, o_att, y_f, y_b, rv, kg, cv)] + [hp, hn, tok(gt),
                  pl.BlockSpec((nb, 1, 6, d), lambda i, j: (i, j // n_lat_tiles, 0, 0))]
                 + [full(a) for a in consts] + [_layer_spec(a, layer) for a in stacked],
        out_specs=[tok(xc), tok(xc), pl.BlockSpec((nb, ne, ROW_TILE), lambda i, j: (i, 0, j))],
        out_shape=[jax.ShapeDtypeStruct((b, s, d), F32), jax.ShapeDtypeStruct((b, s, d), BF16),
                   jax.ShapeDtypeStruct((b, ne, s), F32)],
        compiler_params=_params("parallel", "parallel"),
        name="merge",
    )(xc, o_att, y_f, y_b, rv, kg, cv, cv, cv, gt, mt, *consts, *stacked)


def _select_kernel(aff_ref, pos_ref, gate_ref, *, cap):
    nb, n_exp, n = aff_ref.shape
    a = aff_ref[...].reshape(nb * n_exp, n)
    ne = nb * n_exp
    bits = pltpu.bitcast(a, jnp.int32)

    def count(mask):
        return jnp.sum(jnp.where(mask, 1.0, 0.0), axis=1, keepdims=True)

    def body(_, carry):
        lo, hi = carry
        mid = lo + ((hi - lo + 1) >> 1)
        ok = count(bits >= mid) >= cap
        return jnp.where(ok, mid, lo), jnp.where(ok, hi, mid - 1)

    lo0 = jnp.zeros((ne, 1), jnp.int32)
    hi0 = jnp.full((ne, 1), 0x7F800000, jnp.int32)
    thr, _ = lax.fori_loop(0, 32, body, (lo0, hi0))
    gt = bits > thr
    eq = bits == thr
    need = cap - count(gt)

    def tokens_before(mask):
        m = jnp.where(mask, 1.0, 0.0).astype(BF16)
        blk = min(n, SELECT_BLOCK)
        cols = []
        for j in range(n // blk):
            s_idx = lax.broadcasted_iota(jnp.int32, (n, blk), 0)
            t_idx = lax.broadcasted_iota(jnp.int32, (n, blk), 1) + j * blk
            cols.append(_dg(m, jnp.where(s_idx < t_idx, 1.0, 0.0).astype(BF16)))
        return jnp.concatenate(cols, axis=1)

    sel = jnp.logical_or(gt, jnp.logical_and(eq, tokens_before(eq) < need))
    pos_ref[...] = jnp.where(sel, tokens_before(sel).astype(jnp.int32), -1).reshape(nb, n_exp, n)
    gate_ref[...] = jnp.where(sel, a, 0.0).reshape(nb, n_exp, n)


def _select(aff_t, tok0, n, cap):
    b, ne, _ = aff_t.shape
    blk = tok0 // n
    nb = SELECT_BATCH if b % SELECT_BATCH == 0 else 1
    return pl.pallas_call(
        functools.partial(_select_kernel, cap=cap),
        grid=(b // nb,),
        in_specs=[pl.BlockSpec((nb, ne, n), lambda i: (i, 0, blk))],
        out_specs=[pl.BlockSpec((nb, ne, n), lambda i: (i, 0, 0)), pl.BlockSpec((nb, ne, n), lambda i: (i, 0, 0))],
        out_shape=[jax.ShapeDtypeStruct((b, ne, n), jnp.int32), jax.ShapeDtypeStruct((b, ne, n), F32)],
        compiler_params=_params("parallel"),
        name="moe_select",
    )(aff_t)


def _expert_kernel(h_ref, pos_ref, gate_ref, wg_ref, wu_ref, wd_ref, o_ref, *, cap):
    e = pl.program_id(1)

    @pl.when(e == 0)
    def _():
        o_ref[...] = jnp.zeros_like(o_ref)

    nb, n, _ = h_ref.shape
    slot = lax.broadcasted_iota(jnp.int32, (cap, n), 0)
    onehots, xes, gates = [], [], []
    for i in range(nb):
        hit = pos_ref[i, 0] == slot
        onehot = jnp.where(hit, 1.0, 0.0).astype(BF16)
        onehots.append(onehot)
        xes.append(_dg(onehot, h_ref[i]).astype(BF16))
        gates.append(jnp.sum(jnp.where(hit, gate_ref[i, 0], 0.0), axis=1, keepdims=True))
    xe = jnp.concatenate(xes, axis=0)
    hg = _dg(xe, wg_ref[0])
    hu = _dg(xe, wu_ref[0])
    hid = (hg * _sigmoid(hg) * hu).astype(BF16)
    ye = (_dg(hid, wd_ref[0]) * jnp.concatenate(gates, axis=0)).astype(BF16)
    for i in range(nb):
        o_ref[i] += _dg(onehots[i], ye[i * cap:(i + 1) * cap], TN)


def _experts(h, pos, gate, wg, wu, wd, layer, tok0, n, cap):
    b, s, d = h.shape
    ne = pos.shape[1]
    f = wg.shape[2]
    blk = tok0 // n
    nb = max(1, min(b // 2, EXPERT_ROWS // cap))
    assert b % nb == 0
    sel = pl.BlockSpec((nb, 1, 1, n), lambda i, e: (i, e, 0, 0))
    return pl.pallas_call(
        functools.partial(_expert_kernel, cap=cap),
        grid=(b // nb, ne),
        in_specs=[pl.BlockSpec((nb, n, d), lambda i, e: (i, blk, 0)), sel, sel,
                  pl.BlockSpec((1, d, f), lambda i, e: (layer * ne + e, 0, 0)),
                  pl.BlockSpec((1, d, f), lambda i, e: (layer * ne + e, 0, 0)),
                  pl.BlockSpec((1, f, d), lambda i, e: (layer * ne + e, 0, 0))],
        out_specs=pl.BlockSpec((nb, n, d), lambda i, e: (i, 0, 0)),
        out_shape=jax.ShapeDtypeStruct((b, n, d), F32),
        compiler_params=_params("parallel", "arbitrary"),
        name="moe_experts",
    )(h, pos.reshape(b, ne, 1, n), gate.reshape(b, ne, 1, n), wg, wu, wd)


def _final_kernel(x_ref, m_ref, mt_ref, g_ref, o_ref):
    o_ref[0] = _rms(x_ref[0] + mt_ref[0, 0, 5:6] * m_ref[0]) * g_ref[...]


def _final_residual_norm(xc, moe, mt, gain):
    b, s, d = xc.shape
    n_lat = moe.shape[1]
    tok = pl.BlockSpec((1, ROW_TILE, d), lambda i, j: (i, j, 0))
    return pl.pallas_call(
        _final_kernel,
        grid=(b, n_lat // ROW_TILE),
        in_specs=[tok, tok, pl.BlockSpec((1, 1, 6, d), lambda i, j: (i, 0, 0, 0)),
                  pl.BlockSpec((1, d), lambda i, j: (0, 0))],
        out_specs=tok,
        out_shape=jax.ShapeDtypeStruct((b, n_lat, d), F32),
        compiler_params=_params("parallel", "parallel"),
        name="final_norm",
    )(xc, moe, mt, gain)


def _rope_tables(n_lat, n_ctx):
    rows = n_lat // GRID_W
    row = jnp.repeat(jnp.arange(rows, dtype=F32), GRID_W)
    col = jnp.tile(jnp.arange(GRID_W, dtype=F32), rows)
    axis_dim = HEAD_DIM // 2
    inv_freq = ROPE_THETA ** (-jnp.arange(0, axis_dim, 2, dtype=F32) / axis_dim)
    ang_r = row[:, None] * inv_freq[None, :]
    ang_c = col[:, None] * inv_freq[None, :]
    cos_h = jnp.concatenate([jnp.cos(ang_r), jnp.cos(ang_r), jnp.cos(ang_c), jnp.cos(ang_c)], axis=1)
    sin_h = jnp.concatenate([-jnp.sin(ang_r), jnp.sin(ang_r), -jnp.sin(ang_c), jnp.sin(ang_c)], axis=1)
    cos_h = jnp.concatenate([cos_h, jnp.ones((n_ctx, HEAD_DIM), F32)], axis=0)
    sin_h = jnp.concatenate([sin_h, jnp.zeros((n_ctx, HEAD_DIM), F32)], axis=0)
    return jnp.tile(cos_h, (1, ATT_HEADS)), jnp.tile(sin_h, (1, ATT_HEADS))


def _head_sum_matrix(width):
    idx = np.arange(width) // HEAD_DIM
    return jnp.asarray(idx[:, None] == idx[None, :], dtype=BF16)


def _two_dir_lowrank(w2):
    _, rank, w = w2.shape
    z = jnp.zeros((rank, w), w2.dtype)
    return jnp.concatenate([jnp.concatenate([w2[0], z], axis=1), jnp.concatenate([z, w2[1]], axis=1)], axis=0)


def kernel(x, c, ctx, c_ctx, ada_w, ada_b, norm1, w_in, q_gain, k_gain, shift_mu, decay_w0, decay_w2, iclr_a0, iclr_a2, gate_g2, rwkv_kk, rwkv_ka, rwkv_rk, rwkv_gn_w, rwkv_gn_b, conv_w, w_br_att, w_br_rwkv, w_br_conv, w_out, norm2, w_router, exp_gate, exp_up, exp_down, final_norm):
    b, n_lat, d = x.shape
    n_ctx = ctx.shape[1]
    depth = ada_w.shape[0]
    assert n_lat % ROW_TILE == 0 and n_ctx % ROW_TILE == 0 and n_lat % n_ctx == 0
    n_lat_tiles = n_lat // ROW_TILE

    pad = (-(b + 1)) % 8
    cc = jnp.concatenate([c, c_ctx[None, :], jnp.zeros((pad, d), F32)], axis=0)
    mods = _ada_table(cc, ada_w, ada_b)

    cos_t, sin_t = _rope_tables(n_lat, n_ctx)
    seg = _head_sum_matrix(ATT_WIDTH)
    att_w = ATT_WIDTH + 2 * ATT_KV_WIDTH
    cv_w = 3 * CONV_WIDTH
    offs = np.cumsum([0, att_w, RWKV_SEG, cv_w, 3 * d])

    group_scale = (1.0, 1.0, 1.0, 0.5)
    w_groups = [(w_in[:, :, offs[i]:offs[i + 1]] * group_scale[i]).astype(BF16) for i in range(4)]
    w_tail = [t.astype(BF16) for t in (w_br_att, w_br_rwkv, w_br_conv, 0.5 * w_out)]
    ne, f = exp_gate.shape[1], exp_gate.shape[3]
    wg = exp_gate.astype(BF16).reshape(depth * ne, d, f)
    wu = exp_up.astype(BF16).reshape(depth * ne, d, f)
    wd = exp_down.astype(BF16).reshape(depth * ne, f, d)

    xc = jnp.concatenate([x, ctx], axis=1)
    pending = None
    for l in range(depth):
        mod_lat = mods[l, :b].reshape(b, 1, 6, d)
        mod_ctx = jnp.broadcast_to(mods[l, b].reshape(1, 1, 6, d), (b, 1, 6, d))
        mt = jnp.concatenate([mod_lat, mod_ctx], axis=1)

        gain1 = norm1[l].reshape(1, d)
        xc, qvk, rw, cv, gt = _in_proj(
            xc, pending, gain1, mt, cos_t, sin_t, jnp.tile(q_gain[l], ATT_HEADS).reshape(1, -1),
            jnp.tile(k_gain[l], ATT_KV_HEADS).reshape(1, -1), seg, l, w_groups, n_lat_tiles)
        o_att = _attention(qvk, n_lat)

        rv, kg, nkk, lw, kd, bb = _rwkv_prep(
            rw, shift_mu[l].reshape(1, -1), _two_dir_lowrank(decay_w2[l]), decay_w0[l].reshape(1, -1),
            _two_dir_lowrank(iclr_a2[l]), iclr_a0[l].reshape(1, -1), gate_g2[l],
            rwkv_kk[l].reshape(1, -1), rwkv_ka[l].reshape(1, -1), seg, n_lat_tiles)
        y_f, y_b = _rwkv_scan(rv, nkk, lw, kd, bb, n_lat)

        xc, h2, aff_t = _merge(
            xc, o_att, y_f, y_b, rv, kg, cv, gt, mt, rwkv_rk[l].reshape(1, -1), rwkv_gn_w[l].reshape(1, -1),
            rwkv_gn_b[l].reshape(1, -1), seg, conv_w[l], norm2[l].reshape(1, d), w_router[l].T, l, w_tail,
            n_lat_tiles)
        streams = [(0, n_lat)] + ([(n_lat, n_ctx)] if l < depth - 1 else [])
        moes = []
        for tok0, n in streams:
            cap = CAPACITY_FACTOR * n // N_EXPERTS
            pos, gate = _select(aff_t, tok0, n, cap)
            moes.append(_experts(h2, pos, gate, wg, wu, wd, l, tok0, n, cap))
        if l == depth - 1:
            return _final_residual_norm(xc, moes[0], mt, final_norm.reshape(1, d))
        pending = (moes[0], moes[1], mt)
```

```python
import functools

import numpy as np
import jax
import jax.numpy as jnp
from jax import lax
from jax.experimental import pallas as pl
from jax.experimental.pallas import tpu as pltpu

F32 = jnp.float32
BF16 = jnp.bfloat16

GRID_W = 64
NORM_EPS = 1e-6
LOG2_E = float(np.log2(np.e))
ATT_HEADS = 8
ATT_KV_HEADS = 2
HEAD_DIM = 64
ATT_GROUP = ATT_HEADS // ATT_KV_HEADS
ATT_WIDTH = ATT_HEADS * HEAD_DIM
ATT_KV_WIDTH = ATT_KV_HEADS * HEAD_DIM
ROPE_THETA = 10000.0
RWKV_HEADS = 8
RWKV_WIDTH = RWKV_HEADS * HEAD_DIM
DECAY_RANK = 64
ICLR_RANK = 64
GATE_RANK = 128
RWKV_GN_EPS = 64e-5
RWKV_SEG = 3 * RWKV_WIDTH + 2 * DECAY_RANK + 2 * ICLR_RANK + GATE_RANK
CONV_WIDTH = 512
N_EXPERTS = 16
CAPACITY_FACTOR = 2

ROW_TILE = 256
ATT_TILE = 512
CHUNK = 64
QUAD = 4 * HEAD_DIM
SCAN_BATCH = 4
SCAN_CHUNKS = 4
TOKEN_BATCH = 2
HALO = 8
SELECT_BLOCK = 256
SELECT_BATCH = 4
EXPERT_ROWS = 256
VMEM_LIMIT = 56 * 1024 * 1024

NT = (((1,), (1,)), ((), ()))
TN = (((0,), (0,)), ((), ()))
NN = (((1,), (0,)), ((), ()))


def _params(*sem):
    return pltpu.CompilerParams(dimension_semantics=sem, vmem_limit_bytes=VMEM_LIMIT)


def _layer_spec(a, layer):
    return pl.BlockSpec((1,) + a.shape[1:], lambda *_: (layer,) + (0,) * (a.ndim - 1))


def _dg(a, b, dn=NN):
    return lax.dot_general(a, b, dn, preferred_element_type=F32)


def _split2(x):
    hi = x.astype(BF16)
    lo = (x - hi.astype(F32)).astype(BF16)
    return hi, lo


def _mm1(a, b, dn=NN):
    return _dg(a.astype(BF16), b.astype(BF16), dn)


def _mm3(a, b, dn=NN):
    ah, al = _split2(a)
    bh, bl = _split2(b)
    return _dg(ah, bh, dn) + (_dg(ah, bl, dn) + _dg(al, bh, dn))


def _mm_exact_lhs(a_bf16, b, dn=NN):
    hi, lo = _split2(b)
    return _dg(a_bf16, hi, dn) + _dg(a_bf16, lo, dn)


def _head_sum(x, seg):
    return _dg(x.astype(BF16), seg)


def _sigmoid(x):
    return 0.5 * jnp.tanh(0.5 * x) + 0.5


def _rms(x):
    return x * lax.rsqrt(jnp.mean(x * x, axis=-1, keepdims=True) + NORM_EPS)


def _ada_kernel(c_ref, w_ref, b_ref, o_ref):
    c = c_ref[...]
    s = c * _sigmoid(c)
    o_ref[0] = _mm3(s, w_ref[0]) + b_ref[0]


def _ada_table(cc, ada_w, ada_b):
    n_layers, d, six_d = ada_w.shape
    rows = cc.shape[0]
    tn = six_d // 4
    return pl.pallas_call(
        _ada_kernel,
        grid=(n_layers, six_d // tn),
        in_specs=[pl.BlockSpec((rows, d), lambda l, j: (0, 0)),
                  pl.BlockSpec((1, d, tn), lambda l, j: (l, 0, j)),
                  pl.BlockSpec((1, 1, tn), lambda l, j: (l, 0, j))],
        out_specs=pl.BlockSpec((1, rows, tn), lambda l, j: (l, 0, j)),
        out_shape=jax.ShapeDtypeStruct((n_layers, rows, six_d), F32),
        compiler_params=_params("parallel", "parallel"),
        name="ada_table",
    )(cc, ada_w, ada_b.reshape(n_layers, 1, six_d))


def _swap16(x):
    w = x.shape[-1]
    lane = lax.broadcasted_iota(jnp.int32, x.shape, x.ndim - 1)
    return jnp.where((lane & 16) == 0, pltpu.roll(x, w - 16, x.ndim - 1), pltpu.roll(x, 16, x.ndim - 1))


def _att_heads(a, cos, sin, q_gain, k_gain, seg):
    def norm_rope(u, gain, cos_u, sin_u, seg_u):
        ms = _head_sum(u * u, seg_u) * (1.0 / HEAD_DIM)
        un = u * lax.rsqrt(ms + NORM_EPS) * gain
        return un * cos_u + _swap16(un) * sin_u

    kw = ATT_KV_WIDTH
    q = norm_rope(a[:, :ATT_WIDTH], q_gain, cos, sin, seg)
    k = norm_rope(a[:, ATT_WIDTH:ATT_WIDTH + kw], k_gain, cos[:, :kw], sin[:, :kw], seg[:kw, :kw])
    q = (q * (HEAD_DIM ** -0.5 * LOG2_E)).astype(BF16)
    ones = jnp.ones((a.shape[0], HEAD_DIM), BF16)
    v = a[:, ATT_WIDTH + kw:].astype(BF16)
    v1 = jnp.concatenate([t for h in range(ATT_KV_HEADS) for t in (v[:, h * HEAD_DIM:(h + 1) * HEAD_DIM], ones)],
                         axis=1)
    return q, k.astype(BF16), v1


def _in_proj_kernel(*refs, n_lat_tiles, has_pending):
    if has_pending:
        x_ref, ml_ref, mc_ref, mtp_ref = refs[:4]
        refs = refs[4:]
        xo_ref, refs = refs[-5], refs[:-5] + refs[-4:]
    else:
        x_ref, refs = refs[0], refs[1:]
    gain_ref, mt_ref, cos_ref, sin_ref, qg_ref, kg_ref, seg_ref, wa_ref, wr_ref, wc_ref, wg_ref = refs[:11]
    qvk_ref, rw_ref, cv_ref, gt_ref = refs[11:]
    is_latent = pl.program_id(1) < n_lat_tiles
    for i in range(x_ref.shape[0]):
        x = x_ref[i]
        if has_pending:
            x = x + mtp_ref[i, 0, 5:6] * jnp.where(is_latent, ml_ref[i], mc_ref[i])
            xo_ref[i] = x
        h = _rms(x) * gain_ref[...]
        h = (h * (1.0 + mt_ref[i, 0, 1:2]) + mt_ref[i, 0, 0:1]).astype(BF16)
        q, k, v1 = _att_heads(_dg(h, wa_ref[0]), cos_ref[...], sin_ref[...], qg_ref[...], kg_ref[...], seg_ref[...])
        qvk_ref[i] = jnp.concatenate([q, v1, k], axis=1)
        rw_ref[i] = _dg(h, wr_ref[0]).astype(BF16)
        cv_ref[i] = _dg(h, wc_ref[0]).astype(BF16)
        gt_ref[i] = _dg(h, wg_ref[0]).astype(BF16)


def _in_proj(xc, pending, gain, mt, cos_t, sin_t, q_gain_t, k_gain_t, seg, layer, weights, n_lat_tiles):
    b, s, d = xc.shape
    nb = TOKEN_BATCH if pending is None else 1
    tok = lambda n: pl.BlockSpec((nb, ROW_TILE, n), lambda i, j: (i, j, 0))
    full = lambda a: pl.BlockSpec(a.shape, lambda i, j: (0,) * a.ndim)
    table = pl.BlockSpec((ROW_TILE, ATT_WIDTH), lambda i, j: (j, 0))
    mods = lambda: pl.BlockSpec((nb, 1, 6, d), lambda i, j: (i, j // n_lat_tiles, 0, 0))
    widths = [ATT_WIDTH + 3 * ATT_KV_WIDTH] + [w.shape[2] for w in weights[1:]]
    in_specs, operands = [tok(d)], [xc]
    out_specs = [tok(n) for n in widths]
    out_shape = [jax.ShapeDtypeStruct((b, s, n), BF16) for n in widths]
    if pending is not None:
        moe_lat, moe_ctx, mt_prev = pending
        in_specs += [pl.BlockSpec((nb, ROW_TILE, d), lambda i, j: (i, jnp.minimum(j, n_lat_tiles - 1), 0)),
                     pl.BlockSpec((nb, ROW_TILE, d), lambda i, j: (i, jnp.maximum(j - n_lat_tiles, 0), 0)), mods()]
        operands += [moe_lat, moe_ctx, mt_prev]
        out_specs = [tok(d)] + out_specs
        out_shape = [jax.ShapeDtypeStruct((b, s, d), F32)] + out_shape
    in_specs += [full(gain), mods(), table, table, full(q_gain_t), full(k_gain_t), full(seg)]
    in_specs += [_layer_spec(w, layer) for w in weights]
    operands += [gain, mt, cos_t, sin_t, q_gain_t, k_gain_t, seg, *weights]
    outs = pl.pallas_call(
        functools.partial(_in_proj_kernel, n_lat_tiles=n_lat_tiles, has_pending=pending is not None),
        grid=(b // nb, s // ROW_TILE),
        in_specs=in_specs,
        out_specs=out_specs,
        out_shape=out_shape,
        compiler_params=_params("parallel", "parallel"),
        name="in_proj",
    )(*operands)
    return outs if pending is not None else [xc] + list(outs)


def _attn_kernel(q_ref, k_ref, v_ref, o_ref):
    def scores(h):
        kvh = h // ATT_GROUP
        return _dg(q_ref[0, :, h * HEAD_DIM:(h + 1) * HEAD_DIM], k_ref[0, :, kvh * HEAD_DIM:(kvh + 1) * HEAD_DIM], NT)

    s_next = scores(0)
    for h in range(ATT_HEADS):
        s = s_next
        if h + 1 < ATT_HEADS:
            s_next = scores(h + 1)
        kvh = h // ATT_GROUP
        v1 = v_ref[0, :, 2 * kvh * HEAD_DIM:2 * (kvh + 1) * HEAD_DIM]
        p = jnp.exp2((s - jnp.max(s, axis=-1, keepdims=True)).astype(BF16))
        ov = _dg(p, v1)
        o = ov[:, :HEAD_DIM] / ov[:, HEAD_DIM:]
        o_ref[0, :, h * HEAD_DIM:(h + 1) * HEAD_DIM] = o.astype(BF16)


def _attention(qvk, n_lat):
    b, s, _ = qvk.shape
    n_ctx = s - n_lat
    v_w = 2 * ATT_KV_WIDTH
    k_col, v_col = (ATT_WIDTH + v_w) // ATT_KV_WIDTH, ATT_WIDTH // v_w
    tile = ATT_TILE if n_lat % ATT_TILE == 0 else ROW_TILE
    o_lat = pl.pallas_call(
        _attn_kernel,
        grid=(b, n_lat // tile),
        in_specs=[pl.BlockSpec((1, tile, ATT_WIDTH), lambda i, j: (i, j, 0)),
                  pl.BlockSpec((1, s, ATT_KV_WIDTH), lambda i, j: (i, 0, k_col)),
                  pl.BlockSpec((1, s, v_w), lambda i, j: (i, 0, v_col))],
        out_specs=pl.BlockSpec((1, tile, ATT_WIDTH), lambda i, j: (i, j, 0)),
        out_shape=jax.ShapeDtypeStruct((b, n_lat, ATT_WIDTH), BF16),
        compiler_params=_params("parallel", "parallel"),
        name="attention",
    )(qvk, qvk, qvk)
    ctx_blk = n_lat // n_ctx
    o_ctx = pl.pallas_call(
        _attn_kernel,
        grid=(b,),
        in_specs=[pl.BlockSpec((1, n_ctx, ATT_WIDTH), lambda i: (i, ctx_blk, 0)),
                  pl.BlockSpec((1, n_ctx, ATT_KV_WIDTH), lambda i: (i, ctx_blk, k_col)),
                  pl.BlockSpec((1, n_ctx, v_w), lambda i: (i, ctx_blk, v_col))],
        out_specs=pl.BlockSpec((1, n_ctx, ATT_WIDTH), lambda i: (i, 0, 0)),
        out_shape=jax.ShapeDtypeStruct((b, n_ctx, ATT_WIDTH), BF16),
        compiler_params=_params("parallel"),
        name="attention_ctx",
    )(qvk, qvk, qvk)
    return o_lat, o_ctx


def _neighbours(x, halo_prev, halo_next, first, last):
    rows = x.shape[0]
    ridx = lax.broadcasted_iota(jnp.int32, x.shape, 0)
    row_p = jnp.where(first, 0.0, halo_prev[halo_prev.shape[0] - 1:])
    row_n = jnp.where(last, 0.0, halo_next[0:1])
    prev = jnp.where(ridx == 0, row_p, pltpu.roll(x, 1, 0))
    nxt = jnp.where(ridx == rows - 1, row_n, pltpu.roll(x, rows - 1, 0))
    return prev, nxt


def _tile_ends(j, n_lat_tiles, n_tiles):
    first = jnp.logical_or(j == 0, j == n_lat_tiles)
    last = jnp.logical_or(j == n_lat_tiles - 1, j == n_tiles - 1)
    return first, last


def _halo_specs(width, n_tiles, dtype):
    halo = HALO * (4 // jnp.dtype(dtype).itemsize)
    per = ROW_TILE // halo
    nb = TOKEN_BATCH
    prev = pl.BlockSpec((nb, halo, width), lambda i, j: (i, jnp.maximum(j * per - 1, 0), 0))
    nxt = pl.BlockSpec((nb, halo, width), lambda i, j: (i, jnp.minimum((j + 1) * per, n_tiles * per - 1), 0))
    return prev, nxt


def _rwkv_prep_kernel(x_ref, hp_ref, hn_ref, mu_ref, w2_ref, w0_ref, a2_ref, a0_ref, g2_ref, kk_ref, ka_ref,
                      seg_ref, rv_ref, kg_ref, nkk_ref, lw_ref, kd_ref, bb_ref, *, n_lat_tiles, n_tiles):
    j = pl.program_id(1)
    first, last = _tile_ends(j, n_lat_tiles, n_tiles)
    w = RWKV_WIDTH
    halo = hp_ref.shape[1]
    rows = x_ref.shape[1]
    t_idx = lax.broadcasted_iota(jnp.int32, (rows, rows + 2 * halo), 0) + halo
    u_idx = lax.broadcasted_iota(jnp.int32, (rows, rows + 2 * halo), 1)
    is_prev = jnp.logical_and(u_idx == t_idx - 1, jnp.logical_or(u_idx >= halo, jnp.logical_not(first)))
    is_next = jnp.logical_and(u_idx == t_idx + 1, jnp.logical_or(u_idx < halo + rows, jnp.logical_not(last)))
    band = jnp.where(jnp.logical_or(is_prev, is_next), 0.5, 0.0).astype(BF16)
    for i in range(x_ref.shape[0]):
        x = x_ref[i].astype(F32)
        mean_nb = _dg(band, jnp.concatenate([hp_ref[i], x_ref[i], hn_ref[i]], axis=0))
        xs = x + mu_ref[...] * (mean_nb - x)
        r = xs[:, 0:w]
        k = xs[:, w:2 * w]
        v = xs[:, 2 * w:3 * w]
        o = 3 * w
        w_lo = xs[:, o:o + 2 * DECAY_RANK]
        a_lo = xs[:, o + 2 * DECAY_RANK:o + 2 * DECAY_RANK + 2 * ICLR_RANK]
        g_lo = xs[:, o + 2 * DECAY_RANK + 2 * ICLR_RANK:]
        w_pre = w0_ref[...] + _mm1(jnp.tanh(w_lo), w2_ref[...])
        logw = -_sigmoid(w_pre) * float(np.exp(-0.5))
        a = _sigmoid(a0_ref[...] + _mm1(a_lo, a2_ref[...]))
        g = _mm1(_sigmoid(g_lo), g2_ref[...])
        kk = k * kk_ref[...]
        kk = kk * lax.rsqrt(_head_sum(kk * kk, seg_ref[...]) + 1e-12)
        for c, (t_scan, t_tail) in enumerate(((r, k), (v, g))):
            rv_ref[i, :, c * w:(c + 1) * w] = t_scan.astype(BF16)
            kg_ref[i, :, c * w:(c + 1) * w] = t_tail.astype(BF16)
        nkk_ref[i] = (-kk).astype(BF16)
        for d in range(2):
            a_d = a[:, d * w:(d + 1) * w]
            lw_ref[i, d] = logw[:, d * w:(d + 1) * w]
            kd_ref[i, d] = (k * (1.0 + (a_d - 1.0) * ka_ref[...])).astype(BF16)
            bb_ref[i, d] = (kk * a_d).astype(BF16)


def _rwkv_prep(rw, mu, w2cat, w0, a2cat, a0, g2, k_k, k_a, seg, n_lat_tiles):
    b, s, wseg = rw.shape
    n_tiles = s // ROW_TILE
    w = RWKV_WIDTH
    nb = TOKEN_BATCH
    hp, hn = _halo_specs(wseg, n_tiles, rw.dtype)
    full = lambda shape: pl.BlockSpec(shape, lambda i, j: (0,) * len(shape))
    tok = pl.BlockSpec((nb, ROW_TILE, w), lambda i, j: (i, j, 0))
    tok2 = pl.BlockSpec((nb, 2, ROW_TILE, w), lambda i, j: (i, 0, j, 0))
    pair = pl.BlockSpec((nb, ROW_TILE, 2 * w), lambda i, j: (i, j, 0))
    one = jax.ShapeDtypeStruct((b, s, w), BF16)
    two = jax.ShapeDtypeStruct((b, 2, s, w), BF16)
    two_f32 = jax.ShapeDtypeStruct((b, 2, s, w), F32)
    return pl.pallas_call(
        functools.partial(_rwkv_prep_kernel, n_lat_tiles=n_lat_tiles, n_tiles=n_tiles),
        grid=(b // nb, n_tiles),
        in_specs=[pl.BlockSpec((nb, ROW_TILE, wseg), lambda i, j: (i, j, 0)), hp, hn,
                  full((1, wseg)), full(w2cat.shape), full((1, 2 * w)), full(a2cat.shape), full((1, 2 * w)),
                  full(g2.shape), full((1, w)), full((1, w)), full(seg.shape)],
        out_specs=[pair, pair, tok, tok2, tok2, tok2],
        out_shape=[jax.ShapeDtypeStruct((b, s, 2 * w), BF16)] * 2 + [one, two_f32, two, two],
        compiler_params=_params("parallel", "parallel"),
        name="rwkv_prep",
    )(rw, rw, rw, mu, w2cat, w0, a2cat, a0, g2, k_k, k_a, seg)


def _block_diag(y, head_masks):
    return jnp.concatenate([jnp.where(m, y, 0.0) for m in head_masks], axis=0)


def _scan_chain(sgn, r, v, nkk, lw, kd, bb, s0):
    row = lax.broadcasted_iota(jnp.int32, (CHUNK, QUAD), 0)
    lane = lax.broadcasted_iota(jnp.int32, (CHUNK, QUAD), 1)
    rel = ((lane % CHUNK) - row) * sgn
    strict = rel < 0
    incl = rel <= 0
    eye = jnp.where(rel == 0, 1.0, 0.0)
    head_masks = [(lane // HEAD_DIM) == h for h in range(4)]
    bd_mask = (lax.broadcasted_iota(jnp.int32, (QUAD, QUAD), 0) // HEAD_DIM
               == lax.broadcasted_iota(jnp.int32, (QUAD, QUAD), 1) // HEAD_DIM)
    t_row = lax.broadcasted_iota(jnp.int32, (CHUNK, CHUNK), 0)
    t_col = lax.broadcasted_iota(jnp.int32, (CHUNK, CHUNK), 1)
    tri = jnp.where((t_col - t_row) * sgn <= 0, 1.0, 0.0).astype(BF16)
    last_row = CHUNK - 1 if sgn > 0 else 0
    bd = lambda t: _block_diag(t, head_masks).astype(BF16)

    cum = _mm_exact_lhs(tri, lw)
    yield
    cum_end = cum[last_row:last_row + 1]
    w_inv = jnp.exp(-cum)
    w_rem = jnp.exp(cum_end - cum)
    ar = jnp.concatenate([nkk * jnp.exp(cum - lw), r * jnp.exp(cum)], axis=0).astype(BF16)
    xb = _dg(ar, bd(bb * w_inv), NT)
    xk = _dg(ar, bd(kd * w_inv), NT)
    ars = _dg(ar, s0.astype(BF16), NT)
    yield
    n_ab = jnp.where(strict, xb[:CHUNK], 0.0)
    l_ak = jnp.where(strict, xk[:CHUNK], 0.0)
    g_rb = jnp.where(incl, xb[CHUNK:], 0.0)
    g_rk = jnp.where(incl, xk[CHUNK:], 0.0)

    m = eye + n_ab
    p = _dg(n_ab.astype(BF16), bd(n_ab))
    lg = _dg(jnp.concatenate([l_ak, g_rk], axis=0).astype(BF16), bd(v))
    yield
    for _ in range(int(np.log2(CHUNK)) - 2):
        both = _dg(jnp.concatenate([p, m], axis=0).astype(BF16), bd(p))
        yield
        p = both[:CHUNK]
        m = m + both[CHUNK:]
    m = m + _dg(m.astype(BF16), bd(p))
    yield
    x0 = ars[:CHUNK] + lg[:CHUNK]
    u = _dg(m.astype(BF16), bd(x0))
    yield
    y = ars[CHUNK:] + _dg(g_rb.astype(BF16), bd(u)) + lg[CHUNK:]
    z = _dg(jnp.concatenate([u, v], axis=0).astype(BF16),
            jnp.concatenate([bb * w_rem, kd * w_rem], axis=0).astype(BF16), TN)
    yield
    return y, s0 * jnp.exp(cum_end) + jnp.where(bd_mask, z, 0.0)


def _run_interleaved(chains, stagger=0):
    results = [None] * len(chains)
    live = list(enumerate(chains))
    rnd = 0
    while live:
        still = []
        for idx, g in live:
            if rnd >= idx * stagger:
                try:
                    next(g)
                except StopIteration as stop:
                    results[idx] = stop.value
                    continue
            still.append((idx, g))
        live = still
        rnd += 1
    return results


def _rwkv_scan_kernel(pf_ref, nf_ref, lwf_ref, kdf_ref, bbf_ref, pb_ref, nb_ref, lwb_ref, kdb_ref, bbb_ref,
                      yf_ref, yb_ref, s_ref):
    @pl.when(pl.program_id(1) == 0)
    def _():
        s_ref[...] = jnp.zeros_like(s_ref)

    w = RWKV_WIDTH
    f32 = lambda t: t.astype(F32)
    dirs = ((1, pf_ref, nf_ref, lwf_ref, kdf_ref, bbf_ref, yf_ref),
            (-1, pb_ref, nb_ref, lwb_ref, kdb_ref, bbb_ref, yb_ref))
    keys = [(i, d, q) for i in range(SCAN_BATCH) for d in range(2) for q in range(w // QUAD)]
    state = {key: s_ref[key] for key in keys}
    for step in range(SCAN_CHUNKS):
        work = []
        for i, d, q in keys:
            sgn, p_ref, n_ref, lw_ref, kd_ref, bb_ref, y_ref = dirs[d]
            c = step if sgn > 0 else SCAN_CHUNKS - 1 - step
            rows = slice(c * CHUNK, (c + 1) * CHUNK)
            ql = slice(q * QUAD, (q + 1) * QUAD)
            vl = slice(w + q * QUAD, w + (q + 1) * QUAD)
            args = (f32(p_ref[i, rows, ql]), f32(p_ref[i, rows, vl]), f32(n_ref[i, rows, ql]),
                    lw_ref[i, 0, rows, ql], f32(kd_ref[i, 0, rows, ql]), f32(bb_ref[i, 0, rows, ql]), state[(i, d, q)])
            work.append((sgn, args, y_ref, (i, rows, ql)))
        results = _run_interleaved([_scan_chain(sgn, *args) for sgn, args, _, _ in work])
        for key, (_, _, y_ref, y_idx), (y, s_new) in zip(keys, work, results):
            y_ref[y_idx] = y.astype(y_ref.dtype)
            state[key] = s_new
    for key in keys:
        s_ref[key] = state[key]


def _rwkv_scan(rv, nkk, lw, kd, bb, n_lat):
    b, s, w = nkk.shape
    rows = SCAN_CHUNKS * CHUNK
    assert s % rows == 0 and n_lat % rows == 0 and b % SCAN_BATCH == 0
    nc = s // rows
    nc_lat = n_lat // rows
    nc_ctx = nc - nc_lat
    nb = SCAN_BATCH

    fwd = lambda c: jnp.where(c < nc_ctx, nc_lat + c, c - nc_ctx)
    bwd = lambda c: jnp.where(c < nc_ctx, nc - 1 - c, nc_lat - 1 - (c - nc_ctx))
    tok = lambda order, a: pl.BlockSpec((nb, rows, a.shape[2]), lambda i, c: (i, order(c), 0))
    tok2 = lambda order, d: pl.BlockSpec((nb, 1, rows, w), lambda i, c: (i, d, order(c), 0))
    out = jax.ShapeDtypeStruct((b, s, w), BF16)
    return pl.pallas_call(
        _rwkv_scan_kernel,
        grid=(b // nb, nc),
        in_specs=[tok(fwd, rv), tok(fwd, nkk), tok2(fwd, 0), tok2(fwd, 0), tok2(fwd, 0),
                  tok(bwd, rv), tok(bwd, nkk), tok2(bwd, 1), tok2(bwd, 1), tok2(bwd, 1)],
        out_specs=[tok(fwd, nkk), tok(bwd, nkk)],
        out_shape=[out, out],
        scratch_shapes=[pltpu.VMEM((nb, 2, w // QUAD, QUAD, QUAD), F32)],
        compiler_params=_params("parallel", "arbitrary"),
        name="rwkv_scan",
    )(rv, nkk, lw, kd, bb, rv, nkk, lw, kd, bb)


def _rwkv_readout(y, r, k, v, g, r_k, gn_w, gn_b, seg):
    mean = _head_sum(y, seg) * (1.0 / HEAD_DIM)
    yc = y - mean
    var = _head_sum(yc * yc, seg) * (1.0 / HEAD_DIM)
    yn = yc * lax.rsqrt(var + RWKV_GN_EPS)
    bonus = _head_sum(r * k * r_k, seg) * v
    return (yn * gn_w + gn_b + bonus) * g


def _route(x, gain, shift, scale, w_router_t):
    h = _rms(x) * gain
    h = h * (1.0 + scale) + shift
    logits = _mm3(w_router_t, h, NT)
    e = jnp.exp(logits - jnp.max(logits, axis=0, keepdims=True))
    return h, e / jnp.sum(e, axis=0, keepdims=True)


def _merge_kernel(x_ref, oal_ref, oac_ref, yf_ref, yb_ref, rv_ref, kg_ref, cv_ref, hp_ref, hn_ref, gt_ref, mt_ref,
                  rk_ref, gw_ref, gb_ref, seg_ref, cw_ref, gain2_ref, wrt_ref, wa_ref, wr_ref, wc_ref, wo_ref,
                  o_ref, h_ref, aff_ref, *, n_lat_tiles, n_tiles):
    j = pl.program_id(1)
    first, last = _tile_ends(j, n_lat_tiles, n_tiles)
    cw = CONV_WIDTH
    d = x_ref.shape[2]
    w = RWKV_WIDTH
    def sample(i):
        r, v = [rv_ref[i, :, c * w:(c + 1) * w].astype(F32) for c in range(2)]
        k, g = [kg_ref[i, :, c * w:(c + 1) * w].astype(F32) for c in range(2)]
        o_rw = _rwkv_readout(yf_ref[i].astype(F32) + yb_ref[i].astype(F32), r, k, v, g, rk_ref[...], gw_ref[...],
                             gb_ref[...], seg_ref[...])
        yield
        cv = cv_ref[i].astype(F32)
        hp = hp_ref[i].astype(F32)
        hn = hn_ref[i].astype(F32)
        z = cv[:, cw:2 * cw] * cv[:, 2 * cw:]
        zp = hp[:, cw:2 * cw] * hp[:, 2 * cw:]
        zn = hn[:, cw:2 * cw] * hn[:, 2 * cw:]
        z_prev, z_next = _neighbours(z, zp, zn, first, last)
        o_cv = cv[:, :cw] * (cw_ref[0:1] * z_prev + cw_ref[1:2] * z + cw_ref[2:3] * z_next)
        yield
        t = jnp.tanh(gt_ref[i].astype(F32))
        o_att = jnp.where(j < n_lat_tiles, oal_ref[i], oac_ref[i])
        m = ((1.0 + t[:, :d]) * _dg(o_att, wa_ref[0])
             + (1.0 + t[:, d:2 * d]) * _dg(o_rw.astype(BF16), wr_ref[0])
             + (1.0 + t[:, 2 * d:]) * _dg(o_cv.astype(BF16), wc_ref[0]))
        yield
        y = _dg(m.astype(BF16), wo_ref[0])
        x_new = x_ref[i] + mt_ref[i, 0, 2:3] * y
        o_ref[i] = x_new
        yield
        h, aff = _route(x_new, gain2_ref[...], mt_ref[i, 0, 3:4], mt_ref[i, 0, 4:5], wrt_ref[...])
        h_ref[i] = h.astype(BF16)
        aff_ref[i] = aff

    _run_interleaved([sample(i) for i in range(x_ref.shape[0])], stagger=2)


def _merge(xc, o_att_lat, o_att_ctx, y_f, y_b, rv, kg, cv, gt, mt, r_k, gn_w, gn_b, seg, conv_w, gain2, w_router_t, layer, stacked,
           n_lat_tiles):
    b, s, d = xc.shape
    n_tiles = s // ROW_TILE
    ne = w_router_t.shape[0]
    nb = TOKEN_BATCH
    hp, hn = _halo_specs(cv.shape[2], n_tiles, cv.dtype)
    tok = lambda a: pl.BlockSpec((nb, ROW_TILE, a.shape[2]), lambda i, j: (i, j, 0))
    att = lambda tile_of: pl.BlockSpec((nb, ROW_TILE, ATT_WIDTH), lambda i, j: (i, tile_of(j), 0))
    full = lambda a: pl.BlockSpec(a.shape, lambda i, j: (0,) * a.ndim)
    consts = (r_k, gn_w, gn_b, seg, conv_w, gain2, w_router_t)
    return pl.pallas_call(
        functools.partial(_merge_kernel, n_lat_tiles=n_lat_tiles, n_tiles=n_tiles),
        grid=(b // nb, n_tiles),
        in_specs=[tok(xc), att(lambda j: jnp.minimum(j, n_lat_tiles - 1)), att(lambda j: jnp.maximum(j - n_lat_tiles, 0))]
                 + [tok(a) for a in (y_f, y_b, rv, kg, cv)] + [hp, hn, tok(gt),
                  pl.BlockSpec((nb, 1, 6, d), lambda i, j: (i, j // n_lat_tiles, 0, 0))]
                 + [full(a) for a in consts] + [_layer_spec(a, layer) for a in stacked],
        out_specs=[tok(xc), tok(xc), pl.BlockSpec((nb, ne, ROW_TILE), lambda i, j: (i, 0, j))],
        out_shape=[jax.ShapeDtypeStruct((b, s, d), F32), jax.ShapeDtypeStruct((b, s, d), BF16),
                   jax.ShapeDtypeStruct((b, ne, s), F32)],
        compiler_params=_params("parallel", "parallel"),
        name="merge",
    )(xc, o_att_lat, o_att_ctx, y_f, y_b, rv, kg, cv, cv, cv, gt, mt, *consts, *stacked)


def _select_kernel(aff_ref, pos_ref, gate_ref, *, cap):
    nb, n_exp, n = aff_ref.shape
    a = aff_ref[...].reshape(nb * n_exp, n)
    ne = nb * n_exp
    bits = pltpu.bitcast(a, jnp.int32)

    def count(mask):
        return jnp.sum(jnp.where(mask, 1.0, 0.0), axis=1, keepdims=True)

    def body(_, carry):
        lo, hi = carry
        mid = lo + ((hi - lo + 1) >> 1)
        ok = count(bits >= mid) >= cap
        return jnp.where(ok, mid, lo), jnp.where(ok, hi, mid - 1)

    lo0 = jnp.zeros((ne, 1), jnp.int32)
    hi0 = jnp.full((ne, 1), 0x7F800000, jnp.int32)
    thr, _ = lax.fori_loop(0, 32, body, (lo0, hi0))
    gt = bits > thr
    eq = bits == thr
    need = cap - count(gt)

    def tokens_before(mask):
        m = jnp.where(mask, 1.0, 0.0).astype(BF16)
        blk = min(n, SELECT_BLOCK)
        cols = []
        for j in range(n // blk):
            s_idx = lax.broadcasted_iota(jnp.int32, (n, blk), 0)
            t_idx = lax.broadcasted_iota(jnp.int32, (n, blk), 1) + j * blk
            cols.append(_dg(m, jnp.where(s_idx < t_idx, 1.0, 0.0).astype(BF16)))
        return jnp.concatenate(cols, axis=1)

    sel = jnp.logical_or(gt, jnp.logical_and(eq, tokens_before(eq) < need))
    pos_ref[...] = jnp.where(sel, tokens_before(sel).astype(jnp.int32), -1).reshape(nb, n_exp, n)
    gate_ref[...] = jnp.where(sel, a, 0.0).reshape(nb, n_exp, n)


def _select(aff_t, tok0, n, cap):
    b, ne, _ = aff_t.shape
    blk = tok0 // n
    nb = SELECT_BATCH if b % SELECT_BATCH == 0 else 1
    return pl.pallas_call(
        functools.partial(_select_kernel, cap=cap),
        grid=(b // nb,),
        in_specs=[pl.BlockSpec((nb, ne, n), lambda i: (i, 0, blk))],
        out_specs=[pl.BlockSpec((nb, ne, n), lambda i: (i, 0, 0)), pl.BlockSpec((nb, ne, n), lambda i: (i, 0, 0))],
        out_shape=[jax.ShapeDtypeStruct((b, ne, n), jnp.int32), jax.ShapeDtypeStruct((b, ne, n), F32)],
        compiler_params=_params("parallel"),
        name="moe_select",
    )(aff_t)


def _expert_kernel(h_ref, pos_ref, gate_ref, wg_ref, wu_ref, wd_ref, o_ref, *, cap):
    e = pl.program_id(1)

    @pl.when(e == 0)
    def _():
        o_ref[...] = jnp.zeros_like(o_ref)

    nb, n, _ = h_ref.shape
    slot = lax.broadcasted_iota(jnp.int32, (cap, n), 0)
    onehots, xes, gates = [], [], []
    for i in range(nb):
        hit = pos_ref[i, 0] == slot
        onehot = jnp.where(hit, 1.0, 0.0).astype(BF16)
        onehots.append(onehot)
        xes.append(_dg(onehot, h_ref[i]).astype(BF16))
        gates.append(jnp.sum(jnp.where(hit, gate_ref[i, 0], 0.0), axis=1, keepdims=True))
    xe = jnp.concatenate(xes, axis=0)
    hg = _dg(xe, wg_ref[0])
    hu = _dg(xe, wu_ref[0])
    hid = (hg * _sigmoid(hg) * hu).astype(BF16)
    ye = (_dg(hid, wd_ref[0]) * jnp.concatenate(gates, axis=0)).astype(BF16)
    for i in range(nb):
        o_ref[i] += _dg(onehots[i], ye[i * cap:(i + 1) * cap], TN)


def _experts(h, pos, gate, wg, wu, wd, layer, tok0, n, cap):
    b, s, d = h.shape
    ne = pos.shape[1]
    f = wg.shape[2]
    blk = tok0 // n
    nb = max(1, min(b // 2, EXPERT_ROWS // cap))
    assert b % nb == 0
    sel = pl.BlockSpec((nb, 1, 1, n), lambda i, e: (i, e, 0, 0))
    return pl.pallas_call(
        functools.partial(_expert_kernel, cap=cap),
        grid=(b // nb, ne),
        in_specs=[pl.BlockSpec((nb, n, d), lambda i, e: (i, blk, 0)), sel, sel,
                  pl.BlockSpec((1, d, f), lambda i, e: (layer * ne + e, 0, 0)),
                  pl.BlockSpec((1, d, f), lambda i, e: (layer * ne + e, 0, 0)),
                  pl.BlockSpec((1, f, d), lambda i, e: (layer * ne + e, 0, 0))],
        out_specs=pl.BlockSpec((nb, n, d), lambda i, e: (i, 0, 0)),
        out_shape=jax.ShapeDtypeStruct((b, n, d), F32),
        compiler_params=_params("parallel", "arbitrary"),
        name="moe_experts",
    )(h, pos.reshape(b, ne, 1, n), gate.reshape(b, ne, 1, n), wg, wu, wd)


def _final_kernel(x_ref, m_ref, mt_ref, g_ref, o_ref):
    o_ref[0] = _rms(x_ref[0] + mt_ref[0, 0, 5:6] * m_ref[0]) * g_ref[...]


def _final_residual_norm(xc, moe, mt, gain):
    b, s, d = xc.shape
    n_lat = moe.shape[1]
    tok = pl.BlockSpec((1, ROW_TILE, d), lambda i, j: (i, j, 0))
    return pl.pallas_call(
        _final_kernel,
        grid=(b, n_lat // ROW_TILE),
        in_specs=[tok, tok, pl.BlockSpec((1, 1, 6, d), lambda i, j: (i, 0, 0, 0)),
                  pl.BlockSpec((1, d), lambda i, j: (0, 0))],
        out_specs=tok,
        out_shape=jax.ShapeDtypeStruct((b, n_lat, d), F32),
        compiler_params=_params("parallel", "parallel"),
        name="final_norm",
    )(xc, moe, mt, gain)


def _rope_tables(n_lat, n_ctx):
    rows = n_lat // GRID_W
    row = jnp.repeat(jnp.arange(rows, dtype=F32), GRID_W)
    col = jnp.tile(jnp.arange(GRID_W, dtype=F32), rows)
    axis_dim = HEAD_DIM // 2
    inv_freq = ROPE_THETA ** (-jnp.arange(0, axis_dim, 2, dtype=F32) / axis_dim)
    ang_r = row[:, None] * inv_freq[None, :]
    ang_c = col[:, None] * inv_freq[None, :]
    cos_h = jnp.concatenate([jnp.cos(ang_r), jnp.cos(ang_r), jnp.cos(ang_c), jnp.cos(ang_c)], axis=1)
    sin_h = jnp.concatenate([-jnp.sin(ang_r), jnp.sin(ang_r), -jnp.sin(ang_c), jnp.sin(ang_c)], axis=1)
    cos_h = jnp.concatenate([cos_h, jnp.ones((n_ctx, HEAD_DIM), F32)], axis=0)
    sin_h = jnp.concatenate([sin_h, jnp.zeros((n_ctx, HEAD_DIM), F32)], axis=0)
    return jnp.tile(cos_h, (1, ATT_HEADS)), jnp.tile(sin_h, (1, ATT_HEADS))


def _head_sum_matrix(width):
    idx = np.arange(width) // HEAD_DIM
    return jnp.asarray(idx[:, None] == idx[None, :], dtype=BF16)


def _two_dir_lowrank(w2):
    _, rank, w = w2.shape
    z = jnp.zeros((rank, w), w2.dtype)
    return jnp.concatenate([jnp.concatenate([w2[0], z], axis=1), jnp.concatenate([z, w2[1]], axis=1)], axis=0)


def kernel(x, c, ctx, c_ctx, ada_w, ada_b, norm1, w_in, q_gain, k_gain, shift_mu, decay_w0, decay_w2, iclr_a0, iclr_a2, gate_g2, rwkv_kk, rwkv_ka, rwkv_rk, rwkv_gn_w, rwkv_gn_b, conv_w, w_br_att, w_br_rwkv, w_br_conv, w_out, norm2, w_router, exp_gate, exp_up, exp_down, final_norm):
    b, n_lat, d = x.shape
    n_ctx = ctx.shape[1]
    depth = ada_w.shape[0]
    assert n_lat % ROW_TILE == 0 and n_ctx % ROW_TILE == 0 and n_lat % n_ctx == 0
    n_lat_tiles = n_lat // ROW_TILE

    pad = (-(b + 1)) % 8
    cc = jnp.concatenate([c, c_ctx[None, :], jnp.zeros((pad, d), F32)], axis=0)
    mods = _ada_table(cc, ada_w, ada_b)

    cos_t, sin_t = _rope_tables(n_lat, n_ctx)
    seg = _head_sum_matrix(ATT_WIDTH)
    att_w = ATT_WIDTH + 2 * ATT_KV_WIDTH
    cv_w = 3 * CONV_WIDTH
    offs = np.cumsum([0, att_w, RWKV_SEG, cv_w, 3 * d])

    group_scale = (1.0, 1.0, 1.0, 0.5)
    w_groups = [(w_in[:, :, offs[i]:offs[i + 1]] * group_scale[i]).astype(BF16) for i in range(4)]
    w_tail = [t.astype(BF16) for t in (w_br_att, w_br_rwkv, w_br_conv, 0.5 * w_out)]
    ne, f = exp_gate.shape[1], exp_gate.shape[3]
    wg = exp_gate.astype(BF16).reshape(depth * ne, d, f)
    wu = exp_up.astype(BF16).reshape(depth * ne, d, f)
    wd = exp_down.astype(BF16).reshape(depth * ne, f, d)

    xc = jnp.concatenate([x, ctx], axis=1)
    pending = None
    for l in range(depth):
        mod_lat = mods[l, :b].reshape(b, 1, 6, d)
        mod_ctx = jnp.broadcast_to(mods[l, b].reshape(1, 1, 6, d), (b, 1, 6, d))
        mt = jnp.concatenate([mod_lat, mod_ctx], axis=1)

        gain1 = norm1[l].reshape(1, d)
        xc, qvk, rw, cv, gt = _in_proj(
            xc, pending, gain1, mt, cos_t, sin_t, jnp.tile(q_gain[l], ATT_HEADS).reshape(1, -1),
            jnp.tile(k_gain[l], ATT_KV_HEADS).reshape(1, -1), seg, l, w_groups, n_lat_tiles)
        o_att_lat, o_att_ctx = _attention(qvk, n_lat)

        rv, kg, nkk, lw, kd, bb = _rwkv_prep(
            rw, shift_mu[l].reshape(1, -1), _two_dir_lowrank(decay_w2[l]), decay_w0[l].reshape(1, -1),
            _two_dir_lowrank(iclr_a2[l]), iclr_a0[l].reshape(1, -1), gate_g2[l],
            rwkv_kk[l].reshape(1, -1), rwkv_ka[l].reshape(1, -1), seg, n_lat_tiles)
        y_f, y_b = _rwkv_scan(rv, nkk, lw, kd, bb, n_lat)

        xc, h2, aff_t = _merge(
            xc, o_att_lat, o_att_ctx, y_f, y_b, rv, kg, cv, gt, mt, rwkv_rk[l].reshape(1, -1), rwkv_gn_w[l].reshape(1, -1),
            rwkv_gn_b[l].reshape(1, -1), seg, conv_w[l], norm2[l].reshape(1, d), w_router[l].T, l, w_tail,
            n_lat_tiles)
        streams = [(0, n_lat)] + ([(n_lat, n_ctx)] if l < depth - 1 else [])
        moes = []
        for tok0, n in streams:
            cap = CAPACITY_FACTOR * n // N_EXPERTS
            pos, gate = _select(aff_t, tok0, n, cap)
            moes.append(_experts(h2, pos, gate, wg, wu, wd, l, tok0, n, cap))
        if l == depth - 1:
            return _final_residual_norm(xc, moes[0], mt, final_norm.reshape(1, d))
        pending = (moes[0], moes[1], mt)
```
